```python
import math
import jax
import jax.numpy as jnp
from jax import lax
import numpy as np

D_MODEL = 1024
BATCH = 8
SEQ = 2048
DEPTH = 2

GRID_W = 64
CTX_LEN = 256
D_MIX = D_MODEL

ML_HEADS = 4
ML_DH = 64
ML_W = ML_HEADS * ML_DH
ML_CHUNK = 64
ML_CONV = 3
ML_M_INIT = -1e30
ML_COLS = 4 * ML_W + 4 * ML_HEADS

RW_HEADS = 6
RW_DH = 64
RW_W = RW_HEADS * RW_DH
RW_W_LORA = 64
RW_A_LORA = 64
RW_G_LORA = 128
RW_DECAY_SCALE = math.exp(-0.5)
RW_GN_EPS = 64e-5
RW_COLS = 3 * RW_W + 2 * RW_W_LORA + 2 * RW_A_LORA + RW_G_LORA

MLA_HEADS = 6
MLA_NOPE = 64
MLA_ROPE = 32
MLA_V = 64
MLA_Q_RANK = 384
MLA_KV_RANK = 256
MLA_W = MLA_HEADS * MLA_V
MLA_Q_BLOCK = 128
MLA_COLS = MLA_Q_RANK + MLA_KV_RANK + MLA_ROPE
MLA_SCALE = (MLA_NOPE + MLA_ROPE) ** -0.5
ROPE_AXIS_FREQS = MLA_ROPE // 4
ROPE_THETA = 10000.0

N_IN = ML_COLS + RW_COLS + MLA_COLS

PEER_HEADS = 8
PEER_NKEYS = 128
PEER_EXPERTS = PEER_NKEYS * PEER_NKEYS
PEER_DK = 128
PEER_TOPK = 16
PEER_CHUNK = 128

DEEPNORM_ALPHA = (2 * DEPTH) ** 0.25
DEEPNORM_BETA = (8 * DEPTH) ** -0.25
LN_EPS = 1e-6

kernel_name = 'hybrid_dit_mlstm_rwkv7_mla_peer'


def _ln(x, eps=LN_EPS):
    xf = x.astype(jnp.float32)
    mu = jnp.mean(xf, axis=-1, keepdims=True)
    var = jnp.mean(jnp.square(xf - mu), axis=-1, keepdims=True)
    return (xf - mu) * lax.rsqrt(var + eps)


def _ln_affine(x, g, b):
    return (_ln(x) * g + b).astype(x.dtype)


def _rms(x, g):
    xf = x.astype(jnp.float32)
    return (xf * lax.rsqrt(jnp.mean(jnp.square(xf), axis=-1, keepdims=True) + LN_EPS) * g).astype(x.dtype)


def _modulate(x, shift, scale):
    return (_ln(x) * (1.0 + scale) + shift).astype(x.dtype)


def _split(a, sizes):
    return jnp.split(a, [int(s) for s in np.cumsum(sizes)[:-1]], axis=-1)


def _heads(a, n_heads):
    b, t, _ = a.shape
    return a.reshape(b, t, n_heads, -1).transpose(0, 2, 1, 3)


def _merge_heads(a):
    b, h, t, d = a.shape
    return a.transpose(0, 2, 1, 3).reshape(b, t, h * d)


def _shift3(x):
    xp = jnp.pad(x, ((0, 0), (1, 1), (0, 0)))
    return xp[:, :-2], xp[:, 2:]


def _flip_if(a, rev, axis):
    return jnp.flip(a, axis=axis) if rev else a


def _mlstm_scan(q, k, v, li, lf, state):
    b, h, t, d = q.shape
    nc = t // ML_CHUNK

    def chunks(a):
        return jnp.moveaxis(a.reshape((b, h, nc, ML_CHUNK) + a.shape[3:]), 2, 0)

    causal = jnp.tril(jnp.ones((ML_CHUNK, ML_CHUNK), dtype=bool))

    def step(carry, inp):
        C, n, m = carry
        qc, kc, vc, ic, fc = inp
        bcum = jnp.cumsum(fc, axis=-1)
        a_inter = bcum + m[..., None]
        dmat = jnp.where(causal, bcum[..., :, None] - bcum[..., None, :] + ic[..., None, :], -jnp.inf)
        m_t = jnp.maximum(a_inter, jnp.max(dmat, axis=-1))
        w_inter = jnp.exp(a_inter - m_t)
        s = jnp.einsum('bhtd,bhsd->bhts', qc, kc) * jnp.exp(dmat - m_t[..., None])
        num = w_inter[..., None] * jnp.einsum('bhed,bhtd->bhte', C, qc) + jnp.einsum('bhts,bhse->bhte', s, vc)
        den = w_inter * jnp.einsum('bhd,bhtd->bht', n, qc) + jnp.sum(s, axis=-1)
        h_out = num / jnp.maximum(jnp.abs(den), jnp.exp(-m_t))[..., None]
        b_end = bcum[..., -1]
        g = b_end[..., None] - bcum + ic
        m_new = jnp.maximum(b_end + m, jnp.max(g, axis=-1))
        decay = jnp.exp(b_end + m - m_new)
        wk = jnp.exp(g - m_new[..., None])
        C = decay[..., None, None] * C + jnp.einsum('bhs,bhse,bhsd->bhed', wk, vc, kc)
        n = decay[..., None] * n + jnp.einsum('bhs,bhsd->bhd', wk, kc)
        return (C, n, m_new), h_out

    state, hs = lax.scan(step, state, tuple(chunks(a) for a in (q, k, v, li, lf)))
    return jnp.moveaxis(hs, 0, 2).reshape(b, h, t, d), state


def _mlstm_prep(p, conv_w, conv_b, i_bias, f_bias):
    b, t, _ = p.shape
    qk, v, o, gates = _split(p, [2 * ML_W, ML_W, ML_W, 4 * ML_HEADS])
    prev, nxt = _shift3(qk)
    qk = jax.nn.silu(prev * conv_w[0] + qk * conv_w[1] + nxt * conv_w[2] + conv_b)
    q, k = jnp.split(qk, 2, axis=-1)
    q = _heads(q, ML_HEADS).astype(jnp.float32) * ML_DH ** -0.5
    k = _heads(k, ML_HEADS).astype(jnp.float32)
    v = _heads(v, ML_HEADS).astype(jnp.float32)
    gates = gates.astype(jnp.float32).reshape(b, t, 2, 2, ML_HEADS).transpose(2, 3, 0, 4, 1)
    li = gates[:, 0] + i_bias[:, None, :, None]
    lf = jax.nn.log_sigmoid(gates[:, 1] + f_bias[:, None, :, None])
    return q, k, v, o, li, lf


def _mlstm_out(h, o, norm_g):
    return (jax.nn.sigmoid(o.astype(jnp.float32)) * _merge_heads(_ln(h)) * norm_g).astype(o.dtype)


def _mlstm_mixer(p_ctx, p_lat, ml, need_ctx):
    conv_w, conv_b, i_bias, f_bias, norm_g = ml
    qc, kc, vc, oc, lic, lfc = _mlstm_prep(p_ctx, conv_w, conv_b, i_bias, f_bias)
    ql, kl, vl, ol, lil, lfl = _mlstm_prep(p_lat, conv_w, conv_b, i_bias, f_bias)
    b = p_lat.shape[0]
    init = (jnp.zeros((b, ML_HEADS, ML_DH, ML_DH), jnp.float32),
            jnp.zeros((b, ML_HEADS, ML_DH), jnp.float32),
            jnp.full((b, ML_HEADS), ML_M_INIT, jnp.float32))
    h_ctx = 0.0
    h_lat = 0.0
    for dr in range(2):
        rev = dr == 1
        f = lambda a: _flip_if(a, rev, 2)
        hc, st = _mlstm_scan(f(qc), f(kc), f(vc), f(lic[dr]), f(lfc[dr]), init)
        hl, _ = _mlstm_scan(f(ql), f(kl), f(vl), f(lil[dr]), f(lfl[dr]), st)
        h_lat = h_lat + f(hl)
        if need_ctx:
            h_ctx = h_ctx + f(hc)
    y_ctx = _mlstm_out(h_ctx, oc, norm_g) if need_ctx else None
    return y_ctx, _mlstm_out(h_lat, ol, norm_g)


def _rwkv_prep(p, mu, w0, w_up, a0, a_up, g_up, k_k, k_a):
    b, t, _ = p.shape
    prev, nxt = _shift3(p)
    p = p + mu[0] * (prev - p) + mu[1] * (nxt - p)
    r, k, v, wd, ad, gd = _split(p, [RW_W, RW_W, RW_W, 2 * RW_W_LORA, 2 * RW_A_LORA, RW_G_LORA])
    lora = lambda z, up: jnp.einsum('btdr,drc->btdc', z.reshape(b, t, 2, -1), up)
    w = jnp.exp(-RW_DECAY_SCALE * jax.nn.sigmoid((w0 + lora(jnp.tanh(wd), w_up)).astype(jnp.float32)))
    a = jax.nn.sigmoid((a0 + lora(ad, a_up)).astype(jnp.float32))
    g = jax.nn.sigmoid(gd) @ g_up
    k = k.astype(jnp.float32)
    kk = (k * k_k).reshape(b, t, RW_HEADS, RW_DH)
    kh = (kk * lax.rsqrt(jnp.sum(kk * kk, axis=-1, keepdims=True) + 1e-12)).reshape(b, t, RW_W)
    kt = k[:, :, None] * (1.0 + (a - 1.0) * k_a)
    ab = kh[:, :, None] * a
    tm1 = lambda z: z.reshape(b, t, RW_HEADS, RW_DH).transpose(1, 0, 2, 3).astype(jnp.float32)
    tm2 = lambda z: z.reshape(b, t, 2, RW_HEADS, RW_DH).transpose(2, 1, 0, 3, 4)
    return tm1(r), tm1(kh), tm1(v), tm2(w), tm2(ab), tm2(kt), g


def _rwkv_scan(r, w, kh, ab, v, kt, s0):
    def step(S, inp):
        r_t, w_t, kh_t, ab_t, v_t, kt_t = inp
        sa = jnp.einsum('bhvk,bhk->bhv', S, kh_t)
        S = S * w_t[..., None, :] - sa[..., None] * ab_t[..., None, :] + v_t[..., None] * kt_t[..., None, :]
        return S, jnp.einsum('bhvk,bhk->bhv', S, r_t)
    return lax.scan(step, s0, (r, w, kh, ab, v, kt))


def _rwkv_out(y, bonus, g, gn_g, gn_b):
    t, b = y.shape[:2]
    merge = lambda z: z.transpose(1, 0, 2, 3).reshape(b, t, RW_W)
    z = merge(_ln(y, RW_GN_EPS)) * gn_g + gn_b + merge(bonus)
    return (z * g).astype(g.dtype)


def _rwkv_mixer(p_ctx, p_lat, rw, need_ctx):
    mu, w0, w_up, a0, a_up, g_up, k_k, k_a, r_k, gn_g, gn_b = rw
    sc = _rwkv_prep(p_ctx, mu, w0, w_up, a0, a_up, g_up, k_k, k_a)
    sl = _rwkv_prep(p_lat, mu, w0, w_up, a0, a_up, g_up, k_k, k_a)
    rho = r_k.reshape(RW_HEADS, RW_DH).astype(jnp.float32)
    s0 = jnp.zeros((p_lat.shape[0], RW_HEADS, RW_DH, RW_DH), jnp.float32)

    def run(s, dr, s_init):
        r, kh, v, w, ab, kt, _ = s
        rev = dr == 1
        f = lambda a: _flip_if(a, rev, 0)
        s_fin, y = _rwkv_scan(f(r), f(w[dr]), f(kh), f(ab[dr]), f(v), f(kt[dr]), s_init)
        bonus = jnp.sum(r * rho * kt[dr], axis=-1, keepdims=True) * v
        return s_fin, f(y), bonus

    y_c = b_c = y_l = b_l = 0.0
    for dr in range(2):
        s_c, yc, bc = run(sc, dr, s0)
        _, yl, bl = run(sl, dr, s_c)
        y_l = y_l + yl
        b_l = b_l + bl
        if need_ctx:
            y_c = y_c + yc
            b_c = b_c + bc
    y_ctx = _rwkv_out(y_c, b_c, sc[6], gn_g, gn_b) if need_ctx else None
    return y_ctx, _rwkv_out(y_l, b_l, sl[6], gn_g, gn_b)


def _rope2d(x, ang):
    xf = x.astype(jnp.float32).reshape(x.shape[:-1] + (2, MLA_ROPE // 2))
    x1, x2 = jnp.split(xf, 2, axis=-1)
    cos = jnp.cos(ang)[None, :, None]
    sin = jnp.sin(ang)[None, :, None]
    out = jnp.concatenate([x1 * cos - x2 * sin, x2 * cos + x1 * sin], axis=-1)
    return out.reshape(x.shape).astype(x.dtype)


def _mla_prep(p, q_norm_g, q_up, kv_norm_g, kv_up, ang):
    b, t, _ = p.shape
    cq, ckv, k_rope = _split(p, [MLA_Q_RANK, MLA_KV_RANK, MLA_ROPE])
    q = (_rms(cq, q_norm_g) @ q_up).reshape(b, t, MLA_HEADS, MLA_NOPE + MLA_ROPE)
    kv = (_rms(ckv, kv_norm_g) @ kv_up).reshape(b, t, MLA_HEADS, MLA_NOPE + MLA_V)
    q_nope, q_rope = jnp.split(q, [MLA_NOPE], axis=-1)
    k_nope, v = jnp.split(kv, [MLA_NOPE], axis=-1)
    k_rope = k_rope[:, :, None, :]
    if ang is not None:
        q_rope = _rope2d(q_rope, ang)
        k_rope = _rope2d(k_rope, ang)
    k_rope = jnp.broadcast_to(k_rope, (b, t, MLA_HEADS, MLA_ROPE))
    return (jnp.concatenate([q_nope, q_rope], axis=-1),
            jnp.concatenate([k_nope, k_rope], axis=-1), v)


def _attend(q, k, v):
    s = jnp.einsum('bqhd,bkhd->bhqk', q, k).astype(jnp.float32) * MLA_SCALE
    pr = jax.nn.softmax(s, axis=-1).astype(v.dtype)
    return jnp.einsum('bhqk,bkhd->bqhd', pr, v)


def _mla_mixer(p_ctx, p_lat, mla, ang, need_ctx):
    q_norm_g, q_up, kv_norm_g, kv_up = mla
    qc, kc, vc = _mla_prep(p_ctx, q_norm_g, q_up, kv_norm_g, kv_up, None)
    ql, kl, vl = _mla_prep(p_lat, q_norm_g, q_up, kv_norm_g, kv_up, ang)
    b, t = ql.shape[:2]
    k_all = jnp.concatenate([kl, kc], axis=1)
    v_all = jnp.concatenate([vl, vc], axis=1)
    nb = t // MLA_Q_BLOCK
    qb = jnp.moveaxis(ql.reshape(b, nb, MLA_Q_BLOCK, MLA_HEADS, -1), 1, 0)
    ol = lax.map(lambda qi: _attend(qi, k_all, v_all), qb)
    y_lat = jnp.moveaxis(ol, 0, 1).reshape(b, t, MLA_W)
    y_ctx = _attend(qc, kc, vc).reshape(b, p_ctx.shape[1], MLA_W) if need_ctx else None
    return y_ctx, y_lat


def _hybrid_mixer(h_ctx, h_lat, ang, need_ctx, w_in, w_out, ml, rw, mla):
    pc_ml, pc_rw, pc_at = _split(h_ctx @ w_in, [ML_COLS, RW_COLS, MLA_COLS])
    pl_ml, pl_rw, pl_at = _split(h_lat @ w_in, [ML_COLS, RW_COLS, MLA_COLS])
    ml_c, ml_l = _mlstm_mixer(pc_ml, pl_ml, ml, need_ctx)
    rw_c, rw_l = _rwkv_mixer(pc_rw, pl_rw, rw, need_ctx)
    at_c, at_l = _mla_mixer(pc_at, pl_at, mla, ang, need_ctx)
    y_lat = jnp.concatenate([ml_l, rw_l, at_l], axis=-1) @ w_out
    y_ctx = jnp.concatenate([ml_c, rw_c, at_c], axis=-1) @ w_out if need_ctx else None
    return y_ctx, y_lat


def _peer(h, w_q, keys, u_tab, v_tab):
    n, d = h.shape

    def chunk(hc):
        tc = hc.shape[0]
        q = (hc @ w_q).reshape(tc, PEER_HEADS, 2, PEER_DK)
        s = jnp.einsum('thpd,hpkd->thpk', q, keys).astype(jnp.float32)
        s1, i1 = lax.top_k(s[:, :, 0], PEER_TOPK)
        s2, i2 = lax.top_k(s[:, :, 1], PEER_TOPK)
        cand = (s1[..., :, None] + s2[..., None, :]).reshape(tc, PEER_HEADS, PEER_TOPK * PEER_TOPK)
        cid = (i1[..., :, None] * PEER_NKEYS + i2[..., None, :]).reshape(tc, PEER_HEADS, PEER_TOPK * PEER_TOPK)
        best, pos = lax.top_k(cand, PEER_TOPK)
        eid = jnp.take_along_axis(cid, pos, axis=-1).reshape(tc, PEER_HEADS * PEER_TOPK)
        gate = jax.nn.softmax(best, axis=-1).reshape(tc, PEER_HEADS * PEER_TOPK)
        act = jax.nn.gelu(jnp.einsum('ted,td->te', u_tab[eid], hc).astype(jnp.float32))
        return jnp.einsum('te,ted->td', (gate * act).astype(hc.dtype), v_tab[eid])

    out = lax.map(chunk, h.reshape(n // PEER_CHUNK, PEER_CHUNK, d))
    return out.reshape(n, d)


def setup_inputs(seed: int = 0) -> dict:
    key = jax.random.key(seed)
    ks = iter(jax.random.split(key, 40))
    nrm = lambda shape, s: jax.random.normal(next(ks), shape, jnp.float32) * s
    uni = lambda shape, lo, hi: jax.random.uniform(next(ks), shape, jnp.float32, lo, hi)
    L, D = DEPTH, D_MODEL
    return {
        'x': nrm((BATCH, SEQ, D), 1.0),
        'c': nrm((BATCH, D), 1.0),
        'ctx': nrm((BATCH, CTX_LEN, D), 1.0),
        'c_ctx': nrm((D,), 1.0),
        'w_mod': nrm((L, D, 6 * D), 0.5 * D ** -0.5),
        'b_mod': nrm((L, 6 * D), 0.02),
        'w_in': nrm((L, D, N_IN), D ** -0.5),
        'ml_conv_w': nrm((L, ML_CONV, 2 * ML_W), ML_CONV ** -0.5),
        'ml_conv_b': nrm((L, 2 * ML_W), 0.02),
        'ml_i_bias': nrm((L, 2, ML_HEADS), 0.1),
        'ml_f_bias': uni((L, 2, ML_HEADS), 3.0, 6.0),
        'ml_norm_g': 1.0 + nrm((L, ML_W), 0.02),
        'rw_mu': uni((L, 2, RW_COLS), 0.0, 0.5),
        'rw_w0': uni((L, 2, RW_W), -6.0, -1.0),
        'rw_w_up': nrm((L, 2, RW_W_LORA, RW_W), 0.1),
        'rw_a0': nrm((L, 2, RW_W), 0.1),
        'rw_a_up': nrm((L, 2, RW_A_LORA, RW_W), 0.1),
        'rw_g_up': nrm((L, RW_G_LORA, RW_W), RW_G_LORA ** -0.5),
        'rw_k_k': 0.85 + nrm((L, RW_W), 0.02),
        'rw_k_a': 1.0 + nrm((L, RW_W), 0.02),
        'rw_r_k': nrm((L, RW_W), 0.1),
        'rw_gn_g': 1.0 + nrm((L, RW_W), 0.02),
        'rw_gn_b': nrm((L, RW_W), 0.02),
        'mla_q_norm_g': 1.0 + nrm((L, MLA_Q_RANK), 0.02),
        'mla_q_up': nrm((L, MLA_Q_RANK, MLA_HEADS * (MLA_NOPE + MLA_ROPE)), MLA_Q_RANK ** -0.5),
        'mla_kv_norm_g': 1.0 + nrm((L, MLA_KV_RANK), 0.02),
        'mla_kv_up': nrm((L, MLA_KV_RANK, MLA_HEADS * (MLA_NOPE + MLA_V)), MLA_KV_RANK ** -0.5),
        'w_out': nrm((L, D_MIX, D), DEEPNORM_BETA * D_MIX ** -0.5),
        'ln_mix_g': 1.0 + nrm((L, D), 0.02),
        'ln_mix_b': nrm((L, D), 0.02),
        'peer_w_q': nrm((L, D, PEER_HEADS * 2 * PEER_DK), D ** -0.5),
        'peer_keys': nrm((L, PEER_HEADS, 2, PEER_NKEYS, PEER_DK), PEER_DK ** -0.5),
        'peer_u': nrm((L, PEER_EXPERTS, D), D ** -0.5),
        'peer_v': nrm((L, PEER_EXPERTS, D), DEEPNORM_BETA),
        'ln_ffn_g': 1.0 + nrm((L, D), 0.02),
        'ln_ffn_b': nrm((L, D), 0.02),
    }


def reference(x, c, ctx, c_ctx, w_mod, b_mod, w_in, ml_conv_w, ml_conv_b, ml_i_bias, ml_f_bias, ml_norm_g,
              rw_mu, rw_w0, rw_w_up, rw_a0, rw_a_up, rw_g_up, rw_k_k, rw_k_a, rw_r_k, rw_gn_g, rw_gn_b,
              mla_q_norm_g, mla_q_up, mla_kv_norm_g, mla_kv_up, w_out, ln_mix_g, ln_mix_b,
              peer_w_q, peer_keys, peer_u, peer_v, ln_ffn_g, ln_ffn_b):
    b, t, d = x.shape
    n_ctx = ctx.shape[1]
    ROWS = t // GRID_W
    row = jnp.repeat(jnp.arange(ROWS), GRID_W)
    col = jnp.tile(jnp.arange(GRID_W), ROWS)
    inv_freq = ROPE_THETA ** (-jnp.arange(ROPE_AXIS_FREQS, dtype=jnp.float32) / ROPE_AXIS_FREQS)
    ang = jnp.stack([row[:, None] * inv_freq, col[:, None] * inv_freq], axis=1)

    s_ctx = ctx
    for l in range(DEPTH):
        need_ctx = l < DEPTH - 1
        sh_a, sc_a, g_a, sh_f, sc_f, g_f = jnp.split((jax.nn.silu(c) @ w_mod[l] + b_mod[l])[:, None, :], 6, axis=-1)
        csh_a, csc_a, cg_a, csh_f, csc_f, cg_f = jnp.split(jax.nn.silu(c_ctx) @ w_mod[l] + b_mod[l], 6, axis=-1)
        ml = (ml_conv_w[l], ml_conv_b[l], ml_i_bias[l], ml_f_bias[l], ml_norm_g[l])
        rw = (rw_mu[l], rw_w0[l], rw_w_up[l], rw_a0[l], rw_a_up[l], rw_g_up[l], rw_k_k[l], rw_k_a[l],
              rw_r_k[l], rw_gn_g[l], rw_gn_b[l])
        mla = (mla_q_norm_g[l], mla_q_up[l], mla_kv_norm_g[l], mla_kv_up[l])

        y_c, y_l = _hybrid_mixer(_modulate(s_ctx, csh_a, csc_a), _modulate(x, sh_a, sc_a), ang, need_ctx,
                                 w_in[l], w_out[l], ml, rw, mla)
        x = _ln_affine(DEEPNORM_ALPHA * x + g_a * y_l, ln_mix_g[l], ln_mix_b[l])
        h_l = _modulate(x, sh_f, sc_f).reshape(b * t, d)

        if need_ctx:
            s_ctx = _ln_affine(DEEPNORM_ALPHA * s_ctx + cg_a * y_c, ln_mix_g[l], ln_mix_b[l])
            h_c = _modulate(s_ctx, csh_f, csc_f).reshape(b * n_ctx, d)
            y = _peer(jnp.concatenate([h_c, h_l], axis=0), peer_w_q[l], peer_keys[l], peer_u[l], peer_v[l])
            y_c = y[: b * n_ctx].reshape(b, n_ctx, d)
            y_l = y[b * n_ctx:].reshape(b, t, d)
            s_ctx = _ln_affine(DEEPNORM_ALPHA * s_ctx + cg_f * y_c, ln_ffn_g[l], ln_ffn_b[l])
        else:
            y_l = _peer(h_l, peer_w_q[l], peer_keys[l], peer_u[l], peer_v[l]).reshape(b, t, d)
        x = _ln_affine(DEEPNORM_ALPHA * x + g_f * y_l, ln_ffn_g[l], ln_ffn_b[l])
    return x
```

```python
import functools
import math

import jax
import jax.numpy as jnp
import numpy as np
from jax import lax
from jax.experimental import pallas as pl
from jax.experimental.pallas import tpu as pltpu

F32 = jnp.float32
BF16 = jnp.bfloat16
HIGHEST = lax.Precision.HIGHEST

LANES = 128
SUBLANES = 8

D_MODEL = 1024
DEPTH = 2
GRID_W = 64

ML_HEADS = 4
ML_DH = 64
ML_W = ML_HEADS * ML_DH
ML_CHUNK = 64
ML_M_INIT = -1e30
ML_GATE_COLS = 4 * ML_HEADS
ML_PAD = 4 * ML_W + LANES

RW_HEADS = 6
RW_DH = 64
RW_W = RW_HEADS * RW_DH
RW_PAIRS = RW_HEADS // 2
RW_LORA = 64
RW_G_LORA = 128
RW_DECAY_SCALE = math.exp(-0.5)
RW_GN_EPS = 64e-5
RW_COLS = 3 * RW_W + 4 * RW_LORA + RW_G_LORA
RW_CHUNK = 64

MLA_HEADS = 6
MLA_NOPE = 64
MLA_ROPE = 32
MLA_V = 64
MLA_Q_RANK = 384
MLA_KV_RANK = 256
MLA_W = MLA_HEADS * MLA_V
MLA_SCALE = (MLA_NOPE + MLA_ROPE) ** -0.5
MLA_PAD = 768
ROPE_AXIS_FREQS = MLA_ROPE // 4
ROPE_THETA = 10000.0

PEER_HEADS = 8
PEER_NKEYS = 128
PEER_EXPERTS = PEER_NKEYS * PEER_NKEYS
PEER_DK = 128
PEER_TOPK = 16

DEEPNORM_ALPHA = (2 * DEPTH) ** 0.25
LN_EPS = 1e-6

ROW_TILE = 256
VMEM_LIMIT = 56 * 1024 * 1024


def _cparams(n_axes):
    return pltpu.CompilerParams(dimension_semantics=("arbitrary",) * n_axes,
                                vmem_limit_bytes=VMEM_LIMIT)


def _ln_rows(x, eps=LN_EPS):
    mu = jnp.mean(x, axis=-1, keepdims=True)
    xc = x - mu
    var = jnp.mean(xc * xc, axis=-1, keepdims=True)
    return xc * lax.rsqrt(var + eps)


def _sigmoid(x):
    return 1.0 / (1.0 + jnp.exp(-x))


def _group_ln(x, gmat, eps):
    mu = jnp.dot(x, gmat, precision=HIGHEST, preferred_element_type=F32)
    xc = x - mu
    var = jnp.dot(xc * xc, gmat, precision=HIGHEST, preferred_element_type=F32)
    return xc * lax.rsqrt(var + eps)


def _group_matrix(width, group, value):
    idx = np.arange(width) // group
    return jnp.asarray((idx[:, None] == idx[None, :]).astype(np.float32) * value)


def _mod_kernel(c_ref, w_ref, b_ref, o_ref):
    c = c_ref[...]
    s = c * _sigmoid(c)
    o_ref[...] = jnp.dot(s, w_ref[...], precision=HIGHEST, preferred_element_type=F32) + b_ref[...]


def _modulation(cvec, w_mod, b_mod):
    rows, d = cvec.shape
    n = w_mod.shape[1]
    tn = 1536
    return pl.pallas_call(
        _mod_kernel,
        grid=(n // tn,),
        in_specs=[pl.BlockSpec((rows, d), lambda j: (0, 0)),
                  pl.BlockSpec((d, tn), lambda j: (0, j)),
                  pl.BlockSpec((1, tn), lambda j: (0, j))],
        out_specs=pl.BlockSpec((rows, tn), lambda j: (0, j)),
        out_shape=jax.ShapeDtypeStruct((rows, n), F32),
        compiler_params=_cparams(1),
        name="modulation",
    )(cvec, w_mod, b_mod.reshape(1, n))


def _in_proj_kernel(x_ref, mod_ref, wml_ref, wrw_ref, wat_ref, pml_ref, prw_ref, pat_ref):
    mod = mod_ref[0, 0]
    h = _ln_rows(x_ref[0]) * (1.0 + mod[1:2]) + mod[0:1]
    hb = h.astype(BF16)
    pml_ref[0] = jnp.dot(hb, wml_ref[...], preferred_element_type=F32)
    prw_ref[0] = jnp.dot(hb, wrw_ref[...], preferred_element_type=F32)
    pat_ref[0] = jnp.dot(hb, wat_ref[...], preferred_element_type=F32)


def _in_proj(x_all, mods, wml, wrw, wat, n_ctx_tiles):
    b, s, d = x_all.shape
    tm = ROW_TILE
    seg = lambda i: (i >= n_ctx_tiles).astype(jnp.int32)
    full = lambda w: pl.BlockSpec(w.shape, lambda bi, i: (0, 0))
    out = lambda w: pl.BlockSpec((1, tm, w.shape[1]), lambda bi, i: (bi, i, 0))
    return pl.pallas_call(
        _in_proj_kernel,
        grid=(b, s // tm),
        in_specs=[pl.BlockSpec((1, tm, d), lambda bi, i: (bi, i, 0)),
                  pl.BlockSpec((1, 1, 6, d), lambda bi, i: (bi, seg(i), 0, 0)),
                  full(wml), full(wrw), full(wat)],
        out_specs=[out(wml), out(wrw), out(wat)],
        out_shape=[jax.ShapeDtypeStruct((b, s, w.shape[1]), F32) for w in (wml, wrw, wat)],
        compiler_params=_cparams(2),
        name="in_proj",
    )(x_all, mods, wml, wrw, wat)


def _halo_specs(width, col_block, tm, s):
    per = tm // SUBLANES
    last = s // SUBLANES - 1
    prev = pl.BlockSpec((1, SUBLANES, width),
                        lambda bi, i: (bi, jnp.maximum(i * per - 1, 0), col_block))
    nxt = pl.BlockSpec((1, SUBLANES, width),
                       lambda bi, i: (bi, jnp.minimum((i + 1) * per, last), col_block))
    return prev, nxt


def _neighbours(x, prev_blk, next_blk, tile, n_ctx_tiles, n_tiles):
    tm = x.shape[0]
    row = lax.broadcasted_iota(jnp.int32, x.shape, 0)
    starts = jnp.logical_or(tile == 0, tile == n_ctx_tiles)
    ends = jnp.logical_or(tile == n_ctx_tiles - 1, tile == n_tiles - 1)
    prev_row = jnp.where(starts, 0.0, prev_blk[SUBLANES - 1:SUBLANES, :])
    next_row = jnp.where(ends, 0.0, next_blk[0:1, :])
    prev = jnp.where(row == 0, prev_row, pltpu.roll(x, 1, 0))
    nxt = jnp.where(row == tm - 1, next_row, pltpu.roll(x, tm - 1, 0))
    return prev, nxt


def _ml_prep_kernel(n_ctx_tiles, n_tiles, qk_ref, prev_ref, next_ref, gate_ref, cw_ref, cb_ref,
                    gb_ref, fmask_ref, q_ref, k_ref, lg_ref):
    tile = pl.program_id(1)
    x = qk_ref[0]
    prev, nxt = _neighbours(x, prev_ref[0], next_ref[0], tile, n_ctx_tiles, n_tiles)
    cw = cw_ref[...]
    z = prev * cw[0:1] + x * cw[1:2] + nxt * cw[2:3] + cb_ref[...]
    z = z * _sigmoid(z)
    q_ref[0] = z[:, :ML_W] * (ML_DH ** -0.5)
    k_ref[0] = z[:, ML_W:]
    g = gate_ref[0] + gb_ref[...]
    log_sig = jnp.minimum(g, 0.0) - jnp.log1p(jnp.exp(-jnp.abs(g)))
    lg_ref[0] = jnp.where(fmask_ref[...] > 0.5, log_sig, g)


def _ml_prep(pml, conv_w, conv_b, i_bias, f_bias, n_ctx_tiles):
    b, s, _ = pml.shape
    tm = ROW_TILE
    n_tiles = s // tm
    gate_bias = jnp.stack([i_bias, f_bias], axis=1).reshape(1, ML_GATE_COLS)
    gate_bias = jnp.pad(gate_bias, ((0, 0), (0, LANES - ML_GATE_COLS)))
    fmask = np.zeros((2, 2, ML_HEADS), np.float32)
    fmask[:, 1] = 1.0
    fmask = jnp.asarray(np.pad(fmask.reshape(1, -1), ((0, 0), (0, LANES - ML_GATE_COLS))))
    prev, nxt = _halo_specs(2 * ML_W, 0, tm, s)
    small = lambda a: pl.BlockSpec(a.shape, lambda bi, i: (0, 0))
    cb = conv_b.reshape(1, -1)
    return pl.pallas_call(
        functools.partial(_ml_prep_kernel, n_ctx_tiles, n_tiles),
        grid=(b, n_tiles),
        in_specs=[pl.BlockSpec((1, tm, 2 * ML_W), lambda bi, i: (bi, i, 0)), prev, nxt,
                  pl.BlockSpec((1, tm, LANES), lambda bi, i: (bi, i, 4 * ML_W // LANES)),
                  small(conv_w), small(cb), small(gate_bias), small(fmask)],
        out_specs=[pl.BlockSpec((1, tm, ML_W), lambda bi, i: (bi, i, 0)),
                   pl.BlockSpec((1, tm, ML_W), lambda bi, i: (bi, i, 0)),
                   pl.BlockSpec((1, tm, LANES), lambda bi, i: (bi, i, 0))],
        out_shape=[jax.ShapeDtypeStruct((b, s, ML_W), F32),
                   jax.ShapeDtypeStruct((b, s, ML_W), F32),
                   jax.ShapeDtypeStruct((b, s, LANES), F32)],
        compiler_params=_cparams(2),
        name="mlstm_prep",
    )(pml, pml, pml, pml, conv_w, cb, gate_bias, fmask)


def _ml_scan_kernel(n_chunks, q_ref, k_ref, kt_ref, v_ref, ic_ref, ir_ref, fc_ref, fr_ref, h_ref):
    L = ML_CHUNK
    row = lax.broadcasted_iota(jnp.int32, (L, L), 0)
    col = lax.broadcasted_iota(jnp.int32, (L, L), 1)
    causal = col <= row
    dot = functools.partial(jnp.dot, precision=HIGHEST, preferred_element_type=F32)

    def chunk(c, carry):
        ct, n, m = carry
        q = q_ref[0, 0, 0, c]
        k = k_ref[0, 0, 0, c]
        kt = kt_ref[0, 0, 0, c]
        v = v_ref[0, 0, 0, c]
        i_col = ic_ref[0, 0, 0, c]
        i_row = ir_ref[0, 0, 0, c]
        f_col = fc_ref[0, 0, 0, c]
        f_row = fr_ref[0, 0, 0, c]
        bcum_col = jnp.sum(jnp.where(causal, f_row, 0.0), axis=1, keepdims=True)
        bcum_row = jnp.sum(jnp.where(row <= col, f_col, 0.0), axis=0, keepdims=True)
        dmat = jnp.where(causal, bcum_col - bcum_row + i_row, -jnp.inf)
        a_inter = bcum_col + m
        m_t = jnp.maximum(a_inter, jnp.max(dmat, axis=1, keepdims=True))
        w_inter = jnp.exp(a_inter - m_t)
        s = dot(q, kt) * jnp.exp(dmat - m_t)
        num = w_inter * dot(q, ct) + dot(s, v)
        den = w_inter * jnp.sum(q * n, axis=1, keepdims=True) + jnp.sum(s, axis=1, keepdims=True)
        h_ref[0, 0, 0, c] = num / jnp.maximum(jnp.abs(den), jnp.exp(-m_t))
        b_end = bcum_col[L - 1:L, :]
        g = b_end - bcum_col + i_col
        m_new = jnp.maximum(b_end + m, jnp.max(g, axis=0, keepdims=True))
        decay = jnp.exp(b_end + m - m_new)
        wk = jnp.exp(g - m_new)
        ct = decay * ct + dot(kt, wk * v)
        n = decay * n + jnp.sum(wk * k, axis=0, keepdims=True)
        return ct, n, m_new

    init = (jnp.zeros((ML_DH, ML_DH), F32), jnp.zeros((1, ML_DH), F32),
            jnp.full((1, 1), ML_M_INIT, F32))
    lax.fori_loop(0, n_chunks, chunk, init)


def _flip_segments(a, n_ctx, axis):
    ctx, lat = jnp.split(a, [n_ctx], axis=axis)
    return jnp.concatenate([jnp.flip(ctx, axis), jnp.flip(lat, axis)], axis=axis)


def _ml_scan(q, k, v, lg, n_ctx):
    b, s, _ = q.shape
    L = ML_CHUNK
    nc = s // L
    heads = lambda a: a.reshape(b, s, ML_HEADS, ML_DH).transpose(0, 2, 1, 3)
    both = lambda a: jnp.stack([a, _flip_segments(a, n_ctx, 2)], axis=1)
    chunks = lambda a: a.reshape(b, 2, ML_HEADS, nc, L, a.shape[-1])
    qd, kd, vd = (chunks(both(heads(a))) for a in (q, k, v))
    ktd = jnp.swapaxes(kd, -1, -2)
    gates = lg[:, :, :ML_GATE_COLS].reshape(b, s, 2, 2, ML_HEADS).transpose(2, 3, 0, 4, 1)
    gates = jnp.stack([gates[0], _flip_segments(gates[1], n_ctx, 3)], axis=0)
    gates = gates.transpose(1, 2, 0, 3, 4).reshape(2, b, 2, ML_HEADS, nc, L)
    col = lambda a: a[..., None]
    row = lambda a: a[..., None, :]
    mat = pl.BlockSpec((1, 1, 1, nc, L, ML_DH), lambda bi, di, hi: (bi, di, hi, 0, 0, 0))
    cspec = pl.BlockSpec((1, 1, 1, nc, L, 1), lambda bi, di, hi: (bi, di, hi, 0, 0, 0))
    rspec = pl.BlockSpec((1, 1, 1, nc, 1, L), lambda bi, di, hi: (bi, di, hi, 0, 0, 0))
    h = pl.pallas_call(
        functools.partial(_ml_scan_kernel, nc),
        grid=(b, 2, ML_HEADS),
        in_specs=[mat, mat, mat, mat, cspec, rspec, cspec, rspec],
        out_specs=mat,
        out_shape=jax.ShapeDtypeStruct((b, 2, ML_HEADS, nc, L, ML_DH), F32),
        compiler_params=_cparams(3),
        name="mlstm_scan",
    )(qd, kd, ktd, vd, col(gates[0]), row(gates[0]), col(gates[1]), row(gates[1]))
    h = h.reshape(b, 2, ML_HEADS, s, ML_DH)
    merge = lambda a: a.transpose(0, 2, 1, 3).reshape(b, s, ML_W)
    return merge(h[:, 0]), merge(_flip_segments(h[:, 1], n_ctx, 2))


def _rw_prep_kernel(n_ctx_tiles, n_tiles, p_ref, prev_ref, next_ref, mu_ref, wup_ref, aup_ref,
                    gup_ref, w0_ref, a0_ref, kk_ref, ka_ref, rho_ref, gsum_ref,
                    r_ref, kh_ref, v_ref, g_ref, bonus_ref, w_ref, ab_ref, kt_ref,
                    vhi_ref, vmid_ref, vlo_ref):
    tile = pl.program_id(1)
    p = p_ref[0]
    prev, nxt = _neighbours(p, prev_ref[0], next_ref[0], tile, n_ctx_tiles, n_tiles)
    mu = mu_ref[...]
    p = p + mu[0:1] * (prev - p) + mu[1:2] * (nxt - p)
    r = p[:, 0:RW_W]
    k = p[:, RW_W:2 * RW_W]
    v = p[:, 2 * RW_W:3 * RW_W]
    base = 3 * RW_W
    wd = p[:, base:base + 2 * RW_LORA]
    ad = p[:, base + 2 * RW_LORA:base + 4 * RW_LORA]
    gd = p[:, base + 4 * RW_LORA:base + 4 * RW_LORA + RW_G_LORA]
    dot = functools.partial(jnp.dot, preferred_element_type=F32)
    w = jnp.exp(-RW_DECAY_SCALE * _sigmoid(w0_ref[...] + dot(jnp.tanh(wd).astype(BF16), wup_ref[...])))
    a = _sigmoid(a0_ref[...] + dot(ad.astype(BF16), aup_ref[...]))
    g = dot(_sigmoid(gd).astype(BF16), gup_ref[...])
    kk = k * kk_ref[...]
    ss = jnp.dot(kk * kk, gsum_ref[...], precision=HIGHEST, preferred_element_type=F32)
    kh = kk * lax.rsqrt(ss + 1e-12)
    ka = ka_ref[...]
    rk = r * rho_ref[...]
    kt_sum = jnp.zeros_like(k)
    for dr in range(2):
        a_d = a[:, dr * RW_W:(dr + 1) * RW_W]
        kt_d = k * (1.0 + (a_d - 1.0) * ka)
        kt_ref[0, :, dr * RW_W:(dr + 1) * RW_W] = kt_d
        ab_ref[0, :, dr * RW_W:(dr + 1) * RW_W] = kh * a_d
        kt_sum = kt_sum + kt_d
    bonus_ref[0] = jnp.dot(rk * kt_sum, gsum_ref[...], precision=HIGHEST,
                           preferred_element_type=F32) * v
    r_ref[0] = r
    kh_ref[0] = kh
    v_ref[0] = v
    g_ref[0] = g
    w_ref[0] = w
    hi = v.astype(BF16)
    rem = v - hi.astype(F32)
    mid = rem.astype(BF16)
    vhi_ref[0] = hi
    vmid_ref[0] = mid
    vlo_ref[0] = (rem - mid.astype(F32)).astype(BF16)


def _block_diag2(up):
    z = jnp.zeros_like(up[0])
    return jnp.concatenate([jnp.concatenate([up[0], z], axis=1),
                            jnp.concatenate([z, up[1]], axis=1)], axis=0)


def _rw_prep(prw, mu, w0, w_up, a0, a_up, g_up, k_k, k_a, r_k, n_ctx_tiles):
    b, s, _ = prw.shape
    tm = ROW_TILE
    n_tiles = s // tm
    prev, nxt = _halo_specs(RW_COLS, 0, tm, s)
    small = lambda a: pl.BlockSpec(a.shape, lambda bi, i: (0, 0))
    consts = [mu, _block_diag2(w_up).astype(BF16), _block_diag2(a_up).astype(BF16),
              g_up.astype(BF16), w0.reshape(1, 2 * RW_W), a0.reshape(1, 2 * RW_W),
              k_k.reshape(1, RW_W), k_a.reshape(1, RW_W), r_k.reshape(1, RW_W),
              _group_matrix(RW_W, RW_DH, 1.0)]
    one = lambda width, dt=F32: (pl.BlockSpec((1, tm, width), lambda bi, i: (bi, i, 0)),
                                 jax.ShapeDtypeStruct((b, s, width), dt))
    outs = [one(RW_W)] * 5 + [one(2 * RW_W)] * 3 + [one(RW_W, BF16)] * 3
    return pl.pallas_call(
        functools.partial(_rw_prep_kernel, n_ctx_tiles, n_tiles),
        grid=(b, n_tiles),
        in_specs=[pl.BlockSpec((1, tm, RW_COLS), lambda bi, i: (bi, i, 0)), prev, nxt]
                 + [small(a) for a in consts],
        out_specs=[o[0] for o in outs],
        out_shape=[o[1] for o in outs],
        compiler_params=_cparams(2),
        name="rwkv_prep",
    )(prw, prw, prw, *consts)


def _rw_scan_kernel(batch, *refs):
    ins = (refs[0:8], refs[8:16])
    y_refs = refs[16:18]
    state_ref, yacc_ref = refs[18:20]
    L = RW_CHUNK

    @pl.when(pl.program_id(0) == 0)
    def _():
        state_ref[...] = jnp.zeros_like(state_ref)

    yacc_ref[...] = jnp.zeros_like(yacc_ref)
    lane = lax.broadcasted_iota(jnp.int32, (RW_DH, LANES), 1)
    first_head = lane < RW_DH
    step_of_lane = jnp.where(first_head, lane, lane - RW_DH)
    sel_row = lax.broadcasted_iota(jnp.int32, (LANES, LANES), 0)
    sel_col = lax.broadcasted_iota(jnp.int32, (LANES, LANES), 1)
    same_head = (sel_row < RW_DH) == (sel_col < RW_DH)
    sel_step = jnp.where(sel_row < RW_DH, sel_row, sel_row - RW_DH)

    def halves(x):
        first = jnp.sum(jnp.where(first_head, x, 0.0), axis=1, keepdims=True)
        total = jnp.sum(x, axis=1, keepdims=True)
        return jnp.where(first_head, first, total - first)

    def step(j, _):
        for d in range(2):
            w_ref, ab_ref, kt_ref, kh_ref, r_ref, vhi_ref, vmid_ref, vlo_ref = ins[d]
            jj = j if d == 0 else L - 1 - j
            select = jnp.where(jnp.logical_and(same_head, sel_step == jj), 1.0, 0.0).astype(BF16)
            here = step_of_lane == jj
            w_t, ab_t, kt_t, kh_t, r_t = (ref[jj] for ref in (w_ref, ab_ref, kt_ref, kh_ref, r_ref))
            for bi in range(batch):
                for p in range(RW_PAIRS):
                    idx = (bi * 2 + d) * RW_PAIRS + p
                    cols = slice(p * LANES, (p + 1) * LANES)
                    vec = lambda rows: rows[bi:bi + 1, cols]
                    vb = (jnp.dot(vhi_ref[bi, 0, p], select, preferred_element_type=F32)
                          + jnp.dot(vmid_ref[bi, 0, p], select, preferred_element_type=F32)
                          + jnp.dot(vlo_ref[bi, 0, p], select, preferred_element_type=F32))
                    st = state_ref[idx]
                    sa = halves(st * vec(kh_t))
                    st = st * vec(w_t) - sa * vec(ab_t) + vb * vec(kt_t)
                    state_ref[idx] = st
                    y = halves(st * vec(r_t))
                    yacc_ref[idx] = jnp.where(here, y, yacc_ref[idx])
        return 0

    lax.fori_loop(0, L, step, 0)
    for d in range(2):
        for bi in range(batch):
            for p in range(RW_PAIRS):
                y_refs[d][bi, 0, p] = yacc_ref[(bi * 2 + d) * RW_PAIRS + p]


def _pair_layout(a, nc):
    b = a.shape[0]
    a = a.reshape(b, nc, RW_CHUNK, RW_PAIRS, 2, RW_DH)
    return a.transpose(0, 1, 3, 5, 4, 2).reshape(b, nc, RW_PAIRS, RW_DH, LANES)


def _pair_unlayout(a):
    b, nc = a.shape[:2]
    a = a.reshape(b, nc, RW_PAIRS, RW_DH, 2, RW_CHUNK)
    return a.transpose(0, 1, 5, 2, 4, 3).reshape(b, nc * RW_CHUNK, RW_W)


def _rw_scan(r, kh, w, ab, kt, vhi, vmid, vlo, n_ctx):
    b, s, _ = r.shape
    L = RW_CHUNK
    nc = s // L
    ncc = n_ctx // L
    cmap = (lambda c: c,
            lambda c: jnp.where(c < ncc, ncc - 1 - c, ncc + nc - 1 - c))
    vt = [_pair_layout(a, nc) for a in (vhi, vmid, vlo)]
    time_major = lambda a: a.transpose(1, 0, 2)
    pair_block = (b, 1, RW_PAIRS, RW_DH, LANES)
    in_specs, args = [], []
    for d in range(2):
        cm = cmap[d]
        per_dir = lambda c, cm=cm, d=d: (cm(c), 0, d)
        shared = lambda c, cm=cm: (cm(c), 0, 0)
        pairs = lambda c, cm=cm: (0, cm(c), 0, 0, 0)
        in_specs += [pl.BlockSpec((L, b, RW_W), per_dir)] * 3
        in_specs += [pl.BlockSpec((L, b, RW_W), shared)] * 2
        in_specs += [pl.BlockSpec(pair_block, pairs)] * 3
        args += [time_major(a) for a in (w, ab, kt, kh, r)] + vt
    out_specs = [pl.BlockSpec(pair_block, lambda c, cm=cm: (0, cm(c), 0, 0, 0)) for cm in cmap]
    ys = pl.pallas_call(
        functools.partial(_rw_scan_kernel, b),
        grid=(nc,),
        in_specs=in_specs,
        out_specs=out_specs,
        out_shape=[jax.ShapeDtypeStruct((b, nc, RW_PAIRS, RW_DH, LANES), F32)] * 2,
        scratch_shapes=[pltpu.VMEM((b * 2 * RW_PAIRS, RW_DH, LANES), F32),
                        pltpu.VMEM((b * 2 * RW_PAIRS, RW_DH, LANES), F32)],
        compiler_params=_cparams(1),
        name="rwkv_scan",
    )(*args)
    return _pair_unlayout(ys[0]), _pair_unlayout(ys[1])


def _mla_prep_kernel(p_ref, cosq_ref, sinq_ref, cosk_ref, sink_ref, qg_ref, kvg_ref, wq_ref, wkv_ref,
                     qn_ref, qr_ref, kn_ref, v_ref, kr_ref):
    p = p_ref[0]
    cq = p[:, :MLA_Q_RANK]
    ckv = p[:, MLA_Q_RANK:MLA_Q_RANK + MLA_KV_RANK]
    base = MLA_Q_RANK + MLA_KV_RANK
    k_rope = p[:, base:base + MLA_ROPE]
    k_rot = p[:, base + MLA_ROPE:base + 2 * MLA_ROPE]
    rms = lambda x, g: x * lax.rsqrt(jnp.mean(x * x, axis=-1, keepdims=True) + LN_EPS) * g
    q = jnp.dot(rms(cq, qg_ref[...]).astype(BF16), wq_ref[...], preferred_element_type=F32)
    kv = jnp.dot(rms(ckv, kvg_ref[...]).astype(BF16), wkv_ref[...], preferred_element_type=F32)
    nope = MLA_HEADS * MLA_NOPE
    rope = MLA_HEADS * MLA_ROPE
    qn_ref[0] = q[:, :nope]
    qr_ref[0] = q[:, nope:nope + rope] * cosq_ref[...] + q[:, nope + rope:nope + 2 * rope] * sinq_ref[...]
    kn_ref[0] = kv[:, :nope]
    v_ref[0] = kv[:, nope:]
    kr_ref[0] = k_rope * cosk_ref[...] + k_rot * sink_ref[...]


def _rot_cols(w):
    shape = w.shape
    w = w.reshape(shape[:-1] + (shape[-1] // 16, 2, 8))
    return jnp.concatenate([-w[..., 1:2, :], w[..., 0:1, :]], axis=-2).reshape(shape)


def _mla_prep(pat, q_norm_g, q_up, kv_norm_g, kv_up, cos, sin):
    b, s, _ = pat.shape
    tm = ROW_TILE
    qw = q_up.reshape(MLA_Q_RANK, MLA_HEADS, MLA_NOPE + MLA_ROPE)
    q_nope = qw[:, :, :MLA_NOPE].reshape(MLA_Q_RANK, -1)
    q_rope = qw[:, :, MLA_NOPE:]
    wq = jnp.concatenate([q_nope, q_rope.reshape(MLA_Q_RANK, -1),
                          _rot_cols(q_rope).reshape(MLA_Q_RANK, -1)], axis=1).astype(BF16)
    kvw = kv_up.reshape(MLA_KV_RANK, MLA_HEADS, MLA_NOPE + MLA_V)
    wkv = jnp.concatenate([kvw[:, :, :MLA_NOPE].reshape(MLA_KV_RANK, -1),
                           kvw[:, :, MLA_NOPE:].reshape(MLA_KV_RANK, -1)], axis=1).astype(BF16)
    cosq = jnp.tile(cos, (1, MLA_HEADS))
    sinq = jnp.tile(sin, (1, MLA_HEADS))
    rows = lambda width: pl.BlockSpec((tm, width), lambda bi, i: (i, 0))
    small = lambda a: pl.BlockSpec(a.shape, lambda bi, i: (0, 0))
    qg = q_norm_g.reshape(1, -1)
    kvg = kv_norm_g.reshape(1, -1)
    out = lambda width: (pl.BlockSpec((1, tm, width), lambda bi, i: (bi, i, 0)),
                         jax.ShapeDtypeStruct((b, s, width), F32))
    outs = [out(MLA_HEADS * MLA_NOPE), out(MLA_HEADS * MLA_ROPE), out(MLA_HEADS * MLA_NOPE),
            out(MLA_W), out(MLA_ROPE)]
    return pl.pallas_call(
        _mla_prep_kernel,
        grid=(b, s // tm),
        in_specs=[pl.BlockSpec((1, tm, MLA_PAD), lambda bi, i: (bi, i, 0)),
                  rows(MLA_HEADS * MLA_ROPE), rows(MLA_HEADS * MLA_ROPE), rows(MLA_ROPE), rows(MLA_ROPE),
                  small(qg), small(kvg), small(wq), small(wkv)],
        out_specs=[o[0] for o in outs],
        out_shape=[o[1] for o in outs],
        compiler_params=_cparams(2),
        name="mla_prep",
    )(pat, cosq, sinq, cos, sin, qg, kvg, wq, wkv)


def _attn_kernel(q_ref, k_ref, v_ref, o_ref):
    s = lax.dot_general(q_ref[0, 0], k_ref[0, 0], (((1,), (1,)), ((), ())),
                        preferred_element_type=F32) * MLA_SCALE
    p = jnp.exp(s - jnp.max(s, axis=-1, keepdims=True))
    denom = jnp.sum(p, axis=-1, keepdims=True)
    o_ref[0, 0] = jnp.dot(p.astype(BF16), v_ref[0, 0], preferred_element_type=F32) / denom


def _attention(q, k, v, tq):
    b, h, t, dq = q.shape
    sk = k.shape[2]
    dv = v.shape[3]
    return pl.pallas_call(
        _attn_kernel,
        grid=(b, h, t // tq),
        in_specs=[pl.BlockSpec((1, 1, tq, dq), lambda bi, hi, i: (bi, hi, i, 0)),
                  pl.BlockSpec((1, 1, sk, dq), lambda bi, hi, i: (bi, hi, 0, 0)),
                  pl.BlockSpec((1, 1, sk, dv), lambda bi, hi, i: (bi, hi, 0, 0))],
        out_specs=pl.BlockSpec((1, 1, tq, dv), lambda bi, hi, i: (bi, hi, i, 0)),
        out_shape=jax.ShapeDtypeStruct((b, h, t, dv), F32),
        compiler_params=_cparams(3),
        name="mla_attention",
    )(q, k, v)


def _mla(qn, qr, kn, v, kr, n_ctx, need_ctx):
    b, s, _ = qn.shape
    heads = lambda a, w: a.reshape(b, s, MLA_HEADS, w).transpose(0, 2, 1, 3)
    q = jnp.concatenate([heads(qn, MLA_NOPE), heads(qr, MLA_ROPE)], axis=-1).astype(BF16)
    k_rope = jnp.broadcast_to(kr[:, None], (b, MLA_HEADS, s, MLA_ROPE))
    k = jnp.concatenate([heads(kn, MLA_NOPE), k_rope], axis=-1).astype(BF16)
    vh = heads(v, MLA_V).astype(BF16)
    pad = ((0, 0), (0, 0), (0, 0), (0, LANES - MLA_NOPE - MLA_ROPE))
    q, k = jnp.pad(q, pad), jnp.pad(k, pad)
    merge = lambda a: a.transpose(0, 2, 1, 3).reshape(b, -1, MLA_W)
    t = s - n_ctx
    y_lat = merge(_attention(q[:, :, n_ctx:], k, vh, min(t, 512)))
    if not need_ctx:
        return None, y_lat
    y_ctx = merge(_attention(q[:, :, :n_ctx], k[:, :, :n_ctx], vh[:, :, :n_ctx], n_ctx))
    return y_ctx, y_lat


def _out_proj_kernel(x_ref, mod_ref, hf_ref, hb_ref, o_ref, yf_ref, yb_ref, bonus_ref, g_ref, at_ref,
                     g4_ref, g6_ref, mlg_ref, gng_ref, gnb_ref, wml_ref, wrw_ref, wat_ref,
                     lng_ref, lnb_ref, xo_ref, ho_ref):
    mod = mod_ref[0, 0]
    ml = _sigmoid(o_ref[0]) * _group_ln(hf_ref[0] + hb_ref[0], g4_ref[...], LN_EPS) * mlg_ref[...]
    z = _group_ln(yf_ref[0] + yb_ref[0], g6_ref[...], RW_GN_EPS) * gng_ref[...] + gnb_ref[...]
    rw = (z + bonus_ref[0]) * g_ref[0]
    dot = functools.partial(jnp.dot, preferred_element_type=F32)
    y = (dot(ml.astype(BF16), wml_ref[...]) + dot(rw.astype(BF16), wrw_ref[...])
         + dot(at_ref[0].astype(BF16), wat_ref[...]))
    x = _ln_rows(DEEPNORM_ALPHA * x_ref[0] + mod[2:3] * y) * lng_ref[...] + lnb_ref[...]
    xo_ref[0] = x
    ho_ref[0] = (_ln_rows(x) * (1.0 + mod[4:5]) + mod[3:4]).astype(BF16)


def _out_proj(x_all, mods, pml, hf, hb, yf, yb, bonus, g, y_at, norm_g, gn_g, gn_b, w_out,
              ln_g, ln_b, n_ctx_tiles, first_tile):
    b, s, d = x_all.shape
    tm = ROW_TILE
    n_tiles = s // tm - first_tile
    seg = lambda i: (i + first_tile >= n_ctx_tiles).astype(jnp.int32)
    rows = lambda width, blk=0: pl.BlockSpec((1, tm, width), lambda bi, i: (bi, i + first_tile, blk))
    small = lambda a: pl.BlockSpec(a.shape, lambda bi, i: (0, 0))
    wb = w_out.astype(BF16)
    consts = [_group_matrix(ML_W, ML_DH, 1.0 / ML_DH), _group_matrix(RW_W, RW_DH, 1.0 / RW_DH),
              norm_g.reshape(1, -1), gn_g.reshape(1, -1), gn_b.reshape(1, -1),
              wb[:ML_W], wb[ML_W:ML_W + RW_W], wb[ML_W + RW_W:],
              ln_g.reshape(1, -1), ln_b.reshape(1, -1)]
    out_rows = lambda: pl.BlockSpec((1, tm, d), lambda bi, i: (bi, i, 0))
    return pl.pallas_call(
        _out_proj_kernel,
        grid=(b, n_tiles),
        in_specs=[rows(d), pl.BlockSpec((1, 1, 6, d), lambda bi, i: (bi, seg(i), 0, 0)),
                  rows(ML_W), rows(ML_W), rows(ML_W, 3 * ML_W // ML_W),
                  rows(RW_W), rows(RW_W), rows(RW_W), rows(RW_W), rows(MLA_W)]
                 + [small(a) for a in consts],
        out_specs=[out_rows(), out_rows()],
        out_shape=[jax.ShapeDtypeStruct((b, n_tiles * tm, d), F32),
                   jax.ShapeDtypeStruct((b, n_tiles * tm, d), BF16)],
        compiler_params=_cparams(2),
        name="out_proj",
    )(x_all, mods, hf, hb, pml, yf, yb, bonus, g, y_at, *consts)


def _final_norm_kernel(x_ref, y_ref, mod_ref, g_ref, b_ref, o_ref):
    mod = mod_ref[0, 0]
    o_ref[0] = _ln_rows(DEEPNORM_ALPHA * x_ref[0] + mod[5:6] * y_ref[0]) * g_ref[...] + b_ref[...]


def _final_norm(x, y, mods, ln_g, ln_b, n_ctx_tiles, first_tile):
    b, s, d = x.shape
    tm = ROW_TILE
    seg = lambda i: (i + first_tile >= n_ctx_tiles).astype(jnp.int32)
    rows = pl.BlockSpec((1, tm, d), lambda bi, i: (bi, i, 0))
    small = pl.BlockSpec((1, d), lambda bi, i: (0, 0))
    return pl.pallas_call(
        _final_norm_kernel,
        grid=(b, s // tm),
        in_specs=[rows, rows, pl.BlockSpec((1, 1, 6, d), lambda bi, i: (bi, seg(i), 0, 0)), small, small],
        out_specs=rows,
        out_shape=jax.ShapeDtypeStruct((b, s, d), F32),
        compiler_params=_cparams(2),
        name="final_norm",
    )(x, y, mods, ln_g.reshape(1, d), ln_b.reshape(1, d))


def _peer_scores_kernel(ht_ref, wq_ref, keys_ref, s_ref):
    q = jnp.dot(wq_ref[...], ht_ref[...], preferred_element_type=F32).astype(BF16)
    for g in range(2 * PEER_HEADS):
        s_ref[g] = jnp.dot(keys_ref[g], q[g * PEER_DK:(g + 1) * PEER_DK], preferred_element_type=F32)


def _peer_scores(ht, wq_t, keys, tn):
    d, n = ht.shape
    groups = 2 * PEER_HEADS
    return pl.pallas_call(
        _peer_scores_kernel,
        grid=(n // tn,),
        in_specs=[pl.BlockSpec((d, tn), lambda t: (0, t)),
                  pl.BlockSpec(wq_t.shape, lambda t: (0, 0)),
                  pl.BlockSpec(keys.shape, lambda t: (0, 0, 0))],
        out_specs=pl.BlockSpec((groups, PEER_NKEYS, tn), lambda t: (0, 0, t)),
        out_shape=jax.ShapeDtypeStruct((groups, PEER_NKEYS, n), F32),
        compiler_params=_cparams(1),
        name="peer_scores",
    )(ht, wq_t, keys)


PEER_RANKS = PEER_TOPK + 1


def _top_rows(s, count):
    rows = []
    for _ in range(count):
        m = jnp.max(s, axis=0, keepdims=True)
        rows.append(m)
        s = jnp.where(s == m, -jnp.inf, s)
    return rows


def _peer_route_kernel(s_ref, s2_ref, e2_ref, c1_ref, c2_ref, cand_ref):
    cand_ref[...] = jnp.full(cand_ref.shape, -jnp.inf, F32)
    for h in range(PEER_HEADS):
        s1 = s_ref[2 * h]
        s2 = s_ref[2 * h + 1]
        a = _top_rows(s1, PEER_RANKS)
        b = _top_rows(s2, PEER_RANKS)
        pairs = [(p, q) for p in range(PEER_RANKS) for q in range(PEER_RANKS // (p + 1))]
        for slot, (p, q) in enumerate(pairs):
            cand_ref[slot:slot + 1, :] = a[p] + b[q]
        top = _top_rows(cand_ref[...], PEER_RANKS)
        threshold = 0.5 * (top[PEER_TOPK - 1] + top[PEER_TOPK])
        z = sum(jnp.exp(row - top[0]) for row in top[:PEER_TOPK])
        s2_ref[h] = s2
        e2_ref[h] = jnp.exp(s2 - b[0])
        c1_ref[h] = threshold - s1
        c2_ref[h] = jnp.exp(s1 - a[0]) / z


PEER_CAND_ROWS = 56


def _peer_route(scores, tn):
    groups, nk, n = scores.shape
    spec = pl.BlockSpec((PEER_HEADS, nk, tn), lambda t: (0, 0, t))
    shape = jax.ShapeDtypeStruct((PEER_HEADS, nk, n), F32)
    return pl.pallas_call(
        _peer_route_kernel,
        grid=(n // tn,),
        in_specs=[pl.BlockSpec((groups, nk, tn), lambda t: (0, 0, t))],
        out_specs=[spec] * 4,
        out_shape=[shape] * 4,
        scratch_shapes=[pltpu.VMEM((PEER_CAND_ROWS, tn), F32)],
        compiler_params=_cparams(1),
        name="peer_route",
    )(scores)


def _gelu_tanh(x):
    return 0.5 * x * (1.0 + jnp.tanh(math.sqrt(2.0 / math.pi) * (x + 0.044715 * x * x * x)))


def _peer_experts_kernel(blocks, ht_ref, u_ref, vt_ref, s2_ref, e2_ref, c1_ref, c2_ref, o_ref, w_ref):
    e = pl.program_id(1)

    @pl.when(e == 0)
    def _():
        o_ref[...] = jnp.zeros_like(o_ref)

    ht = ht_ref[...]
    for ii in range(blocks):
        i = e * blocks + ii
        rows = slice(ii * PEER_NKEYS, (ii + 1) * PEER_NKEYS)
        act = _gelu_tanh(jnp.dot(u_ref[rows, :], ht, preferred_element_type=F32))
        gate = jnp.zeros_like(act)
        for h in range(PEER_HEADS):
            c1 = c1_ref[h, pl.ds(i, 1), :]
            c2 = c2_ref[h, pl.ds(i, 1), :]
            gate = gate + jnp.where(s2_ref[h] >= c1, e2_ref[h] * c2, 0.0)
        w_ref[rows, :] = (gate * act).astype(BF16)
    o_ref[...] += jnp.dot(vt_ref[...], w_ref[...], preferred_element_type=F32)


def _peer_experts(ht, u, vt, s2, e2, c1, c2, tn, ec):
    d, n = ht.shape
    blocks = ec // PEER_NKEYS
    route = pl.BlockSpec((PEER_HEADS, PEER_NKEYS, tn), lambda t, e: (0, 0, t))
    return pl.pallas_call(
        functools.partial(_peer_experts_kernel, blocks),
        grid=(n // tn, PEER_EXPERTS // ec),
        in_specs=[pl.BlockSpec((d, tn), lambda t, e: (0, t)),
                  pl.BlockSpec((ec, d), lambda t, e: (e, 0)),
                  pl.BlockSpec((d, ec), lambda t, e: (0, e)),
                  route, route, route, route],
        out_specs=pl.BlockSpec((d, tn), lambda t, e: (0, t)),
        out_shape=jax.ShapeDtypeStruct((d, n), F32),
        scratch_shapes=[pltpu.VMEM((ec, tn), BF16)],
        compiler_params=_cparams(2),
        name="peer_experts",
    )(ht, u, vt, s2, e2, c1, c2)


def _peer(h, w_q, keys, u_tab, v_tab):
    b, s, d = h.shape
    n = b * s
    tn = min(n, 512)
    ht = h.reshape(n, d).T
    wq_t = w_q.T.astype(BF16)
    kb = keys.reshape(2 * PEER_HEADS, PEER_NKEYS, PEER_DK).astype(BF16)
    scores = _peer_scores(ht, wq_t, kb, tn)
    s2, e2, c1, c2 = _peer_route(scores, min(n, 256))
    out_t = _peer_experts(ht, u_tab.astype(BF16), v_tab.T.astype(BF16), s2, e2, c1, c2, tn, 1024)
    return out_t.T.reshape(b, s, d)


def _rope_tables(t, n_ctx):
    rows = t // GRID_W
    row = np.repeat(np.arange(rows), GRID_W).astype(np.float32)
    col = np.tile(np.arange(GRID_W), rows).astype(np.float32)
    inv_freq = jnp.asarray(ROPE_THETA, F32) ** (-jnp.arange(ROPE_AXIS_FREQS, dtype=F32) / ROPE_AXIS_FREQS)
    ang = jnp.concatenate([jnp.asarray(row)[:, None] * inv_freq] * 2
                          + [jnp.asarray(col)[:, None] * inv_freq] * 2, axis=1)
    pad = lambda a, fill: jnp.concatenate([jnp.full((n_ctx, MLA_ROPE), fill, F32), a], axis=0)
    return pad(jnp.cos(ang), 1.0), pad(jnp.sin(ang), 0.0)


def _split_w_in(w_in):
    ml_cols = 4 * ML_W + ML_GATE_COLS
    w_ml = jnp.pad(w_in[:, :ml_cols], ((0, 0), (0, ML_PAD - ml_cols)))
    w_rw = w_in[:, ml_cols:ml_cols + RW_COLS]
    w_at = w_in[:, ml_cols + RW_COLS:]
    k_rope = w_at[:, MLA_Q_RANK + MLA_KV_RANK:]
    w_at = jnp.concatenate([w_at, _rot_cols(k_rope)], axis=1)
    w_at = jnp.pad(w_at, ((0, 0), (0, MLA_PAD - w_at.shape[1])))
    return w_ml.astype(BF16), w_rw.astype(BF16), w_at.astype(BF16)


def kernel(x, c, ctx, c_ctx, w_mod, b_mod, w_in, ml_conv_w, ml_conv_b, ml_i_bias, ml_f_bias, ml_norm_g,
           rw_mu, rw_w0, rw_w_up, rw_a0, rw_a_up, rw_g_up, rw_k_k, rw_k_a, rw_r_k, rw_gn_g, rw_gn_b,
           mla_q_norm_g, mla_q_up, mla_kv_norm_g, mla_kv_up, w_out, ln_mix_g, ln_mix_b,
           peer_w_q, peer_keys, peer_u, peer_v, ln_ffn_g, ln_ffn_b):
    b, t, d = x.shape
    n_ctx = ctx.shape[1]
    assert n_ctx % ROW_TILE == 0 and t % ROW_TILE == 0 and d == D_MODEL
    n_ctx_tiles = n_ctx // ROW_TILE
    cos, sin = _rope_tables(t, n_ctx)
    mod_rows = -(-(b + 1) // SUBLANES) * SUBLANES
    cvec = jnp.concatenate([c, c_ctx[None], jnp.zeros((mod_rows - b - 1, d), F32)], axis=0)

    x_all = jnp.concatenate([ctx, x], axis=1)
    depth = w_mod.shape[0]
    for l in range(depth):
        need_ctx = l < depth - 1
        mod = _modulation(cvec, w_mod[l], b_mod[l]).reshape(mod_rows, 6, d)
        mods = jnp.stack([jnp.broadcast_to(mod[b], (b, 6, d)), mod[:b]], axis=1)

        pml, prw, pat = _in_proj(x_all, mods, *_split_w_in(w_in[l]), n_ctx_tiles)

        q, k, lg = _ml_prep(pml, ml_conv_w[l], ml_conv_b[l], ml_i_bias[l], ml_f_bias[l], n_ctx_tiles)
        hf, hb = _ml_scan(q, k, pml[:, :, 2 * ML_W:3 * ML_W], lg, n_ctx)

        r, kh, v, g, bonus, w, ab, kt, vhi, vmid, vlo = _rw_prep(
            prw, rw_mu[l], rw_w0[l], rw_w_up[l], rw_a0[l], rw_a_up[l], rw_g_up[l], rw_k_k[l], rw_k_a[l],
            rw_r_k[l], n_ctx_tiles)
        yf, yb = _rw_scan(r, kh, w, ab, kt, vhi, vmid, vlo, n_ctx)

        qn, qr, kn, va, kr = _mla_prep(pat, mla_q_norm_g[l], mla_q_up[l], mla_kv_norm_g[l], mla_kv_up[l],
                                       cos, sin)
        at_ctx, at_lat = _mla(qn, qr, kn, va, kr, n_ctx, need_ctx)
        first_tile = 0 if need_ctx else n_ctx_tiles
        y_at = jnp.concatenate([at_ctx, at_lat], axis=1) if need_ctx else jnp.pad(
            at_lat, ((0, 0), (n_ctx, 0), (0, 0)))

        x_mid, h_ffn = _out_proj(x_all, mods, pml, hf, hb, yf, yb, bonus, g, y_at, ml_norm_g[l], rw_gn_g[l],
                                 rw_gn_b[l], w_out[l], ln_mix_g[l], ln_mix_b[l], n_ctx_tiles, first_tile)
        y = _peer(h_ffn, peer_w_q[l], peer_keys[l], peer_u[l], peer_v[l])
        x_all = _final_norm(x_mid, y, mods, ln_ffn_g[l], ln_ffn_b[l], n_ctx_tiles, first_tile)
    return x_all
```

```python
import functools
import math

import jax
import jax.numpy as jnp
import numpy as np
from jax import lax
from jax.experimental import pallas as pl
from jax.experimental.pallas import tpu as pltpu

F32 = jnp.float32
BF16 = jnp.bfloat16
HIGHEST = lax.Precision.HIGHEST

LANES = 128
SUBLANES = 8

D_MODEL = 1024
DEPTH = 2
GRID_W = 64

ML_HEADS = 4
ML_DH = 64
ML_W = ML_HEADS * ML_DH
ML_CHUNK = 64
ML_M_INIT = -1e30
ML_GATE_COLS = 4 * ML_HEADS
ML_PAD = 4 * ML_W + LANES

RW_HEADS = 6
RW_DH = 64
RW_W = RW_HEADS * RW_DH
RW_PAIRS = RW_HEADS // 2
RW_LORA = 64
RW_G_LORA = 128
RW_DECAY_SCALE = math.exp(-0.5)
RW_GN_EPS = 64e-5
RW_COLS = 3 * RW_W + 4 * RW_LORA + RW_G_LORA
RW_CHUNK = 64

MLA_HEADS = 6
MLA_NOPE = 64
MLA_ROPE = 32
MLA_V = 64
MLA_Q_RANK = 384
MLA_KV_RANK = 256
MLA_W = MLA_HEADS * MLA_V
MLA_SCALE = (MLA_NOPE + MLA_ROPE) ** -0.5
MLA_PAD = 768
ROPE_AXIS_FREQS = MLA_ROPE // 4
ROPE_THETA = 10000.0

PEER_HEADS = 8
PEER_NKEYS = 128
PEER_EXPERTS = PEER_NKEYS * PEER_NKEYS
PEER_DK = 128
PEER_TOPK = 16

DEEPNORM_ALPHA = (2 * DEPTH) ** 0.25
LN_EPS = 1e-6

ROW_TILE = 256
VMEM_LIMIT = 56 * 1024 * 1024


def _cparams(n_axes):
    return pltpu.CompilerParams(dimension_semantics=("arbitrary",) * n_axes,
                                vmem_limit_bytes=VMEM_LIMIT)


def _ln_rows(x, eps=LN_EPS):
    mu = jnp.mean(x, axis=-1, keepdims=True)
    xc = x - mu
    var = jnp.mean(xc * xc, axis=-1, keepdims=True)
    return xc * lax.rsqrt(var + eps)


def _sigmoid(x):
    return 1.0 / (1.0 + jnp.exp(-x))


def _group_ln(x, gmat, eps):
    mu = jnp.dot(x, gmat, precision=HIGHEST, preferred_element_type=F32)
    xc = x - mu
    var = jnp.dot(xc * xc, gmat, precision=HIGHEST, preferred_element_type=F32)
    return xc * lax.rsqrt(var + eps)


def _group_matrix(width, group, value):
    idx = np.arange(width) // group
    return jnp.asarray((idx[:, None] == idx[None, :]).astype(np.float32) * value)


def _mod_kernel(c_ref, w_ref, b_ref, o_ref):
    c = c_ref[...]
    s = c * _sigmoid(c)
    o_ref[...] = jnp.dot(s, w_ref[...], precision=HIGHEST, preferred_element_type=F32) + b_ref[...]


def _modulation(cvec, w_mod, b_mod):
    rows, d = cvec.shape
    n = w_mod.shape[1]
    tn = 1536
    return pl.pallas_call(
        _mod_kernel,
        grid=(n // tn,),
        in_specs=[pl.BlockSpec((rows, d), lambda j: (0, 0)),
                  pl.BlockSpec((d, tn), lambda j: (0, j)),
                  pl.BlockSpec((1, tn), lambda j: (0, j))],
        out_specs=pl.BlockSpec((rows, tn), lambda j: (0, j)),
        out_shape=jax.ShapeDtypeStruct((rows, n), F32),
        compiler_params=_cparams(1),
        name="modulation",
    )(cvec, w_mod, b_mod.reshape(1, n))


def _in_proj_kernel(x_ref, mod_ref, wml_ref, wrw_ref, wat_ref, pml_ref, prw_ref, pat_ref):
    mod = mod_ref[0, 0]
    h = _ln_rows(x_ref[0]) * (1.0 + mod[1:2]) + mod[0:1]
    hb = h.astype(BF16)
    pml_ref[0] = jnp.dot(hb, wml_ref[...], preferred_element_type=F32)
    prw_ref[0] = jnp.dot(hb, wrw_ref[...], preferred_element_type=F32)
    pat_ref[0] = jnp.dot(hb, wat_ref[...], preferred_element_type=F32)


def _in_proj(x_all, mods, wml, wrw, wat, n_ctx_tiles):
    b, s, d = x_all.shape
    tm = ROW_TILE
    seg = lambda i: (i >= n_ctx_tiles).astype(jnp.int32)
    full = lambda w: pl.BlockSpec(w.shape, lambda bi, i: (0, 0))
    out = lambda w: pl.BlockSpec((1, tm, w.shape[1]), lambda bi, i: (bi, i, 0))
    return pl.pallas_call(
        _in_proj_kernel,
        grid=(b, s // tm),
        in_specs=[pl.BlockSpec((1, tm, d), lambda bi, i: (bi, i, 0)),
                  pl.BlockSpec((1, 1, 6, d), lambda bi, i: (bi, seg(i), 0, 0)),
                  full(wml), full(wrw), full(wat)],
        out_specs=[out(wml), out(wrw), out(wat)],
        out_shape=[jax.ShapeDtypeStruct((b, s, w.shape[1]), F32) for w in (wml, wrw, wat)],
        compiler_params=_cparams(2),
        name="in_proj",
    )(x_all, mods, wml, wrw, wat)


def _halo_specs(width, col_block, tm, s):
    per = tm // SUBLANES
    last = s // SUBLANES - 1
    prev = pl.BlockSpec((1, SUBLANES, width),
                        lambda bi, i: (bi, jnp.maximum(i * per - 1, 0), col_block))
    nxt = pl.BlockSpec((1, SUBLANES, width),
                       lambda bi, i: (bi, jnp.minimum((i + 1) * per, last), col_block))
    return prev, nxt


def _neighbours(x, prev_blk, next_blk, tile, n_ctx_tiles, n_tiles):
    tm = x.shape[0]
    row = lax.broadcasted_iota(jnp.int32, x.shape, 0)
    starts = jnp.logical_or(tile == 0, tile == n_ctx_tiles)
    ends = jnp.logical_or(tile == n_ctx_tiles - 1, tile == n_tiles - 1)
    prev_row = jnp.where(starts, 0.0, prev_blk[SUBLANES - 1:SUBLANES, :])
    next_row = jnp.where(ends, 0.0, next_blk[0:1, :])
    prev = jnp.where(row == 0, prev_row, pltpu.roll(x, 1, 0))
    nxt = jnp.where(row == tm - 1, next_row, pltpu.roll(x, tm - 1, 0))
    return prev, nxt


def _ml_prep_kernel(n_ctx_tiles, n_tiles, qk_ref, prev_ref, next_ref, gate_ref, cw_ref, cb_ref,
                    gb_ref, fmask_ref, q_ref, k_ref, lg_ref):
    tile = pl.program_id(1)
    x = qk_ref[0]
    prev, nxt = _neighbours(x, prev_ref[0], next_ref[0], tile, n_ctx_tiles, n_tiles)
    cw = cw_ref[...]
    z = prev * cw[0:1] + x * cw[1:2] + nxt * cw[2:3] + cb_ref[...]
    z = z * _sigmoid(z)
    q_ref[0] = z[:, :ML_W] * (ML_DH ** -0.5)
    k_ref[0] = z[:, ML_W:]
    g = gate_ref[0] + gb_ref[...]
    log_sig = jnp.minimum(g, 0.0) - jnp.log1p(jnp.exp(-jnp.abs(g)))
    lg_ref[0] = jnp.where(fmask_ref[...] > 0.5, log_sig, g)


def _ml_prep(pml, conv_w, conv_b, i_bias, f_bias, n_ctx_tiles):
    b, s, _ = pml.shape
    tm = ROW_TILE
    n_tiles = s // tm
    gate_bias = jnp.stack([i_bias, f_bias], axis=1).reshape(1, ML_GATE_COLS)
    gate_bias = jnp.pad(gate_bias, ((0, 0), (0, LANES - ML_GATE_COLS)))
    fmask = np.zeros((2, 2, ML_HEADS), np.float32)
    fmask[:, 1] = 1.0
    fmask = jnp.asarray(np.pad(fmask.reshape(1, -1), ((0, 0), (0, LANES - ML_GATE_COLS))))
    prev, nxt = _halo_specs(2 * ML_W, 0, tm, s)
    small = lambda a: pl.BlockSpec(a.shape, lambda bi, i: (0, 0))
    cb = conv_b.reshape(1, -1)
    return pl.pallas_call(
        functools.partial(_ml_prep_kernel, n_ctx_tiles, n_tiles),
        grid=(b, n_tiles),
        in_specs=[pl.BlockSpec((1, tm, 2 * ML_W), lambda bi, i: (bi, i, 0)), prev, nxt,
                  pl.BlockSpec((1, tm, LANES), lambda bi, i: (bi, i, 4 * ML_W // LANES)),
                  small(conv_w), small(cb), small(gate_bias), small(fmask)],
        out_specs=[pl.BlockSpec((1, tm, ML_W), lambda bi, i: (bi, i, 0)),
                   pl.BlockSpec((1, tm, ML_W), lambda bi, i: (bi, i, 0)),
                   pl.BlockSpec((1, tm, LANES), lambda bi, i: (bi, i, 0))],
        out_shape=[jax.ShapeDtypeStruct((b, s, ML_W), F32),
                   jax.ShapeDtypeStruct((b, s, ML_W), F32),
                   jax.ShapeDtypeStruct((b, s, LANES), F32)],
        compiler_params=_cparams(2),
        name="mlstm_prep",
    )(pml, pml, pml, pml, conv_w, cb, gate_bias, fmask)


def _ml_scan_kernel(n_chunks, q_ref, k_ref, kt_ref, v_ref, ic_ref, ir_ref, fc_ref, fr_ref, h_ref):
    L = ML_CHUNK
    row = lax.broadcasted_iota(jnp.int32, (L, L), 0)
    col = lax.broadcasted_iota(jnp.int32, (L, L), 1)
    causal = col <= row
    dot = lambda a, b: jnp.dot(a.astype(BF16), b.astype(BF16), preferred_element_type=F32)

    def chunk_dir(c, d, carry):
        ct, n, m = carry
        q = q_ref[0, d, 0, c]
        k = k_ref[0, d, 0, c]
        kt = kt_ref[0, d, 0, c]
        v = v_ref[0, d, 0, c]
        i_col = ic_ref[0, d, 0, c]
        i_row = ir_ref[0, d, 0, c]
        f_col = fc_ref[0, d, 0, c]
        f_row = fr_ref[0, d, 0, c]
        bcum_col = jnp.sum(jnp.where(causal, f_row, 0.0), axis=1, keepdims=True)
        bcum_row = jnp.sum(jnp.where(row <= col, f_col, 0.0), axis=0, keepdims=True)
        dmat = jnp.where(causal, bcum_col - bcum_row + i_row, -jnp.inf)
        a_inter = bcum_col + m
        m_t = jnp.maximum(a_inter, jnp.max(dmat, axis=1, keepdims=True))
        w_inter = jnp.exp(a_inter - m_t)
        s = dot(q, kt) * jnp.exp(dmat - m_t)
        num = w_inter * dot(q, ct) + dot(s, v)
        den = w_inter * jnp.sum(q * n, axis=1, keepdims=True) + jnp.sum(s, axis=1, keepdims=True)
        h_ref[0, d, 0, c] = num / jnp.maximum(jnp.abs(den), jnp.exp(-m_t))
        b_end = bcum_col[L - 1:L, :]
        g = b_end - bcum_col + i_col
        m_new = jnp.maximum(b_end + m, jnp.max(g, axis=0, keepdims=True))
        decay = jnp.exp(b_end + m - m_new)
        wk = jnp.exp(g - m_new)
        ct = decay * ct + dot(kt, wk * v)
        n = decay * n + jnp.sum(wk * k, axis=0, keepdims=True)
        return ct, n, m_new

    def chunk(c, carry):
        return tuple(chunk_dir(c, d, carry[d]) for d in range(2))

    init = (jnp.zeros((ML_DH, ML_DH), F32), jnp.zeros((1, ML_DH), F32),
            jnp.full((1, 1), ML_M_INIT, F32))
    lax.fori_loop(0, n_chunks, chunk, (init, init))


def _flip_segments(a, n_ctx, axis):
    ctx, lat = jnp.split(a, [n_ctx], axis=axis)
    return jnp.concatenate([jnp.flip(ctx, axis), jnp.flip(lat, axis)], axis=axis)


def _ml_scan(q, k, v, lg, n_ctx):
    b, s, _ = q.shape
    L = ML_CHUNK
    nc = s // L
    heads = lambda a: a.reshape(b, s, ML_HEADS, ML_DH).transpose(0, 2, 1, 3)
    both = lambda a: jnp.stack([a, _flip_segments(a, n_ctx, 2)], axis=1)
    chunks = lambda a: a.reshape(b, 2, ML_HEADS, nc, L, a.shape[-1])
    qd, kd, vd = (chunks(both(heads(a))) for a in (q, k, v))
    ktd = jnp.swapaxes(kd, -1, -2)
    gates = lg[:, :, :ML_GATE_COLS].reshape(b, s, 2, 2, ML_HEADS).transpose(2, 3, 0, 4, 1)
    gates = jnp.stack([gates[0], _flip_segments(gates[1], n_ctx, 3)], axis=0)
    gates = gates.transpose(1, 2, 0, 3, 4).reshape(2, b, 2, ML_HEADS, nc, L)
    col = lambda a: a[..., None]
    row = lambda a: a[..., None, :]
    mat = pl.BlockSpec((1, 2, 1, nc, L, ML_DH), lambda bi, hi: (bi, 0, hi, 0, 0, 0))
    cspec = pl.BlockSpec((1, 2, 1, nc, L, 1), lambda bi, hi: (bi, 0, hi, 0, 0, 0))
    rspec = pl.BlockSpec((1, 2, 1, nc, 1, L), lambda bi, hi: (bi, 0, hi, 0, 0, 0))
    h = pl.pallas_call(
        functools.partial(_ml_scan_kernel, nc),
        grid=(b, ML_HEADS),
        in_specs=[mat, mat, mat, mat, cspec, rspec, cspec, rspec],
        out_specs=mat,
        out_shape=jax.ShapeDtypeStruct((b, 2, ML_HEADS, nc, L, ML_DH), F32),
        compiler_params=_cparams(2),
        name="mlstm_scan",
    )(qd, kd, ktd, vd, col(gates[0]), row(gates[0]), col(gates[1]), row(gates[1]))
    h = h.reshape(b, 2, ML_HEADS, s, ML_DH)
    merge = lambda a: a.transpose(0, 2, 1, 3).reshape(b, s, ML_W)
    return merge(h[:, 0]), merge(_flip_segments(h[:, 1], n_ctx, 2))


def _rw_prep_kernel(n_ctx_tiles, n_tiles, p_ref, prev_ref, next_ref, mu_ref, wup_ref, aup_ref,
                    gup_ref, w0_ref, a0_ref, kk_ref, ka_ref, rho_ref, gsum_ref,
                    r_ref, kh_ref, v_ref, g_ref, bonus_ref, w_ref, ab_ref, kt_ref,
                    vhi_ref, vmid_ref, vlo_ref):
    tile = pl.program_id(1)
    p = p_ref[0]
    prev, nxt = _neighbours(p, prev_ref[0], next_ref[0], tile, n_ctx_tiles, n_tiles)
    mu = mu_ref[...]
    p = p + mu[0:1] * (prev - p) + mu[1:2] * (nxt - p)
    r = p[:, 0:RW_W]
    k = p[:, RW_W:2 * RW_W]
    v = p[:, 2 * RW_W:3 * RW_W]
    base = 3 * RW_W
    wd = p[:, base:base + 2 * RW_LORA]
    ad = p[:, base + 2 * RW_LORA:base + 4 * RW_LORA]
    gd = p[:, base + 4 * RW_LORA:base + 4 * RW_LORA + RW_G_LORA]
    dot = functools.partial(jnp.dot, preferred_element_type=F32)
    w = jnp.exp(-RW_DECAY_SCALE * _sigmoid(w0_ref[...] + dot(jnp.tanh(wd).astype(BF16), wup_ref[...])))
    a = _sigmoid(a0_ref[...] + dot(ad.astype(BF16), aup_ref[...]))
    g = dot(_sigmoid(gd).astype(BF16), gup_ref[...])
    kk = k * kk_ref[...]
    ss = jnp.dot(kk * kk, gsum_ref[...], precision=HIGHEST, preferred_element_type=F32)
    kh = kk * lax.rsqrt(ss + 1e-12)
    ka = ka_ref[...]
    rk = r * rho_ref[...]
    kt_sum = jnp.zeros_like(k)
    for dr in range(2):
        a_d = a[:, dr * RW_W:(dr + 1) * RW_W]
        kt_d = k * (1.0 + (a_d - 1.0) * ka)
        kt_ref[0, :, dr * RW_W:(dr + 1) * RW_W] = kt_d
        ab_ref[0, :, dr * RW_W:(dr + 1) * RW_W] = kh * a_d
        kt_sum = kt_sum + kt_d
    bonus_ref[0] = jnp.dot(rk * kt_sum, gsum_ref[...], precision=HIGHEST,
                           preferred_element_type=F32) * v
    r_ref[0] = r
    kh_ref[0] = kh
    v_ref[0] = v
    g_ref[0] = g
    w_ref[0] = w
    hi = v.astype(BF16)
    rem = v - hi.astype(F32)
    mid = rem.astype(BF16)
    vhi_ref[0] = hi
    vmid_ref[0] = mid
    vlo_ref[0] = (rem - mid.astype(F32)).astype(BF16)


def _block_diag2(up):
    z = jnp.zeros_like(up[0])
    return jnp.concatenate([jnp.concatenate([up[0], z], axis=1),
                            jnp.concatenate([z, up[1]], axis=1)], axis=0)


def _rw_prep(prw, mu, w0, w_up, a0, a_up, g_up, k_k, k_a, r_k, n_ctx_tiles):
    b, s, _ = prw.shape
    tm = ROW_TILE
    n_tiles = s // tm
    prev, nxt = _halo_specs(RW_COLS, 0, tm, s)
    small = lambda a: pl.BlockSpec(a.shape, lambda bi, i: (0, 0))
    consts = [mu, _block_diag2(w_up).astype(BF16), _block_diag2(a_up).astype(BF16),
              g_up.astype(BF16), w0.reshape(1, 2 * RW_W), a0.reshape(1, 2 * RW_W),
              k_k.reshape(1, RW_W), k_a.reshape(1, RW_W), r_k.reshape(1, RW_W),
              _group_matrix(RW_W, RW_DH, 1.0)]
    one = lambda width, dt=F32: (pl.BlockSpec((1, tm, width), lambda bi, i: (bi, i, 0)),
                                 jax.ShapeDtypeStruct((b, s, width), dt))
    outs = [one(RW_W)] * 5 + [one(2 * RW_W)] * 3 + [one(RW_W, BF16)] * 3
    return pl.pallas_call(
        functools.partial(_rw_prep_kernel, n_ctx_tiles, n_tiles),
        grid=(b, n_tiles),
        in_specs=[pl.BlockSpec((1, tm, RW_COLS), lambda bi, i: (bi, i, 0)), prev, nxt]
                 + [small(a) for a in consts],
        out_specs=[o[0] for o in outs],
        out_shape=[o[1] for o in outs],
        compiler_params=_cparams(2),
        name="rwkv_prep",
    )(prw, prw, prw, *consts)


def _rw_scan_kernel(batch, *refs):
    ins = (refs[0:6], refs[6:12])
    ones_ref = refs[12]
    y_refs = refs[13:15]
    state_ref, yacc_ref, xs_ref, z_ref = refs[15:19]
    L = RW_CHUNK
    items = batch * RW_PAIRS
    per_dir = items * RW_DH

    @pl.when(pl.program_id(0) == 0)
    def _():
        state_ref[...] = jnp.zeros_like(state_ref)

    yacc_ref[...] = jnp.zeros_like(yacc_ref)
    lane = lax.broadcasted_iota(jnp.int32, (RW_DH, LANES), 1)
    step_of_lane = jnp.where(lane < RW_DH, lane, lane - RW_DH)
    sel_row = lax.broadcasted_iota(jnp.int32, (LANES, LANES), 0)
    sel_col = lax.broadcasted_iota(jnp.int32, (LANES, LANES), 1)
    same_head = (sel_row < RW_DH) == (sel_col < RW_DH)
    sel_step = jnp.where(sel_row < RW_DH, sel_row, sel_row - RW_DH)

    def item_rows(d, bi, p):
        start = ((d * batch + bi) * RW_PAIRS + p) * RW_DH
        return slice(start, start + RW_DH)

    def step(j, _):
        steps = (j, L - 1 - j)
        rows_t = [[ref[steps[d]] for ref in ins[d][:5]] for d in range(2)]
        vec = lambda rows, bi, p: rows[bi:bi + 1, p * LANES:(p + 1) * LANES]

        for d in range(2):
            kh_t = rows_t[d][3]
            for bi in range(batch):
                for p in range(RW_PAIRS):
                    rows = item_rows(d, bi, p)
                    x = state_ref[rows, :] * vec(kh_t, bi, p)
                    hi = x.astype(BF16)
                    xs_ref[rows, 0:LANES] = hi
                    xs_ref[rows, LANES:2 * LANES] = (x - hi.astype(F32)).astype(BF16)
        sa = jnp.dot(xs_ref[...], ones_ref[...], preferred_element_type=F32)
        vb = []
        for d in range(2):
            select = jnp.where(jnp.logical_and(same_head, sel_step == steps[d]), 1.0, 0.0).astype(BF16)
            vb.append(jnp.dot(ins[d][5][0], jnp.concatenate([select] * 3, axis=0),
                              preferred_element_type=F32))

        for d in range(2):
            w_t, ab_t, kt_t, _, r_t = rows_t[d]
            for bi in range(batch):
                for p in range(RW_PAIRS):
                    rows = item_rows(d, bi, p)
                    local = slice(rows.start - d * per_dir, rows.stop - d * per_dir)
                    st = (state_ref[rows, :] * vec(w_t, bi, p) - sa[rows] * vec(ab_t, bi, p)
                          + vb[d][local] * vec(kt_t, bi, p))
                    state_ref[rows, :] = st
                    z_ref[rows, :] = (st * vec(r_t, bi, p)).astype(BF16)
        y = jnp.dot(z_ref[...], ones_ref[0:LANES, :], preferred_element_type=F32)
        for d in range(2):
            here = step_of_lane == steps[d]
            for bi in range(batch):
                for p in range(RW_PAIRS):
                    rows = item_rows(d, bi, p)
                    yacc_ref[rows, :] = jnp.where(here, y[rows], yacc_ref[rows, :])
        return 0

    lax.fori_loop(0, L, step, 0)
    for d in range(2):
        y_refs[d][0] = yacc_ref[d * per_dir:(d + 1) * per_dir, :]


def _pair_layout(a, nc):
    b = a.shape[0]
    a = a.reshape(b, nc, RW_CHUNK, RW_PAIRS, 2, RW_DH)
    return a.transpose(1, 0, 3, 5, 4, 2).reshape(nc, b * RW_PAIRS * RW_DH, LANES)


def _pair_unlayout(a, b):
    nc = a.shape[0]
    a = a.reshape(nc, b, RW_PAIRS, RW_DH, 2, RW_CHUNK)
    return a.transpose(1, 0, 5, 2, 4, 3).reshape(b, nc * RW_CHUNK, RW_W)


def _rw_scan(r, kh, w, ab, kt, vhi, vmid, vlo, n_ctx):
    b, s, _ = r.shape
    L = RW_CHUNK
    nc = s // L
    ncc = n_ctx // L
    rows = b * RW_PAIRS * RW_DH
    cmap = (lambda c: c,
            lambda c: jnp.where(c < ncc, ncc - 1 - c, ncc + nc - 1 - c))
    vcat = jnp.concatenate([_pair_layout(a, nc) for a in (vhi, vmid, vlo)], axis=-1)
    time_major = lambda a: a.transpose(1, 0, 2)
    head_ones = _group_matrix(LANES, RW_DH, 1.0).astype(BF16)
    head_ones = jnp.concatenate([head_ones, head_ones], axis=0)
    in_specs, args = [], []
    for d in range(2):
        cm = cmap[d]
        per_dir = lambda c, cm=cm, d=d: (cm(c), 0, d)
        shared = lambda c, cm=cm: (cm(c), 0, 0)
        in_specs += [pl.BlockSpec((L, b, RW_W), per_dir)] * 3
        in_specs += [pl.BlockSpec((L, b, RW_W), shared)] * 2
        in_specs += [pl.BlockSpec((1, rows, 3 * LANES), shared)]
        args += [time_major(a) for a in (w, ab, kt, kh, r)] + [vcat]
    in_specs.append(pl.BlockSpec(head_ones.shape, lambda c: (0, 0)))
    args.append(head_ones)
    out_specs = [pl.BlockSpec((1, rows, LANES), lambda c, cm=cm: (cm(c), 0, 0)) for cm in cmap]
    ys = pl.pallas_call(
        functools.partial(_rw_scan_kernel, b),
        grid=(nc,),
        in_specs=in_specs,
        out_specs=out_specs,
        out_shape=[jax.ShapeDtypeStruct((nc, rows, LANES), F32)] * 2,
        scratch_shapes=[pltpu.VMEM((2 * rows, LANES), F32), pltpu.VMEM((2 * rows, LANES), F32),
                        pltpu.VMEM((2 * rows, 2 * LANES), BF16), pltpu.VMEM((2 * rows, LANES), BF16)],
        compiler_params=_cparams(1),
        name="rwkv_scan",
    )(*args)
    return _pair_unlayout(ys[0], b), _pair_unlayout(ys[1], b)


def _mla_prep_kernel(p_ref, cosq_ref, sinq_ref, cosk_ref, sink_ref, qg_ref, kvg_ref, wq_ref, wkv_ref,
                     qn_ref, qr_ref, kn_ref, v_ref, kr_ref):
    p = p_ref[0]
    cq = p[:, :MLA_Q_RANK]
    ckv = p[:, MLA_Q_RANK:MLA_Q_RANK + MLA_KV_RANK]
    base = MLA_Q_RANK + MLA_KV_RANK
    k_rope = p[:, base:base + MLA_ROPE]
    k_rot = p[:, base + MLA_ROPE:base + 2 * MLA_ROPE]
    rms = lambda x, g: x * lax.rsqrt(jnp.mean(x * x, axis=-1, keepdims=True) + LN_EPS) * g
    q = jnp.dot(rms(cq, qg_ref[...]).astype(BF16), wq_ref[...], preferred_element_type=F32)
    kv = jnp.dot(rms(ckv, kvg_ref[...]).astype(BF16), wkv_ref[...], preferred_element_type=F32)
    nope = MLA_HEADS * MLA_NOPE
    rope = MLA_HEADS * MLA_ROPE
    qn_ref[0] = q[:, :nope]
    qr_ref[0] = q[:, nope:nope + rope] * cosq_ref[...] + q[:, nope + rope:nope + 2 * rope] * sinq_ref[...]
    kn_ref[0] = kv[:, :nope]
    v_ref[0] = kv[:, nope:]
    kr_ref[0] = k_rope * cosk_ref[...] + k_rot * sink_ref[...]


def _rot_cols(w):
    shape = w.shape
    w = w.reshape(shape[:-1] + (shape[-1] // 16, 2, 8))
    return jnp.concatenate([-w[..., 1:2, :], w[..., 0:1, :]], axis=-2).reshape(shape)


def _mla_prep(pat, q_norm_g, q_up, kv_norm_g, kv_up, cos, sin):
    b, s, _ = pat.shape
    tm = ROW_TILE
    qw = q_up.reshape(MLA_Q_RANK, MLA_HEADS, MLA_NOPE + MLA_ROPE)
    q_nope = qw[:, :, :MLA_NOPE].reshape(MLA_Q_RANK, -1)
    q_rope = qw[:, :, MLA_NOPE:]
    wq = jnp.concatenate([q_nope, q_rope.reshape(MLA_Q_RANK, -1),
                          _rot_cols(q_rope).reshape(MLA_Q_RANK, -1)], axis=1).astype(BF16)
    kvw = kv_up.reshape(MLA_KV_RANK, MLA_HEADS, MLA_NOPE + MLA_V)
    wkv = jnp.concatenate([kvw[:, :, :MLA_NOPE].reshape(MLA_KV_RANK, -1),
                           kvw[:, :, MLA_NOPE:].reshape(MLA_KV_RANK, -1)], axis=1).astype(BF16)
    cosq = jnp.tile(cos, (1, MLA_HEADS))
    sinq = jnp.tile(sin, (1, MLA_HEADS))
    rows = lambda width: pl.BlockSpec((tm, width), lambda bi, i: (i, 0))
    small = lambda a: pl.BlockSpec(a.shape, lambda bi, i: (0, 0))
    qg = q_norm_g.reshape(1, -1)
    kvg = kv_norm_g.reshape(1, -1)
    out = lambda width: (pl.BlockSpec((1, tm, width), lambda bi, i: (bi, i, 0)),
                         jax.ShapeDtypeStruct((b, s, width), F32))
    outs = [out(MLA_HEADS * MLA_NOPE), out(MLA_HEADS * MLA_ROPE), out(MLA_HEADS * MLA_NOPE),
            out(MLA_W), out(MLA_ROPE)]
    return pl.pallas_call(
        _mla_prep_kernel,
        grid=(b, s // tm),
        in_specs=[pl.BlockSpec((1, tm, MLA_PAD), lambda bi, i: (bi, i, 0)),
                  rows(MLA_HEADS * MLA_ROPE), rows(MLA_HEADS * MLA_ROPE), rows(MLA_ROPE), rows(MLA_ROPE),
                  small(qg), small(kvg), small(wq), small(wkv)],
        out_specs=[o[0] for o in outs],
        out_shape=[o[1] for o in outs],
        compiler_params=_cparams(2),
        name="mla_prep",
    )(pat, cosq, sinq, cos, sin, qg, kvg, wq, wkv)


def _attn_kernel(q_ref, k_ref, v_ref, o_ref):
    s = lax.dot_general(q_ref[0, 0], k_ref[0, 0], (((1,), (1,)), ((), ())),
                        preferred_element_type=F32) * MLA_SCALE
    p = jnp.exp(s - jnp.max(s, axis=-1, keepdims=True))
    denom = jnp.sum(p, axis=-1, keepdims=True)
    o_ref[0, 0] = jnp.dot(p.astype(BF16), v_ref[0, 0], preferred_element_type=F32) / denom


def _attention(q, k, v, tq):
    b, h, t, dq = q.shape
    sk = k.shape[2]
    dv = v.shape[3]
    return pl.pallas_call(
        _attn_kernel,
        grid=(b, h, t // tq),
        in_specs=[pl.BlockSpec((1, 1, tq, dq), lambda bi, hi, i: (bi, hi, i, 0)),
                  pl.BlockSpec((1, 1, sk, dq), lambda bi, hi, i: (bi, hi, 0, 0)),
                  pl.BlockSpec((1, 1, sk, dv), lambda bi, hi, i: (bi, hi, 0, 0))],
        out_specs=pl.BlockSpec((1, 1, tq, dv), lambda bi, hi, i: (bi, hi, i, 0)),
        out_shape=jax.ShapeDtypeStruct((b, h, t, dv), F32),
        compiler_params=_cparams(3),
        name="mla_attention",
    )(q, k, v)


def _mla(qn, qr, kn, v, kr, n_ctx, need_ctx):
    b, s, _ = qn.shape
    heads = lambda a, w: a.reshape(b, s, MLA_HEADS, w).transpose(0, 2, 1, 3)
    q = jnp.concatenate([heads(qn, MLA_NOPE), heads(qr, MLA_ROPE)], axis=-1).astype(BF16)
    k_rope = jnp.broadcast_to(kr[:, None], (b, MLA_HEADS, s, MLA_ROPE))
    k = jnp.concatenate([heads(kn, MLA_NOPE), k_rope], axis=-1).astype(BF16)
    vh = heads(v, MLA_V).astype(BF16)
    pad = ((0, 0), (0, 0), (0, 0), (0, LANES - MLA_NOPE - MLA_ROPE))
    q, k = jnp.pad(q, pad), jnp.pad(k, pad)
    merge = lambda a: a.transpose(0, 2, 1, 3).reshape(b, -1, MLA_W)
    t = s - n_ctx
    y_lat = merge(_attention(q[:, :, n_ctx:], k, vh, min(t, 512)))
    if not need_ctx:
        return None, y_lat
    y_ctx = merge(_attention(q[:, :, :n_ctx], k[:, :, :n_ctx], vh[:, :, :n_ctx], n_ctx))
    return y_ctx, y_lat


def _out_proj_kernel(x_ref, mod_ref, hf_ref, hb_ref, o_ref, yf_ref, yb_ref, bonus_ref, g_ref, at_ref,
                     g4_ref, g6_ref, mlg_ref, gng_ref, gnb_ref, wml_ref, wrw_ref, wat_ref,
                     lng_ref, lnb_ref, xo_ref, ho_ref):
    mod = mod_ref[0, 0]
    ml = _sigmoid(o_ref[0]) * _group_ln(hf_ref[0] + hb_ref[0], g4_ref[...], LN_EPS) * mlg_ref[...]
    z = _group_ln(yf_ref[0] + yb_ref[0], g6_ref[...], RW_GN_EPS) * gng_ref[...] + gnb_ref[...]
    rw = (z + bonus_ref[0]) * g_ref[0]
    dot = functools.partial(jnp.dot, preferred_element_type=F32)
    y = (dot(ml.astype(BF16), wml_ref[...]) + dot(rw.astype(BF16), wrw_ref[...])
         + dot(at_ref[0].astype(BF16), wat_ref[...]))
    x = _ln_rows(DEEPNORM_ALPHA * x_ref[0] + mod[2:3] * y) * lng_ref[...] + lnb_ref[...]
    xo_ref[0] = x
    ho_ref[0] = (_ln_rows(x) * (1.0 + mod[4:5]) + mod[3:4]).astype(BF16)


def _out_proj(x_all, mods, pml, hf, hb, yf, yb, bonus, g, y_at, norm_g, gn_g, gn_b, w_out,
              ln_g, ln_b, n_ctx_tiles, first_tile):
    b, s, d = x_all.shape
    tm = ROW_TILE
    n_tiles = s // tm - first_tile
    seg = lambda i: (i + first_tile >= n_ctx_tiles).astype(jnp.int32)
    rows = lambda width, blk=0: pl.BlockSpec((1, tm, width), lambda bi, i: (bi, i + first_tile, blk))
    small = lambda a: pl.BlockSpec(a.shape, lambda bi, i: (0, 0))
    wb = w_out.astype(BF16)
    consts = [_group_matrix(ML_W, ML_DH, 1.0 / ML_DH), _group_matrix(RW_W, RW_DH, 1.0 / RW_DH),
              norm_g.reshape(1, -1), gn_g.reshape(1, -1), gn_b.reshape(1, -1),
              wb[:ML_W], wb[ML_W:ML_W + RW_W], wb[ML_W + RW_W:],
              ln_g.reshape(1, -1), ln_b.reshape(1, -1)]
    out_rows = lambda: pl.BlockSpec((1, tm, d), lambda bi, i: (bi, i, 0))
    return pl.pallas_call(
        _out_proj_kernel,
        grid=(b, n_tiles),
        in_specs=[rows(d), pl.BlockSpec((1, 1, 6, d), lambda bi, i: (bi, seg(i), 0, 0)),
                  rows(ML_W), rows(ML_W), rows(ML_W, 3 * ML_W // ML_W),
                  rows(RW_W), rows(RW_W), rows(RW_W), rows(RW_W), rows(MLA_W)]
                 + [small(a) for a in consts],
        out_specs=[out_rows(), out_rows()],
        out_shape=[jax.ShapeDtypeStruct((b, n_tiles * tm, d), F32),
                   jax.ShapeDtypeStruct((b, n_tiles * tm, d), BF16)],
        compiler_params=_cparams(2),
        name="out_proj",
    )(x_all, mods, hf, hb, pml, yf, yb, bonus, g, y_at, *consts)


def _final_norm_kernel(x_ref, y_ref, mod_ref, g_ref, b_ref, o_ref):
    mod = mod_ref[0, 0]
    o_ref[0] = _ln_rows(DEEPNORM_ALPHA * x_ref[0] + mod[5:6] * y_ref[0]) * g_ref[...] + b_ref[...]


def _final_norm(x, y, mods, ln_g, ln_b, n_ctx_tiles, first_tile):
    b, s, d = x.shape
    tm = ROW_TILE
    seg = lambda i: (i + first_tile >= n_ctx_tiles).astype(jnp.int32)
    rows = pl.BlockSpec((1, tm, d), lambda bi, i: (bi, i, 0))
    small = pl.BlockSpec((1, d), lambda bi, i: (0, 0))
    return pl.pallas_call(
        _final_norm_kernel,
        grid=(b, s // tm),
        in_specs=[rows, rows, pl.BlockSpec((1, 1, 6, d), lambda bi, i: (bi, seg(i), 0, 0)), small, small],
        out_specs=rows,
        out_shape=jax.ShapeDtypeStruct((b, s, d), F32),
        compiler_params=_cparams(2),
        name="final_norm",
    )(x, y, mods, ln_g.reshape(1, d), ln_b.reshape(1, d))


def _peer_scores_kernel(ht_ref, wq_ref, keys_ref, s_ref):
    q = jnp.dot(wq_ref[...], ht_ref[...], preferred_element_type=F32).astype(BF16)
    for g in range(2 * PEER_HEADS):
        s_ref[g] = jnp.dot(keys_ref[g], q[g * PEER_DK:(g + 1) * PEER_DK], preferred_element_type=F32)


def _peer_scores(ht, wq_t, keys, tn):
    d, n = ht.shape
    groups = 2 * PEER_HEADS
    return pl.pallas_call(
        _peer_scores_kernel,
        grid=(n // tn,),
        in_specs=[pl.BlockSpec((d, tn), lambda t: (0, t)),
                  pl.BlockSpec(wq_t.shape, lambda t: (0, 0)),
                  pl.BlockSpec(keys.shape, lambda t: (0, 0, 0))],
        out_specs=pl.BlockSpec((groups, PEER_NKEYS, tn), lambda t: (0, 0, t)),
        out_shape=jax.ShapeDtypeStruct((groups, PEER_NKEYS, n), F32),
        compiler_params=_cparams(1),
        name="peer_scores",
    )(ht, wq_t, keys)


PEER_RANKS = PEER_TOPK + 1


PEER_UNRANKED = float(PEER_NKEYS - 1)


def _top_rows(s, count):
    rows = []
    rank = jnp.full(s.shape, PEER_UNRANKED, F32)
    for r in range(count):
        m = jnp.max(s, axis=0, keepdims=True)
        rows.append(m)
        hit = s == m
        rank = jnp.where(hit, float(r), rank)
        s = jnp.where(hit, -jnp.inf, s)
    return rows, rank


def _peer_route_kernel(s_ref, rank2_ref, e2_ref, n1_ref, c2_ref, cand_ref):
    cand_ref[...] = jnp.full(cand_ref.shape, -jnp.inf, F32)
    for h in range(PEER_HEADS):
        s1 = s_ref[2 * h]
        s2 = s_ref[2 * h + 1]
        a, _ = _top_rows(s1, PEER_RANKS)
        b, rank2 = _top_rows(s2, PEER_RANKS)
        pairs = [(p, q) for p in range(PEER_RANKS) for q in range(PEER_RANKS // (p + 1))]
        for slot, (p, q) in enumerate(pairs):
            cand_ref[slot:slot + 1, :] = a[p] + b[q]
        top, _ = _top_rows(cand_ref[...], PEER_RANKS)
        threshold = 0.5 * (top[PEER_TOPK - 1] + top[PEER_TOPK])
        z = sum(jnp.exp(row - top[0]) for row in top[:PEER_TOPK])
        n1 = jnp.zeros_like(s1)
        for q in range(PEER_RANKS):
            n1 = n1 + jnp.where(s1 + b[q] >= threshold, 1.0, 0.0)
        rank2_ref[h] = rank2.astype(BF16)
        e2_ref[h] = jnp.exp(s2 - b[0]).astype(BF16)
        n1_ref[h] = n1
        c2_ref[h] = jnp.exp(s1 - a[0]) / z


PEER_CAND_ROWS = 56


def _peer_route(scores, tn):
    groups, nk, n = scores.shape
    spec = pl.BlockSpec((PEER_HEADS, nk, tn), lambda t: (0, 0, t))
    shape = lambda dt: jax.ShapeDtypeStruct((PEER_HEADS, nk, n), dt)
    return pl.pallas_call(
        _peer_route_kernel,
        grid=(n // tn,),
        in_specs=[pl.BlockSpec((groups, nk, tn), lambda t: (0, 0, t))],
        out_specs=[spec] * 4,
        out_shape=[shape(BF16), shape(BF16), shape(F32), shape(F32)],
        scratch_shapes=[pltpu.VMEM((PEER_CAND_ROWS, tn), F32)],
        compiler_params=_cparams(1),
        name="peer_route",
    )(scores)


def _gelu_tanh(x):
    k = -2.0 * math.sqrt(2.0 / math.pi)
    return x / (1.0 + jnp.exp(x * (k + (k * 0.044715) * (x * x))))


def _peer_experts_kernel(blocks, ht_ref, u_ref, vt_ref, rank2_ref, e2_ref, n1_ref, c2_ref, o_ref, w_ref):
    e = pl.program_id(1)

    @pl.when(e == 0)
    def _():
        o_ref[...] = jnp.zeros_like(o_ref)

    ht = ht_ref[...]
    for ii in range(blocks):
        i = e * blocks + ii
        rows = slice(ii * PEER_NKEYS, (ii + 1) * PEER_NKEYS)
        act = _gelu_tanh(jnp.dot(u_ref[rows, :], ht, preferred_element_type=F32)).astype(BF16)
        gate = jnp.zeros(act.shape, BF16)
        for h in range(PEER_HEADS):
            n1 = n1_ref[h, pl.ds(i, 1), :].astype(BF16)
            c2 = c2_ref[h, pl.ds(i, 1), :].astype(BF16)
            gate = gate + jnp.where(rank2_ref[h] < n1, e2_ref[h] * c2, jnp.zeros((), BF16))
        w_ref[rows, :] = gate * act
    o_ref[...] += jnp.dot(vt_ref[...], w_ref[...], preferred_element_type=F32)


def _peer_experts(ht, u, vt, rank2, e2, n1, c2, tn, ec):
    d, n = ht.shape
    blocks = ec // PEER_NKEYS
    route = pl.BlockSpec((PEER_HEADS, PEER_NKEYS, tn), lambda t, e: (0, 0, t))
    return pl.pallas_call(
        functools.partial(_peer_experts_kernel, blocks),
        grid=(n // tn, PEER_EXPERTS // ec),
        in_specs=[pl.BlockSpec((d, tn), lambda t, e: (0, t)),
                  pl.BlockSpec((ec, d), lambda t, e: (e, 0)),
                  pl.BlockSpec((d, ec), lambda t, e: (0, e)),
                  route, route, route, route],
        out_specs=pl.BlockSpec((d, tn), lambda t, e: (0, t)),
        out_shape=jax.ShapeDtypeStruct((d, n), F32),
        scratch_shapes=[pltpu.VMEM((ec, tn), BF16)],
        compiler_params=_cparams(2),
        name="peer_experts",
    )(ht, u, vt, rank2, e2, n1, c2)


def _peer(h, w_q, keys, u_tab, v_tab):
    b, s, d = h.shape
    n = b * s
    tn = min(n, 512)
    ht = h.reshape(n, d).T
    wq_t = w_q.T.astype(BF16)
    kb = keys.reshape(2 * PEER_HEADS, PEER_NKEYS, PEER_DK).astype(BF16)
    scores = _peer_scores(ht, wq_t, kb, tn)
    rank2, e2, n1, c2 = _peer_route(scores, min(n, 256))
    out_t = _peer_experts(ht, u_tab.astype(BF16), v_tab.T.astype(BF16), rank2, e2, n1, c2, tn, 1024)
    return out_t.T.reshape(b, s, d)


def _rope_tables(t, n_ctx):
    rows = t // GRID_W
    row = np.repeat(np.arange(rows), GRID_W).astype(np.float32)
    col = np.tile(np.arange(GRID_W), rows).astype(np.float32)
    inv_freq = jnp.asarray(ROPE_THETA, F32) ** (-jnp.arange(ROPE_AXIS_FREQS, dtype=F32) / ROPE_AXIS_FREQS)
    ang = jnp.concatenate([jnp.asarray(row)[:, None] * inv_freq] * 2
                          + [jnp.asarray(col)[:, None] * inv_freq] * 2, axis=1)
    pad = lambda a, fill: jnp.concatenate([jnp.full((n_ctx, MLA_ROPE), fill, F32), a], axis=0)
    return pad(jnp.cos(ang), 1.0), pad(jnp.sin(ang), 0.0)


def _split_w_in(w_in):
    ml_cols = 4 * ML_W + ML_GATE_COLS
    w_ml = jnp.pad(w_in[:, :ml_cols], ((0, 0), (0, ML_PAD - ml_cols)))
    w_rw = w_in[:, ml_cols:ml_cols + RW_COLS]
    w_at = w_in[:, ml_cols + RW_COLS:]
    k_rope = w_at[:, MLA_Q_RANK + MLA_KV_RANK:]
    w_at = jnp.concatenate([w_at, _rot_cols(k_rope)], axis=1)
    w_at = jnp.pad(w_at, ((0, 0), (0, MLA_PAD - w_at.shape[1])))
    return w_ml.astype(BF16), w_rw.astype(BF16), w_at.astype(BF16)


def kernel(x, c, ctx, c_ctx, w_mod, b_mod, w_in, ml_conv_w, ml_conv_b, ml_i_bias, ml_f_bias, ml_norm_g,
           rw_mu, rw_w0, rw_w_up, rw_a0, rw_a_up, rw_g_up, rw_k_k, rw_k_a, rw_r_k, rw_gn_g, rw_gn_b,
           mla_q_norm_g, mla_q_up, mla_kv_norm_g, mla_kv_up, w_out, ln_mix_g, ln_mix_b,
           peer_w_q, peer_keys, peer_u, peer_v, ln_ffn_g, ln_ffn_b):
    b, t, d = x.shape
    n_ctx = ctx.shape[1]
    assert n_ctx % ROW_TILE == 0 and t % ROW_TILE == 0 and d == D_MODEL
    n_ctx_tiles = n_ctx // ROW_TILE
    cos, sin = _rope_tables(t, n_ctx)
    mod_rows = -(-(b + 1) // SUBLANES) * SUBLANES
    cvec = jnp.concatenate([c, c_ctx[None], jnp.zeros((mod_rows - b - 1, d), F32)], axis=0)

    x_all = jnp.concatenate([ctx, x], axis=1)
    depth = w_mod.shape[0]
    for l in range(depth):
        need_ctx = l < depth - 1
        mod = _modulation(cvec, w_mod[l], b_mod[l]).reshape(mod_rows, 6, d)
        mods = jnp.stack([jnp.broadcast_to(mod[b], (b, 6, d)), mod[:b]], axis=1)

        pml, prw, pat = _in_proj(x_all, mods, *_split_w_in(w_in[l]), n_ctx_tiles)

        q, k, lg = _ml_prep(pml, ml_conv_w[l], ml_conv_b[l], ml_i_bias[l], ml_f_bias[l], n_ctx_tiles)
        hf, hb = _ml_scan(q, k, pml[:, :, 2 * ML_W:3 * ML_W], lg, n_ctx)

        r, kh, v, g, bonus, w, ab, kt, vhi, vmid, vlo = _rw_prep(
            prw, rw_mu[l], rw_w0[l], rw_w_up[l], rw_a0[l], rw_a_up[l], rw_g_up[l], rw_k_k[l], rw_k_a[l],
            rw_r_k[l], n_ctx_tiles)
        yf, yb = _rw_scan(r, kh, w, ab, kt, vhi, vmid, vlo, n_ctx)

        qn, qr, kn, va, kr = _mla_prep(pat, mla_q_norm_g[l], mla_q_up[l], mla_kv_norm_g[l], mla_kv_up[l],
                                       cos, sin)
        at_ctx, at_lat = _mla(qn, qr, kn, va, kr, n_ctx, need_ctx)
        first_tile = 0 if need_ctx else n_ctx_tiles
        y_at = jnp.concatenate([at_ctx, at_lat], axis=1) if need_ctx else jnp.pad(
            at_lat, ((0, 0), (n_ctx, 0), (0, 0)))

        x_mid, h_ffn = _out_proj(x_all, mods, pml, hf, hb, yf, yb, bonus, g, y_at, ml_norm_g[l], rw_gn_g[l],
                                 rw_gn_b[l], w_out[l], ln_mix_g[l], ln_mix_b[l], n_ctx_tiles, first_tile)
        y = _peer(h_ffn, peer_w_q[l], peer_keys[l], peer_u[l], peer_v[l])
        x_all = _final_norm(x_mid, y, mods, ln_ffn_g[l], ln_ffn_b[l], n_ctx_tiles, first_tile)
    return x_all
```

```python
import functools
import math

import jax
import jax.numpy as jnp
import numpy as np
from jax import lax
from jax.experimental import pallas as pl
from jax.experimental.pallas import tpu as pltpu

F32 = jnp.float32
BF16 = jnp.bfloat16
HIGHEST = lax.Precision.HIGHEST

LANES = 128
SUBLANES = 8

D_MODEL = 1024
DEPTH = 2
GRID_W = 64

ML_HEADS = 4
ML_DH = 64
ML_W = ML_HEADS * ML_DH
ML_CHUNK = 64
ML_M_INIT = -1e30
ML_GATE_COLS = 4 * ML_HEADS
ML_PAD = 4 * ML_W + LANES

RW_HEADS = 6
RW_DH = 64
RW_W = RW_HEADS * RW_DH
RW_PAIRS = RW_HEADS // 2
RW_LORA = 64
RW_G_LORA = 128
RW_DECAY_SCALE = math.exp(-0.5)
RW_GN_EPS = 64e-5
RW_COLS = 3 * RW_W + 4 * RW_LORA + RW_G_LORA
RW_CHUNK = 64

MLA_HEADS = 6
MLA_NOPE = 64
MLA_ROPE = 32
MLA_V = 64
MLA_Q_RANK = 384
MLA_KV_RANK = 256
MLA_W = MLA_HEADS * MLA_V
MLA_SCALE = (MLA_NOPE + MLA_ROPE) ** -0.5
MLA_PAD = 768
ROPE_AXIS_FREQS = MLA_ROPE // 4
ROPE_THETA = 10000.0

PEER_HEADS = 8
PEER_NKEYS = 128
PEER_EXPERTS = PEER_NKEYS * PEER_NKEYS
PEER_DK = 128
PEER_TOPK = 16

DEEPNORM_ALPHA = (2 * DEPTH) ** 0.25
LN_EPS = 1e-6

ROW_TILE = 256
VMEM_LIMIT = 56 * 1024 * 1024


def _cparams(n_axes):
    return pltpu.CompilerParams(dimension_semantics=("arbitrary",) * n_axes,
                                vmem_limit_bytes=VMEM_LIMIT)


def _ln_rows(x, eps=LN_EPS):
    mu = jnp.mean(x, axis=-1, keepdims=True)
    xc = x - mu
    var = jnp.mean(xc * xc, axis=-1, keepdims=True)
    return xc * lax.rsqrt(var + eps)


def _sigmoid(x):
    return 1.0 / (1.0 + jnp.exp(-x))


def _group_ln(x, gmat, eps):
    mu = jnp.dot(x, gmat, precision=HIGHEST, preferred_element_type=F32)
    xc = x - mu
    var = jnp.dot(xc * xc, gmat, precision=HIGHEST, preferred_element_type=F32)
    return xc * lax.rsqrt(var + eps)


def _group_matrix(width, group, value):
    idx = np.arange(width) // group
    return jnp.asarray((idx[:, None] == idx[None, :]).astype(np.float32) * value)


def _mod_kernel(c_ref, w_ref, b_ref, o_ref):
    c = c_ref[...]
    s = c * _sigmoid(c)
    o_ref[...] = jnp.dot(s, w_ref[...], precision=HIGHEST, preferred_element_type=F32) + b_ref[...]


def _modulation(cvec, w_mod, b_mod):
    rows, d = cvec.shape
    n = w_mod.shape[1]
    tn = 1536
    return pl.pallas_call(
        _mod_kernel,
        grid=(n // tn,),
        in_specs=[pl.BlockSpec((rows, d), lambda j: (0, 0)),
                  pl.BlockSpec((d, tn), lambda j: (0, j)),
                  pl.BlockSpec((1, tn), lambda j: (0, j))],
        out_specs=pl.BlockSpec((rows, tn), lambda j: (0, j)),
        out_shape=jax.ShapeDtypeStruct((rows, n), F32),
        compiler_params=_cparams(1),
        name="modulation",
    )(cvec, w_mod, b_mod.reshape(1, n))


def _in_proj_kernel(x_ref, mod_ref, wml_ref, wrw_ref, wat_ref, pml_ref, prw_ref, pat_ref):
    mod = mod_ref[0, 0]
    h = _ln_rows(x_ref[0]) * (1.0 + mod[1:2]) + mod[0:1]
    hb = h.astype(BF16)
    pml_ref[0] = jnp.dot(hb, wml_ref[...], preferred_element_type=F32)
    prw_ref[0] = jnp.dot(hb, wrw_ref[...], preferred_element_type=F32)
    pat_ref[0] = jnp.dot(hb, wat_ref[...], preferred_element_type=F32)


def _in_proj(x_all, mods, wml, wrw, wat, n_ctx_tiles):
    b, s, d = x_all.shape
    tm = ROW_TILE
    seg = lambda i: (i >= n_ctx_tiles).astype(jnp.int32)
    full = lambda w: pl.BlockSpec(w.shape, lambda bi, i: (0, 0))
    out = lambda w: pl.BlockSpec((1, tm, w.shape[1]), lambda bi, i: (bi, i, 0))
    return pl.pallas_call(
        _in_proj_kernel,
        grid=(b, s // tm),
        in_specs=[pl.BlockSpec((1, tm, d), lambda bi, i: (bi, i, 0)),
                  pl.BlockSpec((1, 1, 6, d), lambda bi, i: (bi, seg(i), 0, 0)),
                  full(wml), full(wrw), full(wat)],
        out_specs=[out(wml), out(wrw), out(wat)],
        out_shape=[jax.ShapeDtypeStruct((b, s, w.shape[1]), F32) for w in (wml, wrw, wat)],
        compiler_params=_cparams(2),
        name="in_proj",
    )(x_all, mods, wml, wrw, wat)


def _halo_specs(width, col_block, tm, s):
    per = tm // SUBLANES
    last = s // SUBLANES - 1
    prev = pl.BlockSpec((1, SUBLANES, width),
                        lambda bi, i: (bi, jnp.maximum(i * per - 1, 0), col_block))
    nxt = pl.BlockSpec((1, SUBLANES, width),
                       lambda bi, i: (bi, jnp.minimum((i + 1) * per, last), col_block))
    return prev, nxt


def _neighbours(x, prev_blk, next_blk, tile, n_ctx_tiles, n_tiles):
    tm = x.shape[0]
    row = lax.broadcasted_iota(jnp.int32, x.shape, 0)
    starts = jnp.logical_or(tile == 0, tile == n_ctx_tiles)
    ends = jnp.logical_or(tile == n_ctx_tiles - 1, tile == n_tiles - 1)
    prev_row = jnp.where(starts, 0.0, prev_blk[SUBLANES - 1:SUBLANES, :])
    next_row = jnp.where(ends, 0.0, next_blk[0:1, :])
    prev = jnp.where(row == 0, prev_row, pltpu.roll(x, 1, 0))
    nxt = jnp.where(row == tm - 1, next_row, pltpu.roll(x, tm - 1, 0))
    return prev, nxt


def _ml_prep_kernel(n_ctx_tiles, n_tiles, qk_ref, prev_ref, next_ref, gate_ref, cw_ref, cb_ref,
                    gb_ref, fmask_ref, q_ref, k_ref, lg_ref):
    tile = pl.program_id(1)
    x = qk_ref[0]
    prev, nxt = _neighbours(x, prev_ref[0], next_ref[0], tile, n_ctx_tiles, n_tiles)
    cw = cw_ref[...]
    z = prev * cw[0:1] + x * cw[1:2] + nxt * cw[2:3] + cb_ref[...]
    z = z * _sigmoid(z)
    q_ref[0] = z[:, :ML_W] * (ML_DH ** -0.5)
    k_ref[0] = z[:, ML_W:]
    g = gate_ref[0] + gb_ref[...]
    log_sig = jnp.minimum(g, 0.0) - jnp.log1p(jnp.exp(-jnp.abs(g)))
    lg_ref[0] = jnp.where(fmask_ref[...] > 0.5, log_sig, g)


def _ml_prep(pml, conv_w, conv_b, i_bias, f_bias, n_ctx_tiles):
    b, s, _ = pml.shape
    tm = ROW_TILE
    n_tiles = s // tm
    gate_bias = jnp.stack([i_bias, f_bias], axis=1).reshape(1, ML_GATE_COLS)
    gate_bias = jnp.pad(gate_bias, ((0, 0), (0, LANES - ML_GATE_COLS)))
    fmask = np.zeros((2, 2, ML_HEADS), np.float32)
    fmask[:, 1] = 1.0
    fmask = jnp.asarray(np.pad(fmask.reshape(1, -1), ((0, 0), (0, LANES - ML_GATE_COLS))))
    prev, nxt = _halo_specs(2 * ML_W, 0, tm, s)
    small = lambda a: pl.BlockSpec(a.shape, lambda bi, i: (0, 0))
    cb = conv_b.reshape(1, -1)
    return pl.pallas_call(
        functools.partial(_ml_prep_kernel, n_ctx_tiles, n_tiles),
        grid=(b, n_tiles),
        in_specs=[pl.BlockSpec((1, tm, 2 * ML_W), lambda bi, i: (bi, i, 0)), prev, nxt,
                  pl.BlockSpec((1, tm, LANES), lambda bi, i: (bi, i, 4 * ML_W // LANES)),
                  small(conv_w), small(cb), small(gate_bias), small(fmask)],
        out_specs=[pl.BlockSpec((1, tm, ML_W), lambda bi, i: (bi, i, 0)),
                   pl.BlockSpec((1, tm, ML_W), lambda bi, i: (bi, i, 0)),
                   pl.BlockSpec((1, tm, LANES), lambda bi, i: (bi, i, 0))],
        out_shape=[jax.ShapeDtypeStruct((b, s, ML_W), F32),
                   jax.ShapeDtypeStruct((b, s, ML_W), F32),
                   jax.ShapeDtypeStruct((b, s, LANES), F32)],
        compiler_params=_cparams(2),
        name="mlstm_prep",
    )(pml, pml, pml, pml, conv_w, cb, gate_bias, fmask)


def _ml_scan_kernel(n_chunks, n_ctx_chunks, q_ref, k_ref, v_ref, lg_ref, hf_ref, hb_ref,
                    ct_ref, n_ref, m_ref):
    L = ML_CHUNK
    units = 2 * ML_HEADS
    per_tile = LANES // ML_DH
    row = lax.broadcasted_iota(jnp.int32, (units, L, L), 1)
    col = lax.broadcasted_iota(jnp.int32, (units, L, L), 2)
    sign = jnp.where(lax.broadcasted_iota(jnp.int32, (units, L, L), 0) < ML_HEADS, 1, -1)
    eye = row == col
    seen = (col - row) * sign <= 0
    seen_t = (row - col) * sign <= 0
    h_refs = (hf_ref, hb_ref)

    def bmm(a, b):
        return jnp.einsum("uij,ujk->uik", a.astype(BF16), b.astype(BF16), preferred_element_type=F32)

    def chunk(c, _):
        back = jnp.where(c < n_ctx_chunks, n_ctx_chunks - 1 - c, n_ctx_chunks + n_chunks - 1 - c)
        rows = [pl.ds(pl.multiple_of(cc * L, L), L) for cc in (c, back)]
        q, k, kt, v, i_col, f_col = [], [], [], [], [], []
        for d in range(2):
            lg = lg_ref[0, rows[d], :]
            for tile in range(ML_W // LANES):
                lanes = slice(tile * LANES, (tile + 1) * LANES)
                q2, k2, v2 = q_ref[0, rows[d], lanes], k_ref[0, rows[d], lanes], v_ref[0, rows[d], lanes]
                kt2 = k2.T
                for hh in range(per_tile):
                    head = tile * per_tile + hh
                    sub = slice(hh * ML_DH, (hh + 1) * ML_DH)
                    gate = d * 2 * ML_HEADS + head
                    q.append(q2[:, sub]); k.append(k2[:, sub]); v.append(v2[:, sub]); kt.append(kt2[sub, :])
                    i_col.append(lg[:, gate:gate + 1])
                    f_col.append(lg[:, gate + ML_HEADS:gate + ML_HEADS + 1])
        q, k, kt, v, i_col, f_col = (jnp.stack(a, axis=0) for a in (q, k, kt, v, i_col, f_col))
        ct, n, m = ct_ref[...], n_ref[...], m_ref[...]

        bcum_row = jnp.sum(jnp.where(seen_t, f_col, 0.0), axis=1, keepdims=True)
        bcum_col = jnp.sum(jnp.where(eye, bcum_row, 0.0), axis=2, keepdims=True)
        i_row = jnp.sum(jnp.where(eye, i_col, 0.0), axis=1, keepdims=True)
        dmat = jnp.where(seen, bcum_col - bcum_row + i_row, -jnp.inf)
        rmax = jnp.max(dmat, axis=2, keepdims=True)
        sx = bmm(q, kt) * jnp.exp(dmat - rmax)
        sv = bmm(sx, v)
        rs = jnp.sum(sx, axis=2, keepdims=True)
        b_end = jnp.concatenate([bcum_col[:ML_HEADS, L - 1:L, :], bcum_col[ML_HEADS:, 0:1, :]], axis=0)
        g = b_end - bcum_col + i_col
        gmax = jnp.max(g, axis=1, keepdims=True)
        wkx = jnp.exp(g - gmax)
        kv = bmm(kt, wkx * v)
        nx = jnp.sum(wkx * k, axis=1, keepdims=True)
        a_inter = bcum_col + m
        m_t = jnp.maximum(a_inter, rmax)
        w_inter = jnp.exp(a_inter - m_t)
        scale = jnp.exp(rmax - m_t)
        num = w_inter * bmm(q, ct) + scale * sv
        den = w_inter * jnp.sum(q * n, axis=2, keepdims=True) + scale * rs
        h = num / jnp.maximum(jnp.abs(den), jnp.exp(-m_t))
        m_new = jnp.maximum(b_end + m, gmax)
        decay = jnp.exp(b_end + m - m_new)
        carry_scale = jnp.exp(gmax - m_new)
        ct_ref[...] = decay * ct + carry_scale * kv
        n_ref[...] = decay * n + carry_scale * nx
        m_ref[...] = m_new
        for d in range(2):
            for tile in range(ML_W // LANES):
                first = d * ML_HEADS + tile * per_tile
                h_refs[d][0, rows[d], tile * LANES:(tile + 1) * LANES] = jnp.concatenate(
                    [h[first + hh] for hh in range(per_tile)], axis=1)
        return 0

    ct_ref[...] = jnp.zeros_like(ct_ref)
    n_ref[...] = jnp.zeros_like(n_ref)
    m_ref[...] = jnp.full(m_ref.shape, ML_M_INIT, F32)
    lax.fori_loop(0, n_chunks, chunk, 0)


def _ml_scan(q, k, pml, lg, n_ctx):
    b, s, _ = q.shape
    nc = s // ML_CHUNK
    wide = lambda blk: pl.BlockSpec((1, s, ML_W), lambda bi: (bi, 0, blk))
    return pl.pallas_call(
        functools.partial(_ml_scan_kernel, nc, n_ctx // ML_CHUNK),
        grid=(b,),
        in_specs=[wide(0), wide(0), wide(2), pl.BlockSpec((1, s, LANES), lambda bi: (bi, 0, 0))],
        out_specs=[wide(0), wide(0)],
        out_shape=[jax.ShapeDtypeStruct((b, s, ML_W), F32)] * 2,
        scratch_shapes=[pltpu.VMEM((2 * ML_HEADS, ML_DH, ML_DH), F32),
                        pltpu.VMEM((2 * ML_HEADS, 1, ML_DH), F32),
                        pltpu.VMEM((2 * ML_HEADS, 1, 1), F32)],
        compiler_params=_cparams(1),
        name="mlstm_scan",
    )(q, k, pml, lg)


def _rw_prep_kernel(n_ctx_tiles, n_tiles, p_ref, prev_ref, next_ref, mu_ref, wup_ref, aup_ref,
                    gup_ref, w0_ref, a0_ref, kk_ref, ka_ref, rho_ref, gsum_ref,
                    r_ref, kh_ref, v_ref, g_ref, bonus_ref, w_ref, ab_ref, kt_ref):
    tile = pl.program_id(1)
    p = p_ref[0]
    prev, nxt = _neighbours(p, prev_ref[0], next_ref[0], tile, n_ctx_tiles, n_tiles)
    mu = mu_ref[...]
    p = p + mu[0:1] * (prev - p) + mu[1:2] * (nxt - p)
    r = p[:, 0:RW_W]
    k = p[:, RW_W:2 * RW_W]
    v = p[:, 2 * RW_W:3 * RW_W]
    base = 3 * RW_W
    wd = p[:, base:base + 2 * RW_LORA]
    ad = p[:, base + 2 * RW_LORA:base + 4 * RW_LORA]
    gd = p[:, base + 4 * RW_LORA:base + 4 * RW_LORA + RW_G_LORA]
    dot = functools.partial(jnp.dot, preferred_element_type=F32)
    w = jnp.exp(-RW_DECAY_SCALE * _sigmoid(w0_ref[...] + dot(jnp.tanh(wd).astype(BF16), wup_ref[...])))
    a = _sigmoid(a0_ref[...] + dot(ad.astype(BF16), aup_ref[...]))
    g = dot(_sigmoid(gd).astype(BF16), gup_ref[...])
    kk = k * kk_ref[...]
    ss = jnp.dot(kk * kk, gsum_ref[...], precision=HIGHEST, preferred_element_type=F32)
    kh = kk * lax.rsqrt(ss + 1e-12)
    ka = ka_ref[...]
    rk = r * rho_ref[...]
    kt_sum = jnp.zeros_like(k)
    for dr in range(2):
        a_d = a[:, dr * RW_W:(dr + 1) * RW_W]
        kt_d = k * (1.0 + (a_d - 1.0) * ka)
        kt_ref[0, :, dr * RW_W:(dr + 1) * RW_W] = kt_d
        ab_ref[0, :, dr * RW_W:(dr + 1) * RW_W] = kh * a_d
        kt_sum = kt_sum + kt_d
    bonus_ref[0] = jnp.dot(rk * kt_sum, gsum_ref[...], precision=HIGHEST,
                           preferred_element_type=F32) * v
    r_ref[0] = r
    kh_ref[0] = kh
    v_ref[0] = v
    g_ref[0] = g
    w_ref[0] = w


def _block_diag2(up):
    z = jnp.zeros_like(up[0])
    return jnp.concatenate([jnp.concatenate([up[0], z], axis=1),
                            jnp.concatenate([z, up[1]], axis=1)], axis=0)


def _rw_prep(prw, mu, w0, w_up, a0, a_up, g_up, k_k, k_a, r_k, n_ctx_tiles):
    b, s, _ = prw.shape
    tm = ROW_TILE
    n_tiles = s // tm
    prev, nxt = _halo_specs(RW_COLS, 0, tm, s)
    small = lambda a: pl.BlockSpec(a.shape, lambda bi, i: (0, 0))
    consts = [mu, _block_diag2(w_up).astype(BF16), _block_diag2(a_up).astype(BF16),
              g_up.astype(BF16), w0.reshape(1, 2 * RW_W), a0.reshape(1, 2 * RW_W),
              k_k.reshape(1, RW_W), k_a.reshape(1, RW_W), r_k.reshape(1, RW_W),
              _group_matrix(RW_W, RW_DH, 1.0)]
    one = lambda width, dt=F32: (pl.BlockSpec((1, tm, width), lambda bi, i: (bi, i, 0)),
                                 jax.ShapeDtypeStruct((b, s, width), dt))
    outs = [one(RW_W)] * 5 + [one(2 * RW_W)] * 3
    return pl.pallas_call(
        functools.partial(_rw_prep_kernel, n_ctx_tiles, n_tiles),
        grid=(b, n_tiles),
        in_specs=[pl.BlockSpec((1, tm, RW_COLS), lambda bi, i: (bi, i, 0)), prev, nxt]
                 + [small(a) for a in consts],
        out_specs=[o[0] for o in outs],
        out_shape=[o[1] for o in outs],
        compiler_params=_cparams(2),
        name="rwkv_prep",
    )(prw, prw, prw, *consts)


def _rw_scan_kernel(batch, *refs):
    ins = (refs[0:6], refs[6:12])
    ones_ref = refs[12]
    y_refs = refs[13:15]
    state_ref, yacc_ref, xs_ref, z_ref, vcat_ref = refs[15:20]
    L = RW_CHUNK
    items = batch * RW_PAIRS
    per_dir = items * RW_DH

    def item_rows(d, bi, p):
        start = ((d * batch + bi) * RW_PAIRS + p) * RW_DH
        return slice(start, start + RW_DH)

    @pl.when(pl.program_id(0) == 0)
    def _():
        state_ref[...] = jnp.zeros_like(state_ref)

    yacc_ref[...] = jnp.zeros_like(yacc_ref)
    for d in range(2):
        for bi in range(batch):
            for p in range(RW_PAIRS):
                vt = ins[d][5][bi, :, p * LANES:(p + 1) * LANES].T
                vcat_ref[item_rows(d, bi, p), :] = jnp.concatenate([vt[:RW_DH], vt[RW_DH:]], axis=1)
    lane = lax.broadcasted_iota(jnp.int32, (RW_DH, LANES), 1)
    step_of_lane = jnp.where(lane < RW_DH, lane, lane - RW_DH)
    sel_row = lax.broadcasted_iota(jnp.int32, (LANES, LANES), 0)
    sel_col = lax.broadcasted_iota(jnp.int32, (LANES, LANES), 1)
    same_head = (sel_row < RW_DH) == (sel_col < RW_DH)
    sel_step = jnp.where(sel_row < RW_DH, sel_row, sel_row - RW_DH)

    def step(j, _):
        steps = (j, L - 1 - j)
        rows_t = [[ref[steps[d]] for ref in ins[d][:5]] for d in range(2)]
        vec = lambda rows, bi, p: rows[bi:bi + 1, p * LANES:(p + 1) * LANES]

        for d in range(2):
            kh_t = rows_t[d][3]
            for bi in range(batch):
                for p in range(RW_PAIRS):
                    rows = item_rows(d, bi, p)
                    x = state_ref[rows, :] * vec(kh_t, bi, p)
                    hi = x.astype(BF16)
                    xs_ref[rows, 0:LANES] = hi
                    xs_ref[rows, LANES:2 * LANES] = (x - hi.astype(F32)).astype(BF16)
        sa = jnp.dot(xs_ref[...], ones_ref[...], preferred_element_type=F32)
        pick = [jnp.where(lane < RW_DH, steps[d], steps[d] + RW_DH) for d in range(2)]

        for d in range(2):
            w_t, ab_t, kt_t, _, r_t = rows_t[d]
            for bi in range(batch):
                for p in range(RW_PAIRS):
                    rows = item_rows(d, bi, p)
                    local = slice(rows.start - d * per_dir, rows.stop - d * per_dir)
                    vb = jnp.take_along_axis(vcat_ref[rows, :], pick[d], axis=1)
                    st = (state_ref[rows, :] * vec(w_t, bi, p) - sa[rows] * vec(ab_t, bi, p)
                          + vb * vec(kt_t, bi, p))
                    state_ref[rows, :] = st
                    z_ref[rows, :] = (st * vec(r_t, bi, p)).astype(BF16)
        y = jnp.dot(z_ref[...], ones_ref[0:LANES, :], preferred_element_type=F32)
        for d in range(2):
            here = step_of_lane == steps[d]
            for bi in range(batch):
                for p in range(RW_PAIRS):
                    rows = item_rows(d, bi, p)
                    yacc_ref[rows, :] = jnp.where(here, y[rows], yacc_ref[rows, :])
        return 0

    lax.fori_loop(0, L, step, 0)
    for d in range(2):
        for bi in range(batch):
            for p in range(RW_PAIRS):
                ya = yacc_ref[item_rows(d, bi, p), :]
                ya = jnp.concatenate([ya[:, :RW_DH], ya[:, RW_DH:]], axis=0)
                y_refs[d][bi, :, p * LANES:(p + 1) * LANES] = ya.T


def _rw_scan(r, kh, v, w, ab, kt, n_ctx):
    b, s, _ = r.shape
    L = RW_CHUNK
    nc = s // L
    ncc = n_ctx // L
    rows = b * RW_PAIRS * RW_DH
    cmap = (lambda c: c,
            lambda c: jnp.where(c < ncc, ncc - 1 - c, ncc + nc - 1 - c))
    time_major = lambda a: a.transpose(1, 0, 2)
    head_ones = _group_matrix(LANES, RW_DH, 1.0).astype(BF16)
    head_ones = jnp.concatenate([head_ones, head_ones], axis=0)
    in_specs, args = [], []
    for d in range(2):
        cm = cmap[d]
        per_dir = lambda c, cm=cm, d=d: (cm(c), 0, d)
        shared = lambda c, cm=cm: (cm(c), 0, 0)
        in_specs += [pl.BlockSpec((L, b, RW_W), per_dir)] * 3
        in_specs += [pl.BlockSpec((L, b, RW_W), shared)] * 2
        in_specs += [pl.BlockSpec((b, L, RW_W), lambda c, cm=cm: (0, cm(c), 0))]
        args += [time_major(a) for a in (w, ab, kt, kh, r)] + [v]
    in_specs.append(pl.BlockSpec(head_ones.shape, lambda c: (0, 0)))
    args.append(head_ones)
    out_specs = [pl.BlockSpec((b, L, RW_W), lambda c, cm=cm: (0, cm(c), 0)) for cm in cmap]
    return pl.pallas_call(
        functools.partial(_rw_scan_kernel, b),
        grid=(nc,),
        in_specs=in_specs,
        out_specs=out_specs,
        out_shape=[jax.ShapeDtypeStruct((b, s, RW_W), F32)] * 2,
        scratch_shapes=[pltpu.VMEM((2 * rows, LANES), F32), pltpu.VMEM((2 * rows, LANES), F32),
                        pltpu.VMEM((2 * rows, 2 * LANES), BF16), pltpu.VMEM((2 * rows, LANES), BF16),
                        pltpu.VMEM((2 * rows, LANES), F32)],
        compiler_params=_cparams(1),
        name="rwkv_scan",
    )(*args)


def _mla_prep_kernel(p_ref, cosq_ref, sinq_ref, cosk_ref, sink_ref, qg_ref, kvg_ref, wq_ref, wkv_ref,
                     qn_ref, qr_ref, kn_ref, v_ref, kr_ref):
    p = p_ref[0]
    cq = p[:, :MLA_Q_RANK]
    ckv = p[:, MLA_Q_RANK:MLA_Q_RANK + MLA_KV_RANK]
    base = MLA_Q_RANK + MLA_KV_RANK
    k_rope = p[:, base:base + MLA_ROPE]
    k_rot = p[:, base + MLA_ROPE:base + 2 * MLA_ROPE]
    rms = lambda x, g: x * lax.rsqrt(jnp.mean(x * x, axis=-1, keepdims=True) + LN_EPS) * g
    q = jnp.dot(rms(cq, qg_ref[...]).astype(BF16), wq_ref[...], preferred_element_type=F32)
    kv = jnp.dot(rms(ckv, kvg_ref[...]).astype(BF16), wkv_ref[...], preferred_element_type=F32)
    nope = MLA_HEADS * MLA_NOPE
    rope = MLA_HEADS * MLA_ROPE
    qn_ref[0] = q[:, :nope]
    qr_ref[0] = q[:, nope:nope + rope] * cosq_ref[...] + q[:, nope + rope:nope + 2 * rope] * sinq_ref[...]
    kn_ref[0] = kv[:, :nope]
    v_ref[0] = kv[:, nope:]
    kr_ref[0] = k_rope * cosk_ref[...] + k_rot * sink_ref[...]


def _rot_cols(w):
    shape = w.shape
    w = w.reshape(shape[:-1] + (shape[-1] // 16, 2, 8))
    return jnp.concatenate([-w[..., 1:2, :], w[..., 0:1, :]], axis=-2).reshape(shape)


def _mla_prep(pat, q_norm_g, q_up, kv_norm_g, kv_up, cos, sin):
    b, s, _ = pat.shape
    tm = ROW_TILE
    qw = q_up.reshape(MLA_Q_RANK, MLA_HEADS, MLA_NOPE + MLA_ROPE)
    q_nope = qw[:, :, :MLA_NOPE].reshape(MLA_Q_RANK, -1)
    q_rope = qw[:, :, MLA_NOPE:]
    wq = jnp.concatenate([q_nope, q_rope.reshape(MLA_Q_RANK, -1),
                          _rot_cols(q_rope).reshape(MLA_Q_RANK, -1)], axis=1).astype(BF16)
    kvw = kv_up.reshape(MLA_KV_RANK, MLA_HEADS, MLA_NOPE + MLA_V)
    wkv = jnp.concatenate([kvw[:, :, :MLA_NOPE].reshape(MLA_KV_RANK, -1),
                           kvw[:, :, MLA_NOPE:].reshape(MLA_KV_RANK, -1)], axis=1).astype(BF16)
    cosq = jnp.tile(cos, (1, MLA_HEADS))
    sinq = jnp.tile(sin, (1, MLA_HEADS))
    rows = lambda width: pl.BlockSpec((tm, width), lambda bi, i: (i, 0))
    small = lambda a: pl.BlockSpec(a.shape, lambda bi, i: (0, 0))
    qg = q_norm_g.reshape(1, -1)
    kvg = kv_norm_g.reshape(1, -1)
    out = lambda width: (pl.BlockSpec((1, tm, width), lambda bi, i: (bi, i, 0)),
                         jax.ShapeDtypeStruct((b, s, width), F32))
    outs = [out(MLA_HEADS * MLA_NOPE), out(MLA_HEADS * MLA_ROPE), out(MLA_HEADS * MLA_NOPE),
            out(MLA_W), out(MLA_ROPE)]
    return pl.pallas_call(
        _mla_prep_kernel,
        grid=(b, s // tm),
        in_specs=[pl.BlockSpec((1, tm, MLA_PAD), lambda bi, i: (bi, i, 0)),
                  rows(MLA_HEADS * MLA_ROPE), rows(MLA_HEADS * MLA_ROPE), rows(MLA_ROPE), rows(MLA_ROPE),
                  small(qg), small(kvg), small(wq), small(wkv)],
        out_specs=[o[0] for o in outs],
        out_shape=[o[1] for o in outs],
        compiler_params=_cparams(2),
        name="mla_prep",
    )(pat, cosq, sinq, cos, sin, qg, kvg, wq, wkv)


def _attn_kernel(q_ref, k_ref, v_ref, o_ref):
    s = lax.dot_general(q_ref[0, 0], k_ref[0, 0], (((1,), (1,)), ((), ())),
                        preferred_element_type=F32) * MLA_SCALE
    p = jnp.exp(s - jnp.max(s, axis=-1, keepdims=True))
    denom = jnp.sum(p, axis=-1, keepdims=True)
    o_ref[0, 0] = jnp.dot(p.astype(BF16), v_ref[0, 0], preferred_element_type=F32) / denom


def _attention(q, k, v, tq):
    b, h, t, dq = q.shape
    sk = k.shape[2]
    dv = v.shape[3]
    return pl.pallas_call(
        _attn_kernel,
        grid=(b, h, t // tq),
        in_specs=[pl.BlockSpec((1, 1, tq, dq), lambda bi, hi, i: (bi, hi, i, 0)),
                  pl.BlockSpec((1, 1, sk, dq), lambda bi, hi, i: (bi, hi, 0, 0)),
                  pl.BlockSpec((1, 1, sk, dv), lambda bi, hi, i: (bi, hi, 0, 0))],
        out_specs=pl.BlockSpec((1, 1, tq, dv), lambda bi, hi, i: (bi, hi, i, 0)),
        out_shape=jax.ShapeDtypeStruct((b, h, t, dv), F32),
        compiler_params=_cparams(3),
        name="mla_attention",
    )(q, k, v)


def _mla(qn, qr, kn, v, kr, n_ctx, need_ctx):
    b, s, _ = qn.shape
    heads = lambda a, w: a.reshape(b, s, MLA_HEADS, w).transpose(0, 2, 1, 3)
    q = jnp.concatenate([heads(qn, MLA_NOPE), heads(qr, MLA_ROPE)], axis=-1).astype(BF16)
    k_rope = jnp.broadcast_to(kr[:, None], (b, MLA_HEADS, s, MLA_ROPE))
    k = jnp.concatenate([heads(kn, MLA_NOPE), k_rope], axis=-1).astype(BF16)
    vh = heads(v, MLA_V).astype(BF16)
    pad = ((0, 0), (0, 0), (0, 0), (0, LANES - MLA_NOPE - MLA_ROPE))
    q, k = jnp.pad(q, pad), jnp.pad(k, pad)
    merge = lambda a: a.transpose(0, 2, 1, 3).reshape(b, -1, MLA_W)
    t = s - n_ctx
    y_lat = merge(_attention(q[:, :, n_ctx:], k, vh, min(t, 512)))
    if not need_ctx:
        return None, y_lat
    y_ctx = merge(_attention(q[:, :, :n_ctx], k[:, :, :n_ctx], vh[:, :, :n_ctx], n_ctx))
    return y_ctx, y_lat


def _out_proj_kernel(x_ref, mod_ref, hf_ref, hb_ref, o_ref, yf_ref, yb_ref, bonus_ref, g_ref, at_ref,
                     g4_ref, g6_ref, mlg_ref, gng_ref, gnb_ref, wml_ref, wrw_ref, wat_ref,
                     lng_ref, lnb_ref, xo_ref, ho_ref):
    mod = mod_ref[0, 0]
    ml = _sigmoid(o_ref[0]) * _group_ln(hf_ref[0] + hb_ref[0], g4_ref[...], LN_EPS) * mlg_ref[...]
    z = _group_ln(yf_ref[0] + yb_ref[0], g6_ref[...], RW_GN_EPS) * gng_ref[...] + gnb_ref[...]
    rw = (z + bonus_ref[0]) * g_ref[0]
    dot = functools.partial(jnp.dot, preferred_element_type=F32)
    y = (dot(ml.astype(BF16), wml_ref[...]) + dot(rw.astype(BF16), wrw_ref[...])
         + dot(at_ref[0].astype(BF16), wat_ref[...]))
    x = _ln_rows(DEEPNORM_ALPHA * x_ref[0] + mod[2:3] * y) * lng_ref[...] + lnb_ref[...]
    xo_ref[0] = x
    ho_ref[0] = (_ln_rows(x) * (1.0 + mod[4:5]) + mod[3:4]).astype(BF16)


def _out_proj(x_all, mods, pml, hf, hb, yf, yb, bonus, g, y_at, norm_g, gn_g, gn_b, w_out,
              ln_g, ln_b, n_ctx_tiles, first_tile):
    b, s, d = x_all.shape
    tm = ROW_TILE
    n_tiles = s // tm - first_tile
    seg = lambda i: (i + first_tile >= n_ctx_tiles).astype(jnp.int32)
    rows = lambda width, blk=0: pl.BlockSpec((1, tm, width), lambda bi, i: (bi, i + first_tile, blk))
    small = lambda a: pl.BlockSpec(a.shape, lambda bi, i: (0, 0))
    wb = w_out.astype(BF16)
    consts = [_group_matrix(ML_W, ML_DH, 1.0 / ML_DH), _group_matrix(RW_W, RW_DH, 1.0 / RW_DH),
              norm_g.reshape(1, -1), gn_g.reshape(1, -1), gn_b.reshape(1, -1),
              wb[:ML_W], wb[ML_W:ML_W + RW_W], wb[ML_W + RW_W:],
              ln_g.reshape(1, -1), ln_b.reshape(1, -1)]
    out_rows = lambda: pl.BlockSpec((1, tm, d), lambda bi, i: (bi, i, 0))
    return pl.pallas_call(
        _out_proj_kernel,
        grid=(b, n_tiles),
        in_specs=[rows(d), pl.BlockSpec((1, 1, 6, d), lambda bi, i: (bi, seg(i), 0, 0)),
                  rows(ML_W), rows(ML_W), rows(ML_W, 3 * ML_W // ML_W),
                  rows(RW_W), rows(RW_W), rows(RW_W), rows(RW_W), rows(MLA_W)]
                 + [small(a) for a in consts],
        out_specs=[out_rows(), out_rows()],
        out_shape=[jax.ShapeDtypeStruct((b, n_tiles * tm, d), F32),
                   jax.ShapeDtypeStruct((b, n_tiles * tm, d), BF16)],
        compiler_params=_cparams(2),
        name="out_proj",
    )(x_all, mods, hf, hb, pml, yf, yb, bonus, g, y_at, *consts)


def _final_norm_kernel(x_ref, y_ref, mod_ref, g_ref, b_ref, o_ref):
    mod = mod_ref[0, 0]
    o_ref[0] = _ln_rows(DEEPNORM_ALPHA * x_ref[0] + mod[5:6] * y_ref[0]) * g_ref[...] + b_ref[...]


def _final_norm(x, y, mods, ln_g, ln_b, n_ctx_tiles, first_tile):
    b, s, d = x.shape
    tm = ROW_TILE
    seg = lambda i: (i + first_tile >= n_ctx_tiles).astype(jnp.int32)
    rows = pl.BlockSpec((1, tm, d), lambda bi, i: (bi, i, 0))
    small = pl.BlockSpec((1, d), lambda bi, i: (0, 0))
    return pl.pallas_call(
        _final_norm_kernel,
        grid=(b, s // tm),
        in_specs=[rows, rows, pl.BlockSpec((1, 1, 6, d), lambda bi, i: (bi, seg(i), 0, 0)), small, small],
        out_specs=rows,
        out_shape=jax.ShapeDtypeStruct((b, s, d), F32),
        compiler_params=_cparams(2),
        name="final_norm",
    )(x, y, mods, ln_g.reshape(1, d), ln_b.reshape(1, d))


def _peer_scores_kernel(ht_ref, wq_ref, keys_ref, s_ref):
    q = jnp.dot(wq_ref[...], ht_ref[...], preferred_element_type=F32).astype(BF16)
    for g in range(2 * PEER_HEADS):
        s_ref[g] = jnp.dot(keys_ref[g], q[g * PEER_DK:(g + 1) * PEER_DK], preferred_element_type=F32)


def _peer_scores(ht, wq_t, keys, tn):
    d, n = ht.shape
    groups = 2 * PEER_HEADS
    return pl.pallas_call(
        _peer_scores_kernel,
        grid=(n // tn,),
        in_specs=[pl.BlockSpec((d, tn), lambda t: (0, t)),
                  pl.BlockSpec(wq_t.shape, lambda t: (0, 0)),
                  pl.BlockSpec(keys.shape, lambda t: (0, 0, 0))],
        out_specs=pl.BlockSpec((groups, PEER_NKEYS, tn), lambda t: (0, 0, t)),
        out_shape=jax.ShapeDtypeStruct((groups, PEER_NKEYS, n), F32),
        compiler_params=_cparams(1),
        name="peer_scores",
    )(ht, wq_t, keys)


PEER_RANKS = PEER_TOPK + 1


PEER_UNRANKED = float(PEER_NKEYS - 1)


def _top_rows(s, count):
    rows = []
    rank = jnp.full(s.shape, PEER_UNRANKED, F32)
    for r in range(count):
        m = jnp.max(s, axis=0, keepdims=True)
        rows.append(m)
        hit = s == m
        rank = jnp.where(hit, float(r), rank)
        s = jnp.where(hit, -jnp.inf, s)
    return rows, rank


def _peer_route_kernel(s_ref, rank2_ref, e2_ref, n1_ref, c2_ref, cand_ref):
    cand_ref[...] = jnp.full(cand_ref.shape, -jnp.inf, F32)
    for h in range(PEER_HEADS):
        s1 = s_ref[2 * h]
        s2 = s_ref[2 * h + 1]
        a, _ = _top_rows(s1, PEER_RANKS)
        b, rank2 = _top_rows(s2, PEER_RANKS)
        pairs = [(p, q) for p in range(PEER_RANKS) for q in range(PEER_RANKS // (p + 1))]
        for slot, (p, q) in enumerate(pairs):
            cand_ref[slot:slot + 1, :] = a[p] + b[q]
        top, _ = _top_rows(cand_ref[...], PEER_RANKS)
        threshold = 0.5 * (top[PEER_TOPK - 1] + top[PEER_TOPK])
        z = sum(jnp.exp(row - top[0]) for row in top[:PEER_TOPK])
        n1 = jnp.zeros_like(s1)
        for q in range(PEER_RANKS):
            n1 = n1 + jnp.where(s1 + b[q] >= threshold, 1.0, 0.0)
        rank2_ref[h] = rank2.astype(BF16)
        e2_ref[h] = jnp.exp(s2 - b[0]).astype(BF16)
        n1_ref[h] = n1
        c2_ref[h] = jnp.exp(s1 - a[0]) / z


PEER_CAND_ROWS = 56


def _peer_route(scores, tn):
    groups, nk, n = scores.shape
    spec = pl.BlockSpec((PEER_HEADS, nk, tn), lambda t: (0, 0, t))
    shape = lambda dt: jax.ShapeDtypeStruct((PEER_HEADS, nk, n), dt)
    return pl.pallas_call(
        _peer_route_kernel,
        grid=(n // tn,),
        in_specs=[pl.BlockSpec((groups, nk, tn), lambda t: (0, 0, t))],
        out_specs=[spec] * 4,
        out_shape=[shape(BF16), shape(BF16), shape(F32), shape(F32)],
        scratch_shapes=[pltpu.VMEM((PEER_CAND_ROWS, tn), F32)],
        compiler_params=_cparams(1),
        name="peer_route",
    )(scores)


def _gelu_tanh(x):
    k = -2.0 * math.sqrt(2.0 / math.pi)
    return x / (1.0 + jnp.exp(x * (k + (k * 0.044715) * (x * x))))


def _peer_experts_kernel(blocks, ht_ref, u_ref, vt_ref, rank2_ref, e2_ref, n1_ref, c2_ref, o_ref, w_ref):
    e = pl.program_id(1)

    @pl.when(e == 0)
    def _():
        o_ref[...] = jnp.zeros_like(o_ref)

    ht = ht_ref[...]
    for ii in range(blocks):
        i = e * blocks + ii
        rows = slice(ii * PEER_NKEYS, (ii + 1) * PEER_NKEYS)
        act = _gelu_tanh(jnp.dot(u_ref[rows, :], ht, preferred_element_type=F32)).astype(BF16)
        gate = jnp.zeros(act.shape, BF16)
        for h in range(PEER_HEADS):
            n1 = n1_ref[h, pl.ds(i, 1), :].astype(BF16)
            c2 = c2_ref[h, pl.ds(i, 1), :].astype(BF16)
            gate = gate + jnp.where(rank2_ref[h] < n1, e2_ref[h] * c2, jnp.zeros((), BF16))
        w_ref[rows, :] = gate * act
    o_ref[...] += jnp.dot(vt_ref[...], w_ref[...], preferred_element_type=F32)


def _peer_experts(ht, u, vt, rank2, e2, n1, c2, tn, ec):
    d, n = ht.shape
    blocks = ec // PEER_NKEYS
    route = pl.BlockSpec((PEER_HEADS, PEER_NKEYS, tn), lambda t, e: (0, 0, t))
    return pl.pallas_call(
        functools.partial(_peer_experts_kernel, blocks),
        grid=(n // tn, PEER_EXPERTS // ec),
        in_specs=[pl.BlockSpec((d, tn), lambda t, e: (0, t)),
                  pl.BlockSpec((ec, d), lambda t, e: (e, 0)),
                  pl.BlockSpec((d, ec), lambda t, e: (0, e)),
                  route, route, route, route],
        out_specs=pl.BlockSpec((d, tn), lambda t, e: (0, t)),
        out_shape=jax.ShapeDtypeStruct((d, n), F32),
        scratch_shapes=[pltpu.VMEM((ec, tn), BF16)],
        compiler_params=_cparams(2),
        name="peer_experts",
    )(ht, u, vt, rank2, e2, n1, c2)


def _peer(h, w_q, keys, u_tab, v_tab):
    b, s, d = h.shape
    n = b * s
    tn = min(n, 512)
    ht = h.reshape(n, d).T
    wq_t = w_q.T.astype(BF16)
    kb = keys.reshape(2 * PEER_HEADS, PEER_NKEYS, PEER_DK).astype(BF16)
    scores = _peer_scores(ht, wq_t, kb, tn)
    rank2, e2, n1, c2 = _peer_route(scores, min(n, 256))
    out_t = _peer_experts(ht, u_tab.astype(BF16), v_tab.T.astype(BF16), rank2, e2, n1, c2, tn, 1024)
    return out_t.T.reshape(b, s, d)


def _rope_tables(t, n_ctx):
    rows = t // GRID_W
    row = np.repeat(np.arange(rows), GRID_W).astype(np.float32)
    col = np.tile(np.arange(GRID_W), rows).astype(np.float32)
    inv_freq = jnp.asarray(ROPE_THETA, F32) ** (-jnp.arange(ROPE_AXIS_FREQS, dtype=F32) / ROPE_AXIS_FREQS)
    ang = jnp.concatenate([jnp.asarray(row)[:, None] * inv_freq] * 2
                          + [jnp.asarray(col)[:, None] * inv_freq] * 2, axis=1)
    pad = lambda a, fill: jnp.concatenate([jnp.full((n_ctx, MLA_ROPE), fill, F32), a], axis=0)
    return pad(jnp.cos(ang), 1.0), pad(jnp.sin(ang), 0.0)


def _split_w_in(w_in):
    ml_cols = 4 * ML_W + ML_GATE_COLS
    w_ml = jnp.pad(w_in[:, :ml_cols], ((0, 0), (0, ML_PAD - ml_cols)))
    w_rw = w_in[:, ml_cols:ml_cols + RW_COLS]
    w_at = w_in[:, ml_cols + RW_COLS:]
    k_rope = w_at[:, MLA_Q_RANK + MLA_KV_RANK:]
    w_at = jnp.concatenate([w_at, _rot_cols(k_rope)], axis=1)
    w_at = jnp.pad(w_at, ((0, 0), (0, MLA_PAD - w_at.shape[1])))
    return w_ml.astype(BF16), w_rw.astype(BF16), w_at.astype(BF16)


def kernel(x, c, ctx, c_ctx, w_mod, b_mod, w_in, ml_conv_w, ml_conv_b, ml_i_bias, ml_f_bias, ml_norm_g,
           rw_mu, rw_w0, rw_w_up, rw_a0, rw_a_up, rw_g_up, rw_k_k, rw_k_a, rw_r_k, rw_gn_g, rw_gn_b,
           mla_q_norm_g, mla_q_up, mla_kv_norm_g, mla_kv_up, w_out, ln_mix_g, ln_mix_b,
           peer_w_q, peer_keys, peer_u, peer_v, ln_ffn_g, ln_ffn_b):
    b, t, d = x.shape
    n_ctx = ctx.shape[1]
    assert n_ctx % ROW_TILE == 0 and t % ROW_TILE == 0 and d == D_MODEL
    n_ctx_tiles = n_ctx // ROW_TILE
    cos, sin = _rope_tables(t, n_ctx)
    mod_rows = -(-(b + 1) // SUBLANES) * SUBLANES
    cvec = jnp.concatenate([c, c_ctx[None], jnp.zeros((mod_rows - b - 1, d), F32)], axis=0)

    x_all = jnp.concatenate([ctx, x], axis=1)
    depth = w_mod.shape[0]
    for l in range(depth):
        need_ctx = l < depth - 1
        mod = _modulation(cvec, w_mod[l], b_mod[l]).reshape(mod_rows, 6, d)
        mods = jnp.stack([jnp.broadcast_to(mod[b], (b, 6, d)), mod[:b]], axis=1)

        pml, prw, pat = _in_proj(x_all, mods, *_split_w_in(w_in[l]), n_ctx_tiles)

        q, k, lg = _ml_prep(pml, ml_conv_w[l], ml_conv_b[l], ml_i_bias[l], ml_f_bias[l], n_ctx_tiles)
        hf, hb = _ml_scan(q, k, pml, lg, n_ctx)

        r, kh, v, g, bonus, w, ab, kt = _rw_prep(
            prw, rw_mu[l], rw_w0[l], rw_w_up[l], rw_a0[l], rw_a_up[l], rw_g_up[l], rw_k_k[l], rw_k_a[l],
            rw_r_k[l], n_ctx_tiles)
        yf, yb = _rw_scan(r, kh, v, w, ab, kt, n_ctx)

        qn, qr, kn, va, kr = _mla_prep(pat, mla_q_norm_g[l], mla_q_up[l], mla_kv_norm_g[l], mla_kv_up[l],
                                       cos, sin)
        at_ctx, at_lat = _mla(qn, qr, kn, va, kr, n_ctx, need_ctx)
        first_tile = 0 if need_ctx else n_ctx_tiles
        y_at = jnp.concatenate([at_ctx, at_lat], axis=1) if need_ctx else jnp.pad(
            at_lat, ((0, 0), (n_ctx, 0), (0, 0)))

        x_mid, h_ffn = _out_proj(x_all, mods, pml, hf, hb, yf, yb, bonus, g, y_at, ml_norm_g[l], rw_gn_g[l],
                                 rw_gn_b[l], w_out[l], ln_mix_g[l], ln_mix_b[l], n_ctx_tiles, first_tile)
        y = _peer(h_ffn, peer_w_q[l], peer_keys[l], peer_u[l], peer_v[l])
        x_all = _final_norm(x_mid, y, mods, ln_ffn_g[l], ln_ffn_b[l], n_ctx_tiles, first_tile)
    return x_all
```

```python
import functools
import math

import jax
import jax.numpy as jnp
import numpy as np
from jax import lax
from jax.experimental import pallas as pl
from jax.experimental.pallas import tpu as pltpu

F32 = jnp.float32
BF16 = jnp.bfloat16
HIGHEST = lax.Precision.HIGHEST

LANES = 128
SUBLANES = 8

D_MODEL = 1024
DEPTH = 2
GRID_W = 64

ML_HEADS = 4
ML_DH = 64
ML_W = ML_HEADS * ML_DH
ML_CHUNK = 64
ML_M_INIT = -1e30
ML_GATE_COLS = 4 * ML_HEADS
ML_PAD = 4 * ML_W + LANES

RW_HEADS = 6
RW_DH = 64
RW_W = RW_HEADS * RW_DH
RW_PAIRS = RW_HEADS // 2
RW_LORA = 64
RW_G_LORA = 128
RW_DECAY_SCALE = math.exp(-0.5)
RW_GN_EPS = 64e-5
RW_COLS = 3 * RW_W + 4 * RW_LORA + RW_G_LORA
RW_CHUNK = 64

MLA_HEADS = 6
MLA_NOPE = 64
MLA_ROPE = 32
MLA_V = 64
MLA_Q_RANK = 384
MLA_KV_RANK = 256
MLA_W = MLA_HEADS * MLA_V
MLA_SCALE = (MLA_NOPE + MLA_ROPE) ** -0.5
MLA_PAD = 768
ROPE_AXIS_FREQS = MLA_ROPE // 4
ROPE_THETA = 10000.0

PEER_HEADS = 8
PEER_NKEYS = 128
PEER_EXPERTS = PEER_NKEYS * PEER_NKEYS
PEER_DK = 128
PEER_TOPK = 16

DEEPNORM_ALPHA = (2 * DEPTH) ** 0.25
LN_EPS = 1e-6

ROW_TILE = 256
VMEM_LIMIT = 56 * 1024 * 1024


def _cparams(n_axes):
    return pltpu.CompilerParams(dimension_semantics=("arbitrary",) * n_axes,
                                vmem_limit_bytes=VMEM_LIMIT)


def _ln_rows(x, eps=LN_EPS):
    mu = jnp.mean(x, axis=-1, keepdims=True)
    xc = x - mu
    var = jnp.mean(xc * xc, axis=-1, keepdims=True)
    return xc * lax.rsqrt(var + eps)


def _sigmoid(x):
    return 1.0 / (1.0 + jnp.exp(-x))


def _group_ln(x, gmat, eps):
    mu = jnp.dot(x, gmat, precision=HIGHEST, preferred_element_type=F32)
    xc = x - mu
    var = jnp.dot(xc * xc, gmat, precision=HIGHEST, preferred_element_type=F32)
    return xc * lax.rsqrt(var + eps)


def _group_matrix(width, group, value):
    idx = np.arange(width) // group
    return jnp.asarray((idx[:, None] == idx[None, :]).astype(np.float32) * value)


def _mod_kernel(c_ref, w_ref, b_ref, o_ref):
    c = c_ref[...]
    s = c * _sigmoid(c)
    o_ref[...] = jnp.dot(s, w_ref[...], precision=HIGHEST, preferred_element_type=F32) + b_ref[...]


def _modulation(cvec, w_mod, b_mod):
    rows, d = cvec.shape
    n = w_mod.shape[1]
    tn = 1536
    return pl.pallas_call(
        _mod_kernel,
        grid=(n // tn,),
        in_specs=[pl.BlockSpec((rows, d), lambda j: (0, 0)),
                  pl.BlockSpec((d, tn), lambda j: (0, j)),
                  pl.BlockSpec((1, tn), lambda j: (0, j))],
        out_specs=pl.BlockSpec((rows, tn), lambda j: (0, j)),
        out_shape=jax.ShapeDtypeStruct((rows, n), F32),
        compiler_params=_cparams(1),
        name="modulation",
    )(cvec, w_mod, b_mod.reshape(1, n))


def _in_proj_kernel(x_ref, mod_ref, wml_ref, wrw_ref, wat_ref, pml_ref, prw_ref, pat_ref):
    mod = mod_ref[0, 0]
    h = _ln_rows(x_ref[0]) * (1.0 + mod[1:2]) + mod[0:1]
    hb = h.astype(BF16)
    pml_ref[0] = jnp.dot(hb, wml_ref[...], preferred_element_type=F32)
    prw_ref[0] = jnp.dot(hb, wrw_ref[...], preferred_element_type=F32)
    pat_ref[0] = jnp.dot(hb, wat_ref[...], preferred_element_type=F32)


def _in_proj(x_all, mods, wml, wrw, wat, n_ctx_tiles):
    b, s, d = x_all.shape
    tm = ROW_TILE
    seg = lambda i: (i >= n_ctx_tiles).astype(jnp.int32)
    full = lambda w: pl.BlockSpec(w.shape, lambda bi, i: (0, 0))
    out = lambda w: pl.BlockSpec((1, tm, w.shape[1]), lambda bi, i: (bi, i, 0))
    return pl.pallas_call(
        _in_proj_kernel,
        grid=(b, s // tm),
        in_specs=[pl.BlockSpec((1, tm, d), lambda bi, i: (bi, i, 0)),
                  pl.BlockSpec((1, 1, 6, d), lambda bi, i: (bi, seg(i), 0, 0)),
                  full(wml), full(wrw), full(wat)],
        out_specs=[out(wml), out(wrw), out(wat)],
        out_shape=[jax.ShapeDtypeStruct((b, s, w.shape[1]), F32) for w in (wml, wrw, wat)],
        compiler_params=_cparams(2),
        name="in_proj",
    )(x_all, mods, wml, wrw, wat)


def _halo_specs(width, col_block, tm, s):
    per = tm // SUBLANES
    last = s // SUBLANES - 1
    prev = pl.BlockSpec((1, SUBLANES, width),
                        lambda bi, i: (bi, jnp.maximum(i * per - 1, 0), col_block))
    nxt = pl.BlockSpec((1, SUBLANES, width),
                       lambda bi, i: (bi, jnp.minimum((i + 1) * per, last), col_block))
    return prev, nxt


def _neighbours(x, prev_blk, next_blk, tile, n_ctx_tiles, n_tiles):
    tm = x.shape[0]
    row = lax.broadcasted_iota(jnp.int32, x.shape, 0)
    starts = jnp.logical_or(tile == 0, tile == n_ctx_tiles)
    ends = jnp.logical_or(tile == n_ctx_tiles - 1, tile == n_tiles - 1)
    prev_row = jnp.where(starts, 0.0, prev_blk[SUBLANES - 1:SUBLANES, :])
    next_row = jnp.where(ends, 0.0, next_blk[0:1, :])
    prev = jnp.where(row == 0, prev_row, pltpu.roll(x, 1, 0))
    nxt = jnp.where(row == tm - 1, next_row, pltpu.roll(x, tm - 1, 0))
    return prev, nxt


def _ml_prep_kernel(n_ctx_tiles, n_tiles, qk_ref, prev_ref, next_ref, gate_ref, cw_ref, cb_ref,
                    gb_ref, fmask_ref, q_ref, k_ref, lg_ref):
    tile = pl.program_id(1)
    x = qk_ref[0]
    prev, nxt = _neighbours(x, prev_ref[0], next_ref[0], tile, n_ctx_tiles, n_tiles)
    cw = cw_ref[...]
    z = prev * cw[0:1] + x * cw[1:2] + nxt * cw[2:3] + cb_ref[...]
    z = z * _sigmoid(z)
    q_ref[0] = z[:, :ML_W] * (ML_DH ** -0.5)
    k_ref[0] = z[:, ML_W:]
    g = gate_ref[0] + gb_ref[...]
    log_sig = jnp.minimum(g, 0.0) - jnp.log1p(jnp.exp(-jnp.abs(g)))
    lg_ref[0] = jnp.where(fmask_ref[...] > 0.5, log_sig, g)


def _ml_prep(pml, conv_w, conv_b, i_bias, f_bias, n_ctx_tiles):
    b, s, _ = pml.shape
    tm = ROW_TILE
    n_tiles = s // tm
    gate_bias = jnp.stack([i_bias, f_bias], axis=1).reshape(1, ML_GATE_COLS)
    gate_bias = jnp.pad(gate_bias, ((0, 0), (0, LANES - ML_GATE_COLS)))
    fmask = np.zeros((2, 2, ML_HEADS), np.float32)
    fmask[:, 1] = 1.0
    fmask = jnp.asarray(np.pad(fmask.reshape(1, -1), ((0, 0), (0, LANES - ML_GATE_COLS))))
    prev, nxt = _halo_specs(2 * ML_W, 0, tm, s)
    small = lambda a: pl.BlockSpec(a.shape, lambda bi, i: (0, 0))
    cb = conv_b.reshape(1, -1)
    return pl.pallas_call(
        functools.partial(_ml_prep_kernel, n_ctx_tiles, n_tiles),
        grid=(b, n_tiles),
        in_specs=[pl.BlockSpec((1, tm, 2 * ML_W), lambda bi, i: (bi, i, 0)), prev, nxt,
                  pl.BlockSpec((1, tm, LANES), lambda bi, i: (bi, i, 4 * ML_W // LANES)),
                  small(conv_w), small(cb), small(gate_bias), small(fmask)],
        out_specs=[pl.BlockSpec((1, tm, ML_W), lambda bi, i: (bi, i, 0)),
                   pl.BlockSpec((1, tm, ML_W), lambda bi, i: (bi, i, 0)),
                   pl.BlockSpec((1, tm, LANES), lambda bi, i: (bi, i, 0))],
        out_shape=[jax.ShapeDtypeStruct((b, s, ML_W), F32),
                   jax.ShapeDtypeStruct((b, s, ML_W), F32),
                   jax.ShapeDtypeStruct((b, s, LANES), F32)],
        compiler_params=_cparams(2),
        name="mlstm_prep",
    )(pml, pml, pml, pml, conv_w, cb, gate_bias, fmask)


def _ml_scan_kernel(n_chunks, n_ctx_chunks, q_ref, k_ref, v_ref, lg_ref, hf_ref, hb_ref,
                    ct_ref, n_ref, m_ref):
    L = ML_CHUNK
    units = 2 * ML_HEADS
    per_tile = LANES // ML_DH
    row = lax.broadcasted_iota(jnp.int32, (units, L, L), 1)
    col = lax.broadcasted_iota(jnp.int32, (units, L, L), 2)
    sign = jnp.where(lax.broadcasted_iota(jnp.int32, (units, L, L), 0) < ML_HEADS, 1, -1)
    eye = row == col
    seen = (col - row) * sign <= 0
    seen_t = (row - col) * sign <= 0
    h_refs = (hf_ref, hb_ref)

    def bmm(a, b):
        return jnp.einsum("uij,ujk->uik", a.astype(BF16), b.astype(BF16), preferred_element_type=F32)

    def chunk(c, _):
        back = jnp.where(c < n_ctx_chunks, n_ctx_chunks - 1 - c, n_ctx_chunks + n_chunks - 1 - c)
        rows = [pl.ds(pl.multiple_of(cc * L, L), L) for cc in (c, back)]
        q, k, kt, v, i_col, f_col = [], [], [], [], [], []
        for d in range(2):
            lg = lg_ref[0, rows[d], :]
            for tile in range(ML_W // LANES):
                lanes = slice(tile * LANES, (tile + 1) * LANES)
                q2, k2, v2 = q_ref[0, rows[d], lanes], k_ref[0, rows[d], lanes], v_ref[0, rows[d], lanes]
                kt2 = k2.T
                for hh in range(per_tile):
                    head = tile * per_tile + hh
                    sub = slice(hh * ML_DH, (hh + 1) * ML_DH)
                    gate = d * 2 * ML_HEADS + head
                    q.append(q2[:, sub]); k.append(k2[:, sub]); v.append(v2[:, sub]); kt.append(kt2[sub, :])
                    i_col.append(lg[:, gate:gate + 1])
                    f_col.append(lg[:, gate + ML_HEADS:gate + ML_HEADS + 1])
        q, k, kt, v, i_col, f_col = (jnp.stack(a, axis=0) for a in (q, k, kt, v, i_col, f_col))
        ct, n, m = ct_ref[...], n_ref[...], m_ref[...]

        bcum_row = jnp.sum(jnp.where(seen_t, f_col, 0.0), axis=1, keepdims=True)
        bcum_col = jnp.sum(jnp.where(eye, bcum_row, 0.0), axis=2, keepdims=True)
        i_row = jnp.sum(jnp.where(eye, i_col, 0.0), axis=1, keepdims=True)
        dmat = jnp.where(seen, bcum_col - bcum_row + i_row, -jnp.inf)
        rmax = jnp.max(dmat, axis=2, keepdims=True)
        sx = bmm(q, kt) * jnp.exp(dmat - rmax)
        sv = bmm(sx, v)
        rs = jnp.sum(sx, axis=2, keepdims=True)
        b_end = jnp.concatenate([bcum_col[:ML_HEADS, L - 1:L, :], bcum_col[ML_HEADS:, 0:1, :]], axis=0)
        g = b_end - bcum_col + i_col
        gmax = jnp.max(g, axis=1, keepdims=True)
        wkx = jnp.exp(g - gmax)
        kv = bmm(kt, wkx * v)
        nx = jnp.sum(wkx * k, axis=1, keepdims=True)
        a_inter = bcum_col + m
        m_t = jnp.maximum(a_inter, rmax)
        w_inter = jnp.exp(a_inter - m_t)
        scale = jnp.exp(rmax - m_t)
        num = w_inter * bmm(q, ct) + scale * sv
        den = w_inter * jnp.sum(q * n, axis=2, keepdims=True) + scale * rs
        h = num / jnp.maximum(jnp.abs(den), jnp.exp(-m_t))
        m_new = jnp.maximum(b_end + m, gmax)
        decay = jnp.exp(b_end + m - m_new)
        carry_scale = jnp.exp(gmax - m_new)
        ct_ref[...] = decay * ct + carry_scale * kv
        n_ref[...] = decay * n + carry_scale * nx
        m_ref[...] = m_new
        for d in range(2):
            for tile in range(ML_W // LANES):
                first = d * ML_HEADS + tile * per_tile
                h_refs[d][0, rows[d], tile * LANES:(tile + 1) * LANES] = jnp.concatenate(
                    [h[first + hh] for hh in range(per_tile)], axis=1)
        return 0

    ct_ref[...] = jnp.zeros_like(ct_ref)
    n_ref[...] = jnp.zeros_like(n_ref)
    m_ref[...] = jnp.full(m_ref.shape, ML_M_INIT, F32)
    lax.fori_loop(0, n_chunks, chunk, 0)


def _ml_scan(q, k, pml, lg, n_ctx):
    b, s, _ = q.shape
    nc = s // ML_CHUNK
    wide = lambda blk: pl.BlockSpec((1, s, ML_W), lambda bi: (bi, 0, blk))
    return pl.pallas_call(
        functools.partial(_ml_scan_kernel, nc, n_ctx // ML_CHUNK),
        grid=(b,),
        in_specs=[wide(0), wide(0), wide(2), pl.BlockSpec((1, s, LANES), lambda bi: (bi, 0, 0))],
        out_specs=[wide(0), wide(0)],
        out_shape=[jax.ShapeDtypeStruct((b, s, ML_W), F32)] * 2,
        scratch_shapes=[pltpu.VMEM((2 * ML_HEADS, ML_DH, ML_DH), F32),
                        pltpu.VMEM((2 * ML_HEADS, 1, ML_DH), F32),
                        pltpu.VMEM((2 * ML_HEADS, 1, 1), F32)],
        compiler_params=_cparams(1),
        name="mlstm_scan",
    )(q, k, pml, lg)


def _rw_prep_kernel(n_ctx_tiles, n_tiles, p_ref, prev_ref, next_ref, mu_ref, wup_ref, aup_ref,
                    gup_ref, w0_ref, a0_ref, kk_ref, ka_ref, rho_ref, gsum_ref,
                    r_ref, kh_ref, v_ref, g_ref, bonus_ref, w_ref, ab_ref, kt_ref):
    tile = pl.program_id(1)
    p = p_ref[0]
    prev, nxt = _neighbours(p, prev_ref[0], next_ref[0], tile, n_ctx_tiles, n_tiles)
    mu = mu_ref[...]
    p = p + mu[0:1] * (prev - p) + mu[1:2] * (nxt - p)
    r = p[:, 0:RW_W]
    k = p[:, RW_W:2 * RW_W]
    v = p[:, 2 * RW_W:3 * RW_W]
    base = 3 * RW_W
    wd = p[:, base:base + 2 * RW_LORA]
    ad = p[:, base + 2 * RW_LORA:base + 4 * RW_LORA]
    gd = p[:, base + 4 * RW_LORA:base + 4 * RW_LORA + RW_G_LORA]
    dot = functools.partial(jnp.dot, preferred_element_type=F32)
    w = jnp.exp(-RW_DECAY_SCALE * _sigmoid(w0_ref[...] + dot(jnp.tanh(wd).astype(BF16), wup_ref[...])))
    a = _sigmoid(a0_ref[...] + dot(ad.astype(BF16), aup_ref[...]))
    g = dot(_sigmoid(gd).astype(BF16), gup_ref[...])
    kk = k * kk_ref[...]
    ss = jnp.dot(kk * kk, gsum_ref[...], precision=HIGHEST, preferred_element_type=F32)
    kh = kk * lax.rsqrt(ss + 1e-12)
    ka = ka_ref[...]
    rk = r * rho_ref[...]
    kt_sum = jnp.zeros_like(k)
    for dr in range(2):
        a_d = a[:, dr * RW_W:(dr + 1) * RW_W]
        kt_d = k * (1.0 + (a_d - 1.0) * ka)
        kt_ref[0, :, dr * RW_W:(dr + 1) * RW_W] = kt_d
        ab_ref[0, :, dr * RW_W:(dr + 1) * RW_W] = kh * a_d
        kt_sum = kt_sum + kt_d
    bonus_ref[0] = jnp.dot(rk * kt_sum, gsum_ref[...], precision=HIGHEST,
                           preferred_element_type=F32) * v
    r_ref[0] = r
    kh_ref[0] = kh
    v_ref[0] = v
    g_ref[0] = g
    w_ref[0] = w


def _block_diag2(up):
    z = jnp.zeros_like(up[0])
    return jnp.concatenate([jnp.concatenate([up[0], z], axis=1),
                            jnp.concatenate([z, up[1]], axis=1)], axis=0)


def _rw_prep(prw, mu, w0, w_up, a0, a_up, g_up, k_k, k_a, r_k, n_ctx_tiles):
    b, s, _ = prw.shape
    tm = ROW_TILE
    n_tiles = s // tm
    prev, nxt = _halo_specs(RW_COLS, 0, tm, s)
    small = lambda a: pl.BlockSpec(a.shape, lambda bi, i: (0, 0))
    consts = [mu, _block_diag2(w_up).astype(BF16), _block_diag2(a_up).astype(BF16),
              g_up.astype(BF16), w0.reshape(1, 2 * RW_W), a0.reshape(1, 2 * RW_W),
              k_k.reshape(1, RW_W), k_a.reshape(1, RW_W), r_k.reshape(1, RW_W),
              _group_matrix(RW_W, RW_DH, 1.0)]
    one = lambda width, dt=F32: (pl.BlockSpec((1, tm, width), lambda bi, i: (bi, i, 0)),
                                 jax.ShapeDtypeStruct((b, s, width), dt))
    outs = [one(RW_W)] * 5 + [one(2 * RW_W)] * 3
    return pl.pallas_call(
        functools.partial(_rw_prep_kernel, n_ctx_tiles, n_tiles),
        grid=(b, n_tiles),
        in_specs=[pl.BlockSpec((1, tm, RW_COLS), lambda bi, i: (bi, i, 0)), prev, nxt]
                 + [small(a) for a in consts],
        out_specs=[o[0] for o in outs],
        out_shape=[o[1] for o in outs],
        compiler_params=_cparams(2),
        name="rwkv_prep",
    )(prw, prw, prw, *consts)


def _rw_scan_kernel(batch, *refs):
    ins = (refs[0:6], refs[6:12])
    ones_ref = refs[12]
    y_refs = refs[13:15]
    state_ref, yacc_ref, xs_ref, z_ref, vcat_ref = refs[15:20]
    L = RW_CHUNK
    items = batch * RW_PAIRS
    per_dir = items * RW_DH

    def item_rows(d, bi, p):
        start = ((d * batch + bi) * RW_PAIRS + p) * RW_DH
        return slice(start, start + RW_DH)

    @pl.when(pl.program_id(0) == 0)
    def _():
        state_ref[...] = jnp.zeros_like(state_ref)

    yacc_ref[...] = jnp.zeros_like(yacc_ref)
    for d in range(2):
        for bi in range(batch):
            for p in range(RW_PAIRS):
                vt = ins[d][5][bi, :, p * LANES:(p + 1) * LANES].T
                vcat_ref[item_rows(d, bi, p), :] = jnp.concatenate([vt[:RW_DH], vt[RW_DH:]], axis=1)
    lane = lax.broadcasted_iota(jnp.int32, (RW_DH, LANES), 1)
    step_of_lane = jnp.where(lane < RW_DH, lane, lane - RW_DH)
    sel_row = lax.broadcasted_iota(jnp.int32, (LANES, LANES), 0)
    sel_col = lax.broadcasted_iota(jnp.int32, (LANES, LANES), 1)
    same_head = (sel_row < RW_DH) == (sel_col < RW_DH)
    sel_step = jnp.where(sel_row < RW_DH, sel_row, sel_row - RW_DH)

    def step(j, _):
        steps = (j, L - 1 - j)
        rows_t = [[ref[steps[d]] for ref in ins[d][:5]] for d in range(2)]
        vec = lambda rows, bi, p: rows[bi:bi + 1, p * LANES:(p + 1) * LANES]

        for d in range(2):
            kh_t = rows_t[d][3]
            for bi in range(batch):
                for p in range(RW_PAIRS):
                    rows = item_rows(d, bi, p)
                    xs_ref[rows, :] = (state_ref[rows, :] * vec(kh_t, bi, p)).astype(BF16)
        sa = jnp.dot(xs_ref[...], ones_ref[...], preferred_element_type=F32)
        pick = [jnp.where(lane < RW_DH, steps[d], steps[d] + RW_DH) for d in range(2)]

        for d in range(2):
            w_t, ab_t, kt_t, _, r_t = rows_t[d]
            for bi in range(batch):
                for p in range(RW_PAIRS):
                    rows = item_rows(d, bi, p)
                    local = slice(rows.start - d * per_dir, rows.stop - d * per_dir)
                    vb = jnp.take_along_axis(vcat_ref[rows, :], pick[d], axis=1)
                    st = (state_ref[rows, :] * vec(w_t, bi, p) - sa[rows] * vec(ab_t, bi, p)
                          + vb * vec(kt_t, bi, p))
                    state_ref[rows, :] = st
                    z_ref[rows, :] = (st * vec(r_t, bi, p)).astype(BF16)
        y = jnp.dot(z_ref[...], ones_ref[...], preferred_element_type=F32)
        for d in range(2):
            here = step_of_lane == steps[d]
            for bi in range(batch):
                for p in range(RW_PAIRS):
                    rows = item_rows(d, bi, p)
                    yacc_ref[rows, :] = jnp.where(here, y[rows], yacc_ref[rows, :])
        return 0

    lax.fori_loop(0, L, step, 0)
    for d in range(2):
        for bi in range(batch):
            for p in range(RW_PAIRS):
                ya = yacc_ref[item_rows(d, bi, p), :]
                ya = jnp.concatenate([ya[:, :RW_DH], ya[:, RW_DH:]], axis=0)
                y_refs[d][bi, :, p * LANES:(p + 1) * LANES] = ya.T


def _rw_scan(r, kh, v, w, ab, kt, n_ctx):
    b, s, _ = r.shape
    L = RW_CHUNK
    nc = s // L
    ncc = n_ctx // L
    rows = b * RW_PAIRS * RW_DH
    cmap = (lambda c: c,
            lambda c: jnp.where(c < ncc, ncc - 1 - c, ncc + nc - 1 - c))
    time_major = lambda a: a.transpose(1, 0, 2)
    head_ones = _group_matrix(LANES, RW_DH, 1.0).astype(BF16)
    in_specs, args = [], []
    for d in range(2):
        cm = cmap[d]
        per_dir = lambda c, cm=cm, d=d: (cm(c), 0, d)
        shared = lambda c, cm=cm: (cm(c), 0, 0)
        in_specs += [pl.BlockSpec((L, b, RW_W), per_dir)] * 3
        in_specs += [pl.BlockSpec((L, b, RW_W), shared)] * 2
        in_specs += [pl.BlockSpec((b, L, RW_W), lambda c, cm=cm: (0, cm(c), 0))]
        args += [time_major(a) for a in (w, ab, kt, kh, r)] + [v]
    in_specs.append(pl.BlockSpec(head_ones.shape, lambda c: (0, 0)))
    args.append(head_ones)
    out_specs = [pl.BlockSpec((b, L, RW_W), lambda c, cm=cm: (0, cm(c), 0)) for cm in cmap]
    return pl.pallas_call(
        functools.partial(_rw_scan_kernel, b),
        grid=(nc,),
        in_specs=in_specs,
        out_specs=out_specs,
        out_shape=[jax.ShapeDtypeStruct((b, s, RW_W), F32)] * 2,
        scratch_shapes=[pltpu.VMEM((2 * rows, LANES), F32), pltpu.VMEM((2 * rows, LANES), F32),
                        pltpu.VMEM((2 * rows, LANES), BF16), pltpu.VMEM((2 * rows, LANES), BF16),
                        pltpu.VMEM((2 * rows, LANES), F32)],
        compiler_params=_cparams(1),
        name="rwkv_scan",
    )(*args)


def _mla_prep_kernel(p_ref, cosq_ref, sinq_ref, cosk_ref, sink_ref, qg_ref, kvg_ref, wq_ref, wkv_ref,
                     qn_ref, qr_ref, kn_ref, v_ref, kr_ref):
    p = p_ref[0]
    cq = p[:, :MLA_Q_RANK]
    ckv = p[:, MLA_Q_RANK:MLA_Q_RANK + MLA_KV_RANK]
    base = MLA_Q_RANK + MLA_KV_RANK
    k_rope = p[:, base:base + MLA_ROPE]
    k_rot = p[:, base + MLA_ROPE:base + 2 * MLA_ROPE]
    rms = lambda x, g: x * lax.rsqrt(jnp.mean(x * x, axis=-1, keepdims=True) + LN_EPS) * g
    q = jnp.dot(rms(cq, qg_ref[...]).astype(BF16), wq_ref[...], preferred_element_type=F32)
    kv = jnp.dot(rms(ckv, kvg_ref[...]).astype(BF16), wkv_ref[...], preferred_element_type=F32)
    nope = MLA_HEADS * MLA_NOPE
    rope = MLA_HEADS * MLA_ROPE
    qn_ref[0] = q[:, :nope]
    qr_ref[0] = q[:, nope:nope + rope] * cosq_ref[...] + q[:, nope + rope:nope + 2 * rope] * sinq_ref[...]
    kn_ref[0] = kv[:, :nope]
    v_ref[0] = kv[:, nope:]
    kr_ref[0] = k_rope * cosk_ref[...] + k_rot * sink_ref[...]


def _rot_cols(w):
    shape = w.shape
    w = w.reshape(shape[:-1] + (shape[-1] // 16, 2, 8))
    return jnp.concatenate([-w[..., 1:2, :], w[..., 0:1, :]], axis=-2).reshape(shape)


def _mla_prep(pat, q_norm_g, q_up, kv_norm_g, kv_up, cos, sin):
    b, s, _ = pat.shape
    tm = ROW_TILE
    qw = q_up.reshape(MLA_Q_RANK, MLA_HEADS, MLA_NOPE + MLA_ROPE)
    q_nope = qw[:, :, :MLA_NOPE].reshape(MLA_Q_RANK, -1)
    q_rope = qw[:, :, MLA_NOPE:]
    wq = jnp.concatenate([q_nope, q_rope.reshape(MLA_Q_RANK, -1),
                          _rot_cols(q_rope).reshape(MLA_Q_RANK, -1)], axis=1).astype(BF16)
    kvw = kv_up.reshape(MLA_KV_RANK, MLA_HEADS, MLA_NOPE + MLA_V)
    wkv = jnp.concatenate([kvw[:, :, :MLA_NOPE].reshape(MLA_KV_RANK, -1),
                           kvw[:, :, MLA_NOPE:].reshape(MLA_KV_RANK, -1)], axis=1).astype(BF16)
    cosq = jnp.tile(cos, (1, MLA_HEADS))
    sinq = jnp.tile(sin, (1, MLA_HEADS))
    rows = lambda width: pl.BlockSpec((tm, width), lambda bi, i: (i, 0))
    small = lambda a: pl.BlockSpec(a.shape, lambda bi, i: (0, 0))
    qg = q_norm_g.reshape(1, -1)
    kvg = kv_norm_g.reshape(1, -1)
    out = lambda width: (pl.BlockSpec((1, tm, width), lambda bi, i: (bi, i, 0)),
                         jax.ShapeDtypeStruct((b, s, width), F32))
    outs = [out(MLA_HEADS * MLA_NOPE), out(MLA_HEADS * MLA_ROPE), out(MLA_HEADS * MLA_NOPE),
            out(MLA_W), out(MLA_ROPE)]
    return pl.pallas_call(
        _mla_prep_kernel,
        grid=(b, s // tm),
        in_specs=[pl.BlockSpec((1, tm, MLA_PAD), lambda bi, i: (bi, i, 0)),
                  rows(MLA_HEADS * MLA_ROPE), rows(MLA_HEADS * MLA_ROPE), rows(MLA_ROPE), rows(MLA_ROPE),
                  small(qg), small(kvg), small(wq), small(wkv)],
        out_specs=[o[0] for o in outs],
        out_shape=[o[1] for o in outs],
        compiler_params=_cparams(2),
        name="mla_prep",
    )(pat, cosq, sinq, cos, sin, qg, kvg, wq, wkv)


def _attn_kernel(q_ref, k_ref, v_ref, o_ref):
    s = lax.dot_general(q_ref[0, 0], k_ref[0, 0], (((1,), (1,)), ((), ())),
                        preferred_element_type=F32) * MLA_SCALE
    p = jnp.exp(s - jnp.max(s, axis=-1, keepdims=True))
    denom = jnp.sum(p, axis=-1, keepdims=True)
    o_ref[0, 0] = jnp.dot(p.astype(BF16), v_ref[0, 0], preferred_element_type=F32) / denom


def _attention(q, k, v, tq):
    b, h, t, dq = q.shape
    sk = k.shape[2]
    dv = v.shape[3]
    return pl.pallas_call(
        _attn_kernel,
        grid=(b, h, t // tq),
        in_specs=[pl.BlockSpec((1, 1, tq, dq), lambda bi, hi, i: (bi, hi, i, 0)),
                  pl.BlockSpec((1, 1, sk, dq), lambda bi, hi, i: (bi, hi, 0, 0)),
                  pl.BlockSpec((1, 1, sk, dv), lambda bi, hi, i: (bi, hi, 0, 0))],
        out_specs=pl.BlockSpec((1, 1, tq, dv), lambda bi, hi, i: (bi, hi, i, 0)),
        out_shape=jax.ShapeDtypeStruct((b, h, t, dv), F32),
        compiler_params=_cparams(3),
        name="mla_attention",
    )(q, k, v)


def _mla(qn, qr, kn, v, kr, n_ctx, need_ctx):
    b, s, _ = qn.shape
    heads = lambda a, w: a.reshape(b, s, MLA_HEADS, w).transpose(0, 2, 1, 3)
    q = jnp.concatenate([heads(qn, MLA_NOPE), heads(qr, MLA_ROPE)], axis=-1).astype(BF16)
    k_rope = jnp.broadcast_to(kr[:, None], (b, MLA_HEADS, s, MLA_ROPE))
    k = jnp.concatenate([heads(kn, MLA_NOPE), k_rope], axis=-1).astype(BF16)
    vh = heads(v, MLA_V).astype(BF16)
    pad = ((0, 0), (0, 0), (0, 0), (0, LANES - MLA_NOPE - MLA_ROPE))
    q, k = jnp.pad(q, pad), jnp.pad(k, pad)
    merge = lambda a: a.transpose(0, 2, 1, 3).reshape(b, -1, MLA_W)
    t = s - n_ctx
    y_lat = merge(_attention(q[:, :, n_ctx:], k, vh, min(t, 512)))
    if not need_ctx:
        return None, y_lat
    y_ctx = merge(_attention(q[:, :, :n_ctx], k[:, :, :n_ctx], vh[:, :, :n_ctx], n_ctx))
    return y_ctx, y_lat


def _out_proj_kernel(x_ref, mod_ref, hf_ref, hb_ref, o_ref, yf_ref, yb_ref, bonus_ref, g_ref, at_ref,
                     g4_ref, g6_ref, mlg_ref, gng_ref, gnb_ref, wml_ref, wrw_ref, wat_ref,
                     lng_ref, lnb_ref, xo_ref, ho_ref):
    mod = mod_ref[0, 0]
    ml = _sigmoid(o_ref[0]) * _group_ln(hf_ref[0] + hb_ref[0], g4_ref[...], LN_EPS) * mlg_ref[...]
    z = _group_ln(yf_ref[0] + yb_ref[0], g6_ref[...], RW_GN_EPS) * gng_ref[...] + gnb_ref[...]
    rw = (z + bonus_ref[0]) * g_ref[0]
    dot = functools.partial(jnp.dot, preferred_element_type=F32)
    y = (dot(ml.astype(BF16), wml_ref[...]) + dot(rw.astype(BF16), wrw_ref[...])
         + dot(at_ref[0].astype(BF16), wat_ref[...]))
    x = _ln_rows(DEEPNORM_ALPHA * x_ref[0] + mod[2:3] * y) * lng_ref[...] + lnb_ref[...]
    xo_ref[0] = x
    ho_ref[0] = (_ln_rows(x) * (1.0 + mod[4:5]) + mod[3:4]).astype(BF16)


def _out_proj(x_all, mods, pml, hf, hb, yf, yb, bonus, g, y_at, norm_g, gn_g, gn_b, w_out,
              ln_g, ln_b, n_ctx_tiles, first_tile):
    b, s, d = x_all.shape
    tm = ROW_TILE
    n_tiles = s // tm - first_tile
    seg = lambda i: (i + first_tile >= n_ctx_tiles).astype(jnp.int32)
    rows = lambda width, blk=0: pl.BlockSpec((1, tm, width), lambda bi, i: (bi, i + first_tile, blk))
    small = lambda a: pl.BlockSpec(a.shape, lambda bi, i: (0, 0))
    wb = w_out.astype(BF16)
    consts = [_group_matrix(ML_W, ML_DH, 1.0 / ML_DH), _group_matrix(RW_W, RW_DH, 1.0 / RW_DH),
              norm_g.reshape(1, -1), gn_g.reshape(1, -1), gn_b.reshape(1, -1),
              wb[:ML_W], wb[ML_W:ML_W + RW_W], wb[ML_W + RW_W:],
              ln_g.reshape(1, -1), ln_b.reshape(1, -1)]
    out_rows = lambda: pl.BlockSpec((1, tm, d), lambda bi, i: (bi, i, 0))
    return pl.pallas_call(
        _out_proj_kernel,
        grid=(b, n_tiles),
        in_specs=[rows(d), pl.BlockSpec((1, 1, 6, d), lambda bi, i: (bi, seg(i), 0, 0)),
                  rows(ML_W), rows(ML_W), rows(ML_W, 3 * ML_W // ML_W),
                  rows(RW_W), rows(RW_W), rows(RW_W), rows(RW_W), rows(MLA_W)]
                 + [small(a) for a in consts],
        out_specs=[out_rows(), out_rows()],
        out_shape=[jax.ShapeDtypeStruct((b, n_tiles * tm, d), F32),
                   jax.ShapeDtypeStruct((b, n_tiles * tm, d), BF16)],
        compiler_params=_cparams(2),
        name="out_proj",
    )(x_all, mods, hf, hb, pml, yf, yb, bonus, g, y_at, *consts)


def _final_norm_kernel(x_ref, y_ref, mod_ref, g_ref, b_ref, o_ref):
    mod = mod_ref[0, 0]
    o_ref[0] = _ln_rows(DEEPNORM_ALPHA * x_ref[0] + mod[5:6] * y_ref[0]) * g_ref[...] + b_ref[...]


def _final_norm(x, y, mods, ln_g, ln_b, n_ctx_tiles, first_tile):
    b, s, d = x.shape
    tm = ROW_TILE
    seg = lambda i: (i + first_tile >= n_ctx_tiles).astype(jnp.int32)
    rows = pl.BlockSpec((1, tm, d), lambda bi, i: (bi, i, 0))
    small = pl.BlockSpec((1, d), lambda bi, i: (0, 0))
    return pl.pallas_call(
        _final_norm_kernel,
        grid=(b, s // tm),
        in_specs=[rows, rows, pl.BlockSpec((1, 1, 6, d), lambda bi, i: (bi, seg(i), 0, 0)), small, small],
        out_specs=rows,
        out_shape=jax.ShapeDtypeStruct((b, s, d), F32),
        compiler_params=_cparams(2),
        name="final_norm",
    )(x, y, mods, ln_g.reshape(1, d), ln_b.reshape(1, d))


def _peer_scores_kernel(ht_ref, wq_ref, keys_ref, s_ref):
    q = jnp.dot(wq_ref[...], ht_ref[...], preferred_element_type=F32).astype(BF16)
    for g in range(2 * PEER_HEADS):
        s_ref[g] = jnp.dot(keys_ref[g], q[g * PEER_DK:(g + 1) * PEER_DK], preferred_element_type=F32)


def _peer_scores(ht, wq_t, keys, tn):
    d, n = ht.shape
    groups = 2 * PEER_HEADS
    return pl.pallas_call(
        _peer_scores_kernel,
        grid=(n // tn,),
        in_specs=[pl.BlockSpec((d, tn), lambda t: (0, t)),
                  pl.BlockSpec(wq_t.shape, lambda t: (0, 0)),
                  pl.BlockSpec(keys.shape, lambda t: (0, 0, 0))],
        out_specs=pl.BlockSpec((groups, PEER_NKEYS, tn), lambda t: (0, 0, t)),
        out_shape=jax.ShapeDtypeStruct((groups, PEER_NKEYS, n), F32),
        compiler_params=_cparams(1),
        name="peer_scores",
    )(ht, wq_t, keys)


PEER_RANKS = PEER_TOPK + 1


PEER_UNRANKED = float(PEER_NKEYS - 1)


def _top_rows(s, count):
    rows = []
    rank = jnp.full(s.shape, PEER_UNRANKED, F32)
    for r in range(count):
        m = jnp.max(s, axis=0, keepdims=True)
        rows.append(m)
        hit = s == m
        rank = jnp.where(hit, float(r), rank)
        s = jnp.where(hit, -jnp.inf, s)
    return rows, rank


def _peer_route_kernel(s_ref, rank2_ref, e2_ref, n1_ref, c2_ref, cand_ref):
    cand_ref[...] = jnp.full(cand_ref.shape, -jnp.inf, F32)
    for h in range(PEER_HEADS):
        s1 = s_ref[2 * h]
        s2 = s_ref[2 * h + 1]
        a, _ = _top_rows(s1, PEER_RANKS)
        b, rank2 = _top_rows(s2, PEER_RANKS)
        pairs = [(p, q) for p in range(PEER_RANKS) for q in range(PEER_RANKS // (p + 1))]
        for slot, (p, q) in enumerate(pairs):
            cand_ref[slot:slot + 1, :] = a[p] + b[q]
        top, _ = _top_rows(cand_ref[...], PEER_RANKS)
        threshold = 0.5 * (top[PEER_TOPK - 1] + top[PEER_TOPK])
        z = sum(jnp.exp(row - top[0]) for row in top[:PEER_TOPK])
        need = threshold - s1
        n1 = jnp.zeros_like(s1)
        for q in range(PEER_RANKS):
            n1 = n1 + jnp.where(b[q] >= need, 1.0, 0.0)
        rank2_ref[h] = rank2.astype(BF16)
        e2_ref[h] = jnp.exp(s2 - b[0]).astype(BF16)
        n1_ref[h] = n1
        c2_ref[h] = jnp.exp(s1 - a[0]) / z


PEER_CAND_ROWS = 56


def _peer_route(scores, tn):
    groups, nk, n = scores.shape
    spec = pl.BlockSpec((PEER_HEADS, nk, tn), lambda t: (0, 0, t))
    shape = lambda dt: jax.ShapeDtypeStruct((PEER_HEADS, nk, n), dt)
    return pl.pallas_call(
        _peer_route_kernel,
        grid=(n // tn,),
        in_specs=[pl.BlockSpec((groups, nk, tn), lambda t: (0, 0, t))],
        out_specs=[spec] * 4,
        out_shape=[shape(BF16), shape(BF16), shape(F32), shape(F32)],
        scratch_shapes=[pltpu.VMEM((PEER_CAND_ROWS, tn), F32)],
        compiler_params=_cparams(1),
        name="peer_route",
    )(scores)


def _gelu_tanh(x):
    k = -2.0 * math.sqrt(2.0 / math.pi)
    return x / (1.0 + jnp.exp(x * (k + (k * 0.044715) * (x * x))))


def _peer_experts_kernel(blocks, ht_ref, u_ref, vt_ref, rank2_ref, e2_ref, n1_ref, c2_ref, o_ref, w_ref):
    e = pl.program_id(1)

    @pl.when(e == 0)
    def _():
        o_ref[...] = jnp.zeros_like(o_ref)

    ht = ht_ref[...]
    for ii in range(blocks):
        i = e * blocks + ii
        rows = slice(ii * PEER_NKEYS, (ii + 1) * PEER_NKEYS)
        act = _gelu_tanh(jnp.dot(u_ref[rows, :], ht, preferred_element_type=F32).astype(BF16))
        gate = jnp.zeros(act.shape, BF16)
        for h in range(PEER_HEADS):
            n1 = n1_ref[h, pl.ds(i, 1), :].astype(BF16)
            c2 = c2_ref[h, pl.ds(i, 1), :].astype(BF16)
            gate = gate + jnp.where(rank2_ref[h] < n1, e2_ref[h] * c2, jnp.zeros((), BF16))
        w_ref[rows, :] = gate * act
    o_ref[...] += jnp.dot(vt_ref[...], w_ref[...], preferred_element_type=F32)


def _peer_experts(ht, u, vt, rank2, e2, n1, c2, tn, ec):
    d, n = ht.shape
    blocks = ec // PEER_NKEYS
    route = pl.BlockSpec((PEER_HEADS, PEER_NKEYS, tn), lambda t, e: (0, 0, t))
    return pl.pallas_call(
        functools.partial(_peer_experts_kernel, blocks),
        grid=(n // tn, PEER_EXPERTS // ec),
        in_specs=[pl.BlockSpec((d, tn), lambda t, e: (0, t)),
                  pl.BlockSpec((ec, d), lambda t, e: (e, 0)),
                  pl.BlockSpec((d, ec), lambda t, e: (0, e)),
                  route, route, route, route],
        out_specs=pl.BlockSpec((d, tn), lambda t, e: (0, t)),
        out_shape=jax.ShapeDtypeStruct((d, n), F32),
        scratch_shapes=[pltpu.VMEM((ec, tn), BF16)],
        compiler_params=_cparams(2),
        name="peer_experts",
    )(ht, u, vt, rank2, e2, n1, c2)


def _peer(h, w_q, keys, u_tab, v_tab):
    b, s, d = h.shape
    n = b * s
    tn = min(n, 512)
    ht = h.reshape(n, d).T
    wq_t = w_q.T.astype(BF16)
    kb = keys.reshape(2 * PEER_HEADS, PEER_NKEYS, PEER_DK).astype(BF16)
    scores = _peer_scores(ht, wq_t, kb, tn)
    rank2, e2, n1, c2 = _peer_route(scores, min(n, 256))
    out_t = _peer_experts(ht, u_tab.astype(BF16), v_tab.T.astype(BF16), rank2, e2, n1, c2, tn, 1024)
    return out_t.T.reshape(b, s, d)


def _rope_tables(t, n_ctx):
    rows = t // GRID_W
    row = np.repeat(np.arange(rows), GRID_W).astype(np.float32)
    col = np.tile(np.arange(GRID_W), rows).astype(np.float32)
    inv_freq = jnp.asarray(ROPE_THETA, F32) ** (-jnp.arange(ROPE_AXIS_FREQS, dtype=F32) / ROPE_AXIS_FREQS)
    ang = jnp.concatenate([jnp.asarray(row)[:, None] * inv_freq] * 2
                          + [jnp.asarray(col)[:, None] * inv_freq] * 2, axis=1)
    pad = lambda a, fill: jnp.concatenate([jnp.full((n_ctx, MLA_ROPE), fill, F32), a], axis=0)
    return pad(jnp.cos(ang), 1.0), pad(jnp.sin(ang), 0.0)


def _split_w_in(w_in):
    ml_cols = 4 * ML_W + ML_GATE_COLS
    w_ml = jnp.pad(w_in[:, :ml_cols], ((0, 0), (0, ML_PAD - ml_cols)))
    w_rw = w_in[:, ml_cols:ml_cols + RW_COLS]
    w_at = w_in[:, ml_cols + RW_COLS:]
    k_rope = w_at[:, MLA_Q_RANK + MLA_KV_RANK:]
    w_at = jnp.concatenate([w_at, _rot_cols(k_rope)], axis=1)
    w_at = jnp.pad(w_at, ((0, 0), (0, MLA_PAD - w_at.shape[1])))
    return w_ml.astype(BF16), w_rw.astype(BF16), w_at.astype(BF16)


def kernel(x, c, ctx, c_ctx, w_mod, b_mod, w_in, ml_conv_w, ml_conv_b, ml_i_bias, ml_f_bias, ml_norm_g,
           rw_mu, rw_w0, rw_w_up, rw_a0, rw_a_up, rw_g_up, rw_k_k, rw_k_a, rw_r_k, rw_gn_g, rw_gn_b,
           mla_q_norm_g, mla_q_up, mla_kv_norm_g, mla_kv_up, w_out, ln_mix_g, ln_mix_b,
           peer_w_q, peer_keys, peer_u, peer_v, ln_ffn_g, ln_ffn_b):
    b, t, d = x.shape
    n_ctx = ctx.shape[1]
    assert n_ctx % ROW_TILE == 0 and t % ROW_TILE == 0 and d == D_MODEL
    n_ctx_tiles = n_ctx // ROW_TILE
    cos, sin = _rope_tables(t, n_ctx)
    mod_rows = -(-(b + 1) // SUBLANES) * SUBLANES
    cvec = jnp.concatenate([c, c_ctx[None], jnp.zeros((mod_rows - b - 1, d), F32)], axis=0)

    x_all = jnp.concatenate([ctx, x], axis=1)
    depth = w_mod.shape[0]
    for l in range(depth):
        need_ctx = l < depth - 1
        mod = _modulation(cvec, w_mod[l], b_mod[l]).reshape(mod_rows, 6, d)
        mods = jnp.stack([jnp.broadcast_to(mod[b], (b, 6, d)), mod[:b]], axis=1)

        pml, prw, pat = _in_proj(x_all, mods, *_split_w_in(w_in[l]), n_ctx_tiles)

        q, k, lg = _ml_prep(pml, ml_conv_w[l], ml_conv_b[l], ml_i_bias[l], ml_f_bias[l], n_ctx_tiles)
        hf, hb = _ml_scan(q, k, pml, lg, n_ctx)

        r, kh, v, g, bonus, w, ab, kt = _rw_prep(
            prw, rw_mu[l], rw_w0[l], rw_w_up[l], rw_a0[l], rw_a_up[l], rw_g_up[l], rw_k_k[l], rw_k_a[l],
            rw_r_k[l], n_ctx_tiles)
        yf, yb = _rw_scan(r, kh, v, w, ab, kt, n_ctx)

        qn, qr, kn, va, kr = _mla_prep(pat, mla_q_norm_g[l], mla_q_up[l], mla_kv_norm_g[l], mla_kv_up[l],
                                       cos, sin)
        at_ctx, at_lat = _mla(qn, qr, kn, va, kr, n_ctx, need_ctx)
        first_tile = 0 if need_ctx else n_ctx_tiles
        y_at = jnp.concatenate([at_ctx, at_lat], axis=1) if need_ctx else jnp.pad(
            at_lat, ((0, 0), (n_ctx, 0), (0, 0)))

        x_mid, h_ffn = _out_proj(x_all, mods, pml, hf, hb, yf, yb, bonus, g, y_at, ml_norm_g[l], rw_gn_g[l],
                                 rw_gn_b[l], w_out[l], ln_mix_g[l], ln_mix_b[l], n_ctx_tiles, first_tile)
        y = _peer(h_ffn, peer_w_q[l], peer_keys[l], peer_u[l], peer_v[l])
        x_all = _final_norm(x_mid, y, mods, ln_ffn_g[l], ln_ffn_b[l], n_ctx_tiles, first_tile)
    return x_all
```

```python
import functools
import math

import jax
import jax.numpy as jnp
import numpy as np
from jax import lax
from jax.experimental import pallas as pl
from jax.experimental.pallas import tpu as pltpu

F32 = jnp.float32
BF16 = jnp.bfloat16
HIGHEST = lax.Precision.HIGHEST

LANES = 128
SUBLANES = 8

D_MODEL = 1024
DEPTH = 2
GRID_W = 64

ML_HEADS = 4
ML_DH = 64
ML_W = ML_HEADS * ML_DH
ML_CHUNK = 64
ML_M_INIT = -1e30
ML_GATE_COLS = 4 * ML_HEADS
ML_PAD = 4 * ML_W + LANES

RW_HEADS = 6
RW_DH = 64
RW_W = RW_HEADS * RW_DH
RW_PAIRS = RW_HEADS // 2
RW_LORA = 64
RW_G_LORA = 128
RW_DECAY_SCALE = math.exp(-0.5)
RW_GN_EPS = 64e-5
RW_COLS = 3 * RW_W + 4 * RW_LORA + RW_G_LORA
RW_CHUNK = 64

MLA_HEADS = 6
MLA_NOPE = 64
MLA_ROPE = 32
MLA_V = 64
MLA_Q_RANK = 384
MLA_KV_RANK = 256
MLA_W = MLA_HEADS * MLA_V
MLA_SCALE = (MLA_NOPE + MLA_ROPE) ** -0.5
MLA_PAD = 768
ROPE_AXIS_FREQS = MLA_ROPE // 4
ROPE_THETA = 10000.0

PEER_HEADS = 8
PEER_NKEYS = 128
PEER_EXPERTS = PEER_NKEYS * PEER_NKEYS
PEER_DK = 128
PEER_TOPK = 16

DEEPNORM_ALPHA = (2 * DEPTH) ** 0.25
LN_EPS = 1e-6

ROW_TILE = 256
VMEM_LIMIT = 56 * 1024 * 1024


def _cparams(n_axes):
    return pltpu.CompilerParams(dimension_semantics=("arbitrary",) * n_axes,
                                vmem_limit_bytes=VMEM_LIMIT)


def _ln_rows(x, eps=LN_EPS):
    mu = jnp.mean(x, axis=-1, keepdims=True)
    xc = x - mu
    var = jnp.mean(xc * xc, axis=-1, keepdims=True)
    return xc * lax.rsqrt(var + eps)


def _sigmoid(x):
    return 1.0 / (1.0 + jnp.exp(-x))


def _group_ln(x, gmat, eps):
    mu = jnp.dot(x, gmat, precision=HIGHEST, preferred_element_type=F32)
    xc = x - mu
    var = jnp.dot(xc * xc, gmat, precision=HIGHEST, preferred_element_type=F32)
    return xc * lax.rsqrt(var + eps)


def _group_matrix(width, group, value):
    idx = np.arange(width) // group
    return jnp.asarray((idx[:, None] == idx[None, :]).astype(np.float32) * value)


def _mod_kernel(c_ref, w_ref, b_ref, o_ref):
    c = c_ref[...]
    s = c * _sigmoid(c)
    o_ref[...] = jnp.dot(s, w_ref[...], precision=HIGHEST, preferred_element_type=F32) + b_ref[...]


def _modulation(cvec, w_mod, b_mod):
    rows, d = cvec.shape
    n = w_mod.shape[1]
    tn = 1536
    return pl.pallas_call(
        _mod_kernel,
        grid=(n // tn,),
        in_specs=[pl.BlockSpec((rows, d), lambda j: (0, 0)),
                  pl.BlockSpec((d, tn), lambda j: (0, j)),
                  pl.BlockSpec((1, tn), lambda j: (0, j))],
        out_specs=pl.BlockSpec((rows, tn), lambda j: (0, j)),
        out_shape=jax.ShapeDtypeStruct((rows, n), F32),
        compiler_params=_cparams(1),
        name="modulation",
    )(cvec, w_mod, b_mod.reshape(1, n))


def _in_proj_kernel(x_ref, mod_ref, wml_ref, wrw_ref, wat_ref, pml_ref, prw_ref, pat_ref):
    mod = mod_ref[0, 0]
    h = _ln_rows(x_ref[0]) * (1.0 + mod[1:2]) + mod[0:1]
    hb = h.astype(BF16)
    pml_ref[0] = jnp.dot(hb, wml_ref[...], preferred_element_type=F32)
    prw_ref[0] = jnp.dot(hb, wrw_ref[...], preferred_element_type=F32)
    pat_ref[0] = jnp.dot(hb, wat_ref[...], preferred_element_type=F32)


def _in_proj(x_all, mods, wml, wrw, wat, n_ctx_tiles):
    b, s, d = x_all.shape
    tm = ROW_TILE
    seg = lambda i: (i >= n_ctx_tiles).astype(jnp.int32)
    full = lambda w: pl.BlockSpec(w.shape, lambda bi, i: (0, 0))
    out = lambda w: pl.BlockSpec((1, tm, w.shape[1]), lambda bi, i: (bi, i, 0))
    return pl.pallas_call(
        _in_proj_kernel,
        grid=(b, s // tm),
        in_specs=[pl.BlockSpec((1, tm, d), lambda bi, i: (bi, i, 0)),
                  pl.BlockSpec((1, 1, 6, d), lambda bi, i: (bi, seg(i), 0, 0)),
                  full(wml), full(wrw), full(wat)],
        out_specs=[out(wml), out(wrw), out(wat)],
        out_shape=[jax.ShapeDtypeStruct((b, s, w.shape[1]), F32) for w in (wml, wrw, wat)],
        compiler_params=_cparams(2),
        name="in_proj",
    )(x_all, mods, wml, wrw, wat)


def _halo_specs(width, col_block, tm, s):
    per = tm // SUBLANES
    last = s // SUBLANES - 1
    prev = pl.BlockSpec((1, SUBLANES, width),
                        lambda bi, i: (bi, jnp.maximum(i * per - 1, 0), col_block))
    nxt = pl.BlockSpec((1, SUBLANES, width),
                       lambda bi, i: (bi, jnp.minimum((i + 1) * per, last), col_block))
    return prev, nxt


def _neighbours(x, prev_blk, next_blk, tile, n_ctx_tiles, n_tiles):
    tm = x.shape[0]
    row = lax.broadcasted_iota(jnp.int32, x.shape, 0)
    starts = jnp.logical_or(tile == 0, tile == n_ctx_tiles)
    ends = jnp.logical_or(tile == n_ctx_tiles - 1, tile == n_tiles - 1)
    prev_row = jnp.where(starts, 0.0, prev_blk[SUBLANES - 1:SUBLANES, :])
    next_row = jnp.where(ends, 0.0, next_blk[0:1, :])
    prev = jnp.where(row == 0, prev_row, pltpu.roll(x, 1, 0))
    nxt = jnp.where(row == tm - 1, next_row, pltpu.roll(x, tm - 1, 0))
    return prev, nxt


def _ml_prep_kernel(n_ctx_tiles, n_tiles, qk_ref, prev_ref, next_ref, gate_ref, cw_ref, cb_ref,
                    gb_ref, fmask_ref, q_ref, k_ref, lg_ref):
    tile = pl.program_id(1)
    x = qk_ref[0]
    prev, nxt = _neighbours(x, prev_ref[0], next_ref[0], tile, n_ctx_tiles, n_tiles)
    cw = cw_ref[...]
    z = prev * cw[0:1] + x * cw[1:2] + nxt * cw[2:3] + cb_ref[...]
    z = z * _sigmoid(z)
    q_ref[0] = z[:, :ML_W] * (ML_DH ** -0.5)
    k_ref[0] = z[:, ML_W:]
    g = gate_ref[0] + gb_ref[...]
    log_sig = jnp.minimum(g, 0.0) - jnp.log1p(jnp.exp(-jnp.abs(g)))
    lg_ref[0] = jnp.where(fmask_ref[...] > 0.5, log_sig, g)


def _ml_prep(pml, conv_w, conv_b, i_bias, f_bias, n_ctx_tiles):
    b, s, _ = pml.shape
    tm = ROW_TILE
    n_tiles = s // tm
    gate_bias = jnp.stack([i_bias, f_bias], axis=1).reshape(1, ML_GATE_COLS)
    gate_bias = jnp.pad(gate_bias, ((0, 0), (0, LANES - ML_GATE_COLS)))
    fmask = np.zeros((2, 2, ML_HEADS), np.float32)
    fmask[:, 1] = 1.0
    fmask = jnp.asarray(np.pad(fmask.reshape(1, -1), ((0, 0), (0, LANES - ML_GATE_COLS))))
    prev, nxt = _halo_specs(2 * ML_W, 0, tm, s)
    small = lambda a: pl.BlockSpec(a.shape, lambda bi, i: (0, 0))
    cb = conv_b.reshape(1, -1)
    return pl.pallas_call(
        functools.partial(_ml_prep_kernel, n_ctx_tiles, n_tiles),
        grid=(b, n_tiles),
        in_specs=[pl.BlockSpec((1, tm, 2 * ML_W), lambda bi, i: (bi, i, 0)), prev, nxt,
                  pl.BlockSpec((1, tm, LANES), lambda bi, i: (bi, i, 4 * ML_W // LANES)),
                  small(conv_w), small(cb), small(gate_bias), small(fmask)],
        out_specs=[pl.BlockSpec((1, tm, ML_W), lambda bi, i: (bi, i, 0)),
                   pl.BlockSpec((1, tm, ML_W), lambda bi, i: (bi, i, 0)),
                   pl.BlockSpec((1, tm, LANES), lambda bi, i: (bi, i, 0))],
        out_shape=[jax.ShapeDtypeStruct((b, s, ML_W), F32),
                   jax.ShapeDtypeStruct((b, s, ML_W), F32),
                   jax.ShapeDtypeStruct((b, s, LANES), F32)],
        compiler_params=_cparams(2),
        name="mlstm_prep",
    )(pml, pml, pml, pml, conv_w, cb, gate_bias, fmask)


def _ml_scan_kernel(n_chunks, n_ctx_chunks, q_ref, k_ref, v_ref, lg_ref, hf_ref, hb_ref,
                    ct_ref, n_ref, m_ref):
    L = ML_CHUNK
    units = 2 * ML_HEADS
    per_tile = LANES // ML_DH
    row = lax.broadcasted_iota(jnp.int32, (units, L, L), 1)
    col = lax.broadcasted_iota(jnp.int32, (units, L, L), 2)
    sign = jnp.where(lax.broadcasted_iota(jnp.int32, (units, L, L), 0) < ML_HEADS, 1, -1)
    eye = row == col
    seen = (col - row) * sign <= 0
    seen_t = (row - col) * sign <= 0
    h_refs = (hf_ref, hb_ref)

    def bmm(a, b):
        return jnp.einsum("uij,ujk->uik", a.astype(BF16), b.astype(BF16), preferred_element_type=F32)

    def chunk(c, _):
        back = jnp.where(c < n_ctx_chunks, n_ctx_chunks - 1 - c, n_ctx_chunks + n_chunks - 1 - c)
        rows = [pl.ds(pl.multiple_of(cc * L, L), L) for cc in (c, back)]
        q, k, kt, v, i_col, f_col = [], [], [], [], [], []
        for d in range(2):
            lg = lg_ref[0, rows[d], :]
            for tile in range(ML_W // LANES):
                lanes = slice(tile * LANES, (tile + 1) * LANES)
                q2, k2, v2 = q_ref[0, rows[d], lanes], k_ref[0, rows[d], lanes], v_ref[0, rows[d], lanes]
                kt2 = k2.T
                for hh in range(per_tile):
                    head = tile * per_tile + hh
                    sub = slice(hh * ML_DH, (hh + 1) * ML_DH)
                    gate = d * 2 * ML_HEADS + head
                    q.append(q2[:, sub]); k.append(k2[:, sub]); v.append(v2[:, sub]); kt.append(kt2[sub, :])
                    i_col.append(lg[:, gate:gate + 1])
                    f_col.append(lg[:, gate + ML_HEADS:gate + ML_HEADS + 1])
        q, k, kt, v, i_col, f_col = (jnp.stack(a, axis=0) for a in (q, k, kt, v, i_col, f_col))
        ct, n, m = ct_ref[...], n_ref[...], m_ref[...]

        bcum_row = jnp.sum(jnp.where(seen_t, f_col, 0.0), axis=1, keepdims=True)
        bcum_col = jnp.sum(jnp.where(eye, bcum_row, 0.0), axis=2, keepdims=True)
        i_row = jnp.sum(jnp.where(eye, i_col, 0.0), axis=1, keepdims=True)
        dmat = jnp.where(seen, bcum_col - bcum_row + i_row, -jnp.inf)
        rmax = jnp.max(dmat, axis=2, keepdims=True)
        sx = bmm(q, kt) * jnp.exp(dmat - rmax)
        sv = bmm(sx, v)
        rs = jnp.sum(sx, axis=2, keepdims=True)
        b_end = jnp.concatenate([bcum_col[:ML_HEADS, L - 1:L, :], bcum_col[ML_HEADS:, 0:1, :]], axis=0)
        g = b_end - bcum_col + i_col
        gmax = jnp.max(g, axis=1, keepdims=True)
        wkx = jnp.exp(g - gmax)
        kv = bmm(kt, wkx * v)
        nx = jnp.sum(wkx * k, axis=1, keepdims=True)
        a_inter = bcum_col + m
        m_t = jnp.maximum(a_inter, rmax)
        w_inter = jnp.exp(a_inter - m_t)
        scale = jnp.exp(rmax - m_t)
        num = w_inter * bmm(q, ct) + scale * sv
        den = w_inter * jnp.sum(q * n, axis=2, keepdims=True) + scale * rs
        h = num / jnp.maximum(jnp.abs(den), jnp.exp(-m_t))
        m_new = jnp.maximum(b_end + m, gmax)
        decay = jnp.exp(b_end + m - m_new)
        carry_scale = jnp.exp(gmax - m_new)
        ct_ref[...] = decay * ct + carry_scale * kv
        n_ref[...] = decay * n + carry_scale * nx
        m_ref[...] = m_new
        for d in range(2):
            for tile in range(ML_W // LANES):
                first = d * ML_HEADS + tile * per_tile
                h_refs[d][0, rows[d], tile * LANES:(tile + 1) * LANES] = jnp.concatenate(
                    [h[first + hh] for hh in range(per_tile)], axis=1)
        return 0

    ct_ref[...] = jnp.zeros_like(ct_ref)
    n_ref[...] = jnp.zeros_like(n_ref)
    m_ref[...] = jnp.full(m_ref.shape, ML_M_INIT, F32)
    lax.fori_loop(0, n_chunks, chunk, 0)


def _ml_scan(q, k, pml, lg, n_ctx):
    b, s, _ = q.shape
    nc = s // ML_CHUNK
    wide = lambda blk: pl.BlockSpec((1, s, ML_W), lambda bi: (bi, 0, blk))
    return pl.pallas_call(
        functools.partial(_ml_scan_kernel, nc, n_ctx // ML_CHUNK),
        grid=(b,),
        in_specs=[wide(0), wide(0), wide(2), pl.BlockSpec((1, s, LANES), lambda bi: (bi, 0, 0))],
        out_specs=[wide(0), wide(0)],
        out_shape=[jax.ShapeDtypeStruct((b, s, ML_W), F32)] * 2,
        scratch_shapes=[pltpu.VMEM((2 * ML_HEADS, ML_DH, ML_DH), F32),
                        pltpu.VMEM((2 * ML_HEADS, 1, ML_DH), F32),
                        pltpu.VMEM((2 * ML_HEADS, 1, 1), F32)],
        compiler_params=_cparams(1),
        name="mlstm_scan",
    )(q, k, pml, lg)


def _rw_prep_kernel(n_ctx_tiles, n_tiles, p_ref, prev_ref, next_ref, mu_ref, wup_ref, aup_ref,
                    gup_ref, w0_ref, a0_ref, kk_ref, ka_ref, rho_ref, gsum_ref,
                    r_ref, kh_ref, v_ref, g_ref, bonus_ref, w_ref, ab_ref, kt_ref):
    tile = pl.program_id(1)
    p = p_ref[0]
    prev, nxt = _neighbours(p, prev_ref[0], next_ref[0], tile, n_ctx_tiles, n_tiles)
    mu = mu_ref[...]
    p = p + mu[0:1] * (prev - p) + mu[1:2] * (nxt - p)
    r = p[:, 0:RW_W]
    k = p[:, RW_W:2 * RW_W]
    v = p[:, 2 * RW_W:3 * RW_W]
    base = 3 * RW_W
    wd = p[:, base:base + 2 * RW_LORA]
    ad = p[:, base + 2 * RW_LORA:base + 4 * RW_LORA]
    gd = p[:, base + 4 * RW_LORA:base + 4 * RW_LORA + RW_G_LORA]
    dot = functools.partial(jnp.dot, preferred_element_type=F32)
    w = jnp.exp(-RW_DECAY_SCALE * _sigmoid(w0_ref[...] + dot(jnp.tanh(wd).astype(BF16), wup_ref[...])))
    a = _sigmoid(a0_ref[...] + dot(ad.astype(BF16), aup_ref[...]))
    g = dot(_sigmoid(gd).astype(BF16), gup_ref[...])
    kk = k * kk_ref[...]
    ss = jnp.dot(kk * kk, gsum_ref[...], precision=HIGHEST, preferred_element_type=F32)
    kh = kk * lax.rsqrt(ss + 1e-12)
    ka = ka_ref[...]
    rk = r * rho_ref[...]
    kt_sum = jnp.zeros_like(k)
    for dr in range(2):
        a_d = a[:, dr * RW_W:(dr + 1) * RW_W]
        kt_d = k * (1.0 + (a_d - 1.0) * ka)
        kt_ref[0, :, dr * RW_W:(dr + 1) * RW_W] = kt_d
        ab_ref[0, :, dr * RW_W:(dr + 1) * RW_W] = kh * a_d
        kt_sum = kt_sum + kt_d
    bonus_ref[0] = jnp.dot(rk * kt_sum, gsum_ref[...], precision=HIGHEST,
                           preferred_element_type=F32) * v
    r_ref[0] = r
    kh_ref[0] = kh
    v_ref[0] = v
    g_ref[0] = g
    w_ref[0] = w


def _block_diag2(up):
    z = jnp.zeros_like(up[0])
    return jnp.concatenate([jnp.concatenate([up[0], z], axis=1),
                            jnp.concatenate([z, up[1]], axis=1)], axis=0)


def _rw_prep(prw, mu, w0, w_up, a0, a_up, g_up, k_k, k_a, r_k, n_ctx_tiles):
    b, s, _ = prw.shape
    tm = ROW_TILE
    n_tiles = s // tm
    prev, nxt = _halo_specs(RW_COLS, 0, tm, s)
    small = lambda a: pl.BlockSpec(a.shape, lambda bi, i: (0, 0))
    consts = [mu, _block_diag2(w_up).astype(BF16), _block_diag2(a_up).astype(BF16),
              g_up.astype(BF16), w0.reshape(1, 2 * RW_W), a0.reshape(1, 2 * RW_W),
              k_k.reshape(1, RW_W), k_a.reshape(1, RW_W), r_k.reshape(1, RW_W),
              _group_matrix(RW_W, RW_DH, 1.0)]
    one = lambda width, dt=F32: (pl.BlockSpec((1, tm, width), lambda bi, i: (bi, i, 0)),
                                 jax.ShapeDtypeStruct((b, s, width), dt))
    outs = [one(RW_W)] * 5 + [one(2 * RW_W)] * 3
    return pl.pallas_call(
        functools.partial(_rw_prep_kernel, n_ctx_tiles, n_tiles),
        grid=(b, n_tiles),
        in_specs=[pl.BlockSpec((1, tm, RW_COLS), lambda bi, i: (bi, i, 0)), prev, nxt]
                 + [small(a) for a in consts],
        out_specs=[o[0] for o in outs],
        out_shape=[o[1] for o in outs],
        compiler_params=_cparams(2),
        name="rwkv_prep",
    )(prw, prw, prw, *consts)


def _rw_scan_kernel(batch, *refs):
    ins = (refs[0:6], refs[6:12])
    ones_ref = refs[12]
    y_refs = refs[13:15]
    state_ref, yacc_ref, xs_ref, z_ref, vcat_ref = refs[15:20]
    L = RW_CHUNK
    items = batch * RW_PAIRS
    per_dir = items * RW_DH

    def item_rows(d, bi, p):
        start = ((d * batch + bi) * RW_PAIRS + p) * RW_DH
        return slice(start, start + RW_DH)

    @pl.when(pl.program_id(0) == 0)
    def _():
        state_ref[...] = jnp.zeros_like(state_ref)

    yacc_ref[...] = jnp.zeros_like(yacc_ref)
    for d in range(2):
        for bi in range(batch):
            for p in range(RW_PAIRS):
                vt = ins[d][5][bi, :, p * LANES:(p + 1) * LANES].T
                vcat_ref[item_rows(d, bi, p), :] = jnp.concatenate([vt[:RW_DH], vt[RW_DH:]], axis=1)
    lane = lax.broadcasted_iota(jnp.int32, (RW_DH, LANES), 1)
    step_of_lane = jnp.where(lane < RW_DH, lane, lane - RW_DH)
    sel_row = lax.broadcasted_iota(jnp.int32, (LANES, LANES), 0)
    sel_col = lax.broadcasted_iota(jnp.int32, (LANES, LANES), 1)
    same_head = (sel_row < RW_DH) == (sel_col < RW_DH)
    sel_step = jnp.where(sel_row < RW_DH, sel_row, sel_row - RW_DH)

    def step(j, _):
        steps = (j, L - 1 - j)
        rows_t = [[ref[steps[d]] for ref in ins[d][:5]] for d in range(2)]
        vec = lambda rows, bi, p: rows[bi:bi + 1, p * LANES:(p + 1) * LANES]

        for d in range(2):
            kh_t = rows_t[d][3]
            for bi in range(batch):
                for p in range(RW_PAIRS):
                    rows = item_rows(d, bi, p)
                    xs_ref[rows, :] = (state_ref[rows, :] * vec(kh_t, bi, p)).astype(BF16)
        sa = jnp.dot(xs_ref[...], ones_ref[...], preferred_element_type=F32)
        pick = [jnp.where(lane < RW_DH, steps[d], steps[d] + RW_DH) for d in range(2)]

        for d in range(2):
            w_t, ab_t, kt_t, _, r_t = rows_t[d]
            for bi in range(batch):
                for p in range(RW_PAIRS):
                    rows = item_rows(d, bi, p)
                    local = slice(rows.start - d * per_dir, rows.stop - d * per_dir)
                    vb = jnp.take_along_axis(vcat_ref[rows, :], pick[d], axis=1)
                    st = (state_ref[rows, :] * vec(w_t, bi, p) - sa[rows] * vec(ab_t, bi, p)
                          + vb * vec(kt_t, bi, p))
                    state_ref[rows, :] = st
                    z_ref[rows, :] = (st * vec(r_t, bi, p)).astype(BF16)
        y = jnp.dot(z_ref[...], ones_ref[...], preferred_element_type=F32)
        for d in range(2):
            here = step_of_lane == steps[d]
            for bi in range(batch):
                for p in range(RW_PAIRS):
                    rows = item_rows(d, bi, p)
                    yacc_ref[rows, :] = jnp.where(here, y[rows], yacc_ref[rows, :])
        return 0

    lax.fori_loop(0, L, step, 0)
    for d in range(2):
        for bi in range(batch):
            for p in range(RW_PAIRS):
                ya = yacc_ref[item_rows(d, bi, p), :]
                ya = jnp.concatenate([ya[:, :RW_DH], ya[:, RW_DH:]], axis=0)
                y_refs[d][bi, :, p * LANES:(p + 1) * LANES] = ya.T


def _rw_scan(r, kh, v, w, ab, kt, n_ctx):
    b, s, _ = r.shape
    L = RW_CHUNK
    nc = s // L
    ncc = n_ctx // L
    rows = b * RW_PAIRS * RW_DH
    cmap = (lambda c: c,
            lambda c: jnp.where(c < ncc, ncc - 1 - c, ncc + nc - 1 - c))
    time_major = lambda a: a.transpose(1, 0, 2)
    head_ones = _group_matrix(LANES, RW_DH, 1.0).astype(BF16)
    in_specs, args = [], []
    for d in range(2):
        cm = cmap[d]
        per_dir = lambda c, cm=cm, d=d: (cm(c), 0, d)
        shared = lambda c, cm=cm: (cm(c), 0, 0)
        in_specs += [pl.BlockSpec((L, b, RW_W), per_dir)] * 3
        in_specs += [pl.BlockSpec((L, b, RW_W), shared)] * 2
        in_specs += [pl.BlockSpec((b, L, RW_W), lambda c, cm=cm: (0, cm(c), 0))]
        args += [time_major(a) for a in (w, ab, kt, kh, r)] + [v]
    in_specs.append(pl.BlockSpec(head_ones.shape, lambda c: (0, 0)))
    args.append(head_ones)
    out_specs = [pl.BlockSpec((b, L, RW_W), lambda c, cm=cm: (0, cm(c), 0)) for cm in cmap]
    return pl.pallas_call(
        functools.partial(_rw_scan_kernel, b),
        grid=(nc,),
        in_specs=in_specs,
        out_specs=out_specs,
        out_shape=[jax.ShapeDtypeStruct((b, s, RW_W), F32)] * 2,
        scratch_shapes=[pltpu.VMEM((2 * rows, LANES), F32), pltpu.VMEM((2 * rows, LANES), F32),
                        pltpu.VMEM((2 * rows, LANES), BF16), pltpu.VMEM((2 * rows, LANES), BF16),
                        pltpu.VMEM((2 * rows, LANES), F32)],
        compiler_params=_cparams(1),
        name="rwkv_scan",
    )(*args)


def _mla_prep_kernel(p_ref, cosq_ref, sinq_ref, cosk_ref, sink_ref, qg_ref, kvg_ref, wq_ref, wkv_ref,
                     qn_ref, qr_ref, kn_ref, v_ref, kr_ref):
    p = p_ref[0]
    cq = p[:, :MLA_Q_RANK]
    ckv = p[:, MLA_Q_RANK:MLA_Q_RANK + MLA_KV_RANK]
    base = MLA_Q_RANK + MLA_KV_RANK
    k_rope = p[:, base:base + MLA_ROPE]
    k_rot = p[:, base + MLA_ROPE:base + 2 * MLA_ROPE]
    rms = lambda x, g: x * lax.rsqrt(jnp.mean(x * x, axis=-1, keepdims=True) + LN_EPS) * g
    q = jnp.dot(rms(cq, qg_ref[...]).astype(BF16), wq_ref[...], preferred_element_type=F32)
    kv = jnp.dot(rms(ckv, kvg_ref[...]).astype(BF16), wkv_ref[...], preferred_element_type=F32)
    nope = MLA_HEADS * MLA_NOPE
    rope = MLA_HEADS * MLA_ROPE
    q = q * (MLA_SCALE * math.log2(math.e))
    qn_ref[0] = q[:, :nope]
    qr_ref[0] = q[:, nope:nope + rope] * cosq_ref[...] + q[:, nope + rope:nope + 2 * rope] * sinq_ref[...]
    kn_ref[0] = kv[:, :nope]
    v_ref[0] = kv[:, nope:]
    kr_ref[0] = k_rope * cosk_ref[...] + k_rot * sink_ref[...]


def _rot_cols(w):
    shape = w.shape
    w = w.reshape(shape[:-1] + (shape[-1] // 16, 2, 8))
    return jnp.concatenate([-w[..., 1:2, :], w[..., 0:1, :]], axis=-2).reshape(shape)


def _mla_prep(pat, q_norm_g, q_up, kv_norm_g, kv_up, cos, sin):
    b, s, _ = pat.shape
    tm = ROW_TILE
    qw = q_up.reshape(MLA_Q_RANK, MLA_HEADS, MLA_NOPE + MLA_ROPE)
    q_nope = qw[:, :, :MLA_NOPE].reshape(MLA_Q_RANK, -1)
    q_rope = qw[:, :, MLA_NOPE:]
    wq = jnp.concatenate([q_nope, q_rope.reshape(MLA_Q_RANK, -1),
                          _rot_cols(q_rope).reshape(MLA_Q_RANK, -1)], axis=1).astype(BF16)
    kvw = kv_up.reshape(MLA_KV_RANK, MLA_HEADS, MLA_NOPE + MLA_V)
    wkv = jnp.concatenate([kvw[:, :, :MLA_NOPE].reshape(MLA_KV_RANK, -1),
                           kvw[:, :, MLA_NOPE:].reshape(MLA_KV_RANK, -1)], axis=1).astype(BF16)
    cosq = jnp.tile(cos, (1, MLA_HEADS))
    sinq = jnp.tile(sin, (1, MLA_HEADS))
    rows = lambda width: pl.BlockSpec((tm, width), lambda bi, i: (i, 0))
    small = lambda a: pl.BlockSpec(a.shape, lambda bi, i: (0, 0))
    qg = q_norm_g.reshape(1, -1)
    kvg = kv_norm_g.reshape(1, -1)
    out = lambda width: (pl.BlockSpec((1, tm, width), lambda bi, i: (bi, i, 0)),
                         jax.ShapeDtypeStruct((b, s, width), F32))
    outs = [out(MLA_HEADS * MLA_NOPE), out(MLA_HEADS * MLA_ROPE), out(MLA_HEADS * MLA_NOPE),
            out(MLA_W), out(MLA_ROPE)]
    return pl.pallas_call(
        _mla_prep_kernel,
        grid=(b, s // tm),
        in_specs=[pl.BlockSpec((1, tm, MLA_PAD), lambda bi, i: (bi, i, 0)),
                  rows(MLA_HEADS * MLA_ROPE), rows(MLA_HEADS * MLA_ROPE), rows(MLA_ROPE), rows(MLA_ROPE),
                  small(qg), small(kvg), small(wq), small(wkv)],
        out_specs=[o[0] for o in outs],
        out_shape=[o[1] for o in outs],
        compiler_params=_cparams(2),
        name="mla_prep",
    )(pat, cosq, sinq, cos, sin, qg, kvg, wq, wkv)


ATTN_HEADS_PER_STEP = 3


def _attn_kernel(q_ref, k_ref, v_ref, o_ref):
    for h in range(ATTN_HEADS_PER_STEP):
        s = lax.dot_general(q_ref[0, h], k_ref[0, h], (((1,), (1,)), ((), ())),
                            preferred_element_type=F32)
        p = jnp.exp2(s - jnp.max(s, axis=-1, keepdims=True))
        denom = jnp.sum(p, axis=-1, keepdims=True)
        o_ref[0, h] = jnp.dot(p.astype(BF16), v_ref[0, h], preferred_element_type=F32) / denom


def _attention(q, k, v, tq):
    b, h, t, dq = q.shape
    sk = k.shape[2]
    dv = v.shape[3]
    hp = ATTN_HEADS_PER_STEP
    return pl.pallas_call(
        _attn_kernel,
        grid=(b, h // hp, t // tq),
        in_specs=[pl.BlockSpec((1, hp, tq, dq), lambda bi, hi, i: (bi, hi, i, 0)),
                  pl.BlockSpec((1, hp, sk, dq), lambda bi, hi, i: (bi, hi, 0, 0)),
                  pl.BlockSpec((1, hp, sk, dv), lambda bi, hi, i: (bi, hi, 0, 0))],
        out_specs=pl.BlockSpec((1, hp, tq, dv), lambda bi, hi, i: (bi, hi, i, 0)),
        out_shape=jax.ShapeDtypeStruct((b, h, t, dv), F32),
        compiler_params=_cparams(3),
        name="mla_attention",
    )(q, k, v)


def _mla(qn, qr, kn, v, kr, n_ctx, need_ctx):
    b, s, _ = qn.shape
    heads = lambda a, w: a.reshape(b, s, MLA_HEADS, w).transpose(0, 2, 1, 3)
    q = jnp.concatenate([heads(qn, MLA_NOPE), heads(qr, MLA_ROPE)], axis=-1).astype(BF16)
    k_rope = jnp.broadcast_to(kr[:, None], (b, MLA_HEADS, s, MLA_ROPE))
    k = jnp.concatenate([heads(kn, MLA_NOPE), k_rope], axis=-1).astype(BF16)
    vh = heads(v, MLA_V).astype(BF16)
    pad = ((0, 0), (0, 0), (0, 0), (0, LANES - MLA_NOPE - MLA_ROPE))
    q, k = jnp.pad(q, pad), jnp.pad(k, pad)
    merge = lambda a: a.transpose(0, 2, 1, 3).reshape(b, -1, MLA_W)
    t = s - n_ctx
    y_lat = merge(_attention(q[:, :, n_ctx:], k, vh, min(t, 512)))
    if not need_ctx:
        return None, y_lat
    y_ctx = merge(_attention(q[:, :, :n_ctx], k[:, :, :n_ctx], vh[:, :, :n_ctx], n_ctx))
    return y_ctx, y_lat


def _out_proj_kernel(x_ref, mod_ref, hf_ref, hb_ref, o_ref, yf_ref, yb_ref, bonus_ref, g_ref, at_ref,
                     g4_ref, g6_ref, mlg_ref, gng_ref, gnb_ref, wml_ref, wrw_ref, wat_ref,
                     lng_ref, lnb_ref, xo_ref, ho_ref):
    mod = mod_ref[0, 0]
    ml = _sigmoid(o_ref[0]) * _group_ln(hf_ref[0] + hb_ref[0], g4_ref[...], LN_EPS) * mlg_ref[...]
    z = _group_ln(yf_ref[0] + yb_ref[0], g6_ref[...], RW_GN_EPS) * gng_ref[...] + gnb_ref[...]
    rw = (z + bonus_ref[0]) * g_ref[0]
    dot = functools.partial(jnp.dot, preferred_element_type=F32)
    y = (dot(ml.astype(BF16), wml_ref[...]) + dot(rw.astype(BF16), wrw_ref[...])
         + dot(at_ref[0].astype(BF16), wat_ref[...]))
    x = _ln_rows(DEEPNORM_ALPHA * x_ref[0] + mod[2:3] * y) * lng_ref[...] + lnb_ref[...]
    xo_ref[0] = x
    ho_ref[0] = (_ln_rows(x) * (1.0 + mod[4:5]) + mod[3:4]).astype(BF16)


def _out_proj(x_all, mods, pml, hf, hb, yf, yb, bonus, g, y_at, norm_g, gn_g, gn_b, w_out,
              ln_g, ln_b, n_ctx_tiles, first_tile):
    b, s, d = x_all.shape
    tm = ROW_TILE
    n_tiles = s // tm - first_tile
    seg = lambda i: (i + first_tile >= n_ctx_tiles).astype(jnp.int32)
    rows = lambda width, blk=0: pl.BlockSpec((1, tm, width), lambda bi, i: (bi, i + first_tile, blk))
    small = lambda a: pl.BlockSpec(a.shape, lambda bi, i: (0, 0))
    wb = w_out.astype(BF16)
    consts = [_group_matrix(ML_W, ML_DH, 1.0 / ML_DH), _group_matrix(RW_W, RW_DH, 1.0 / RW_DH),
              norm_g.reshape(1, -1), gn_g.reshape(1, -1), gn_b.reshape(1, -1),
              wb[:ML_W], wb[ML_W:ML_W + RW_W], wb[ML_W + RW_W:],
              ln_g.reshape(1, -1), ln_b.reshape(1, -1)]
    out_rows = lambda: pl.BlockSpec((1, tm, d), lambda bi, i: (bi, i, 0))
    return pl.pallas_call(
        _out_proj_kernel,
        grid=(b, n_tiles),
        in_specs=[rows(d), pl.BlockSpec((1, 1, 6, d), lambda bi, i: (bi, seg(i), 0, 0)),
                  rows(ML_W), rows(ML_W), rows(ML_W, 3 * ML_W // ML_W),
                  rows(RW_W), rows(RW_W), rows(RW_W), rows(RW_W), rows(MLA_W)]
                 + [small(a) for a in consts],
        out_specs=[out_rows(), out_rows()],
        out_shape=[jax.ShapeDtypeStruct((b, n_tiles * tm, d), F32),
                   jax.ShapeDtypeStruct((b, n_tiles * tm, d), BF16)],
        compiler_params=_cparams(2),
        name="out_proj",
    )(x_all, mods, hf, hb, pml, yf, yb, bonus, g, y_at, *consts)


def _final_norm_kernel(x_ref, y_ref, mod_ref, g_ref, b_ref, o_ref):
    mod = mod_ref[0, 0]
    o_ref[0] = _ln_rows(DEEPNORM_ALPHA * x_ref[0] + mod[5:6] * y_ref[0]) * g_ref[...] + b_ref[...]


def _final_norm(x, y, mods, ln_g, ln_b, n_ctx_tiles, first_tile):
    b, s, d = x.shape
    tm = ROW_TILE
    seg = lambda i: (i + first_tile >= n_ctx_tiles).astype(jnp.int32)
    rows = pl.BlockSpec((1, tm, d), lambda bi, i: (bi, i, 0))
    small = pl.BlockSpec((1, d), lambda bi, i: (0, 0))
    return pl.pallas_call(
        _final_norm_kernel,
        grid=(b, s // tm),
        in_specs=[rows, rows, pl.BlockSpec((1, 1, 6, d), lambda bi, i: (bi, seg(i), 0, 0)), small, small],
        out_specs=rows,
        out_shape=jax.ShapeDtypeStruct((b, s, d), F32),
        compiler_params=_cparams(2),
        name="final_norm",
    )(x, y, mods, ln_g.reshape(1, d), ln_b.reshape(1, d))


def _peer_scores_kernel(ht_ref, wq_ref, keys_ref, s_ref):
    q = jnp.dot(wq_ref[...], ht_ref[...], preferred_element_type=F32).astype(BF16)
    for g in range(2 * PEER_HEADS):
        s_ref[g] = jnp.dot(keys_ref[g], q[g * PEER_DK:(g + 1) * PEER_DK], preferred_element_type=F32)


def _peer_scores(ht, wq_t, keys, tn):
    d, n = ht.shape
    groups = 2 * PEER_HEADS
    return pl.pallas_call(
        _peer_scores_kernel,
        grid=(n // tn,),
        in_specs=[pl.BlockSpec((d, tn), lambda t: (0, t)),
                  pl.BlockSpec(wq_t.shape, lambda t: (0, 0)),
                  pl.BlockSpec(keys.shape, lambda t: (0, 0, 0))],
        out_specs=pl.BlockSpec((groups, PEER_NKEYS, tn), lambda t: (0, 0, t)),
        out_shape=jax.ShapeDtypeStruct((groups, PEER_NKEYS, n), F32),
        compiler_params=_cparams(1),
        name="peer_scores",
    )(ht, wq_t, keys)


PEER_RANKS = PEER_TOPK + 1


PEER_UNRANKED = float(PEER_NKEYS - 1)


def _top_rows(s, count):
    rows = []
    rank = jnp.full(s.shape, PEER_UNRANKED, F32)
    for r in range(count):
        m = jnp.max(s, axis=0, keepdims=True)
        rows.append(m)
        hit = s == m
        rank = jnp.where(hit, float(r), rank)
        s = jnp.where(hit, -jnp.inf, s)
    return rows, rank


def _peer_route_kernel(s_ref, rank2_ref, e2_ref, n1_ref, c2_ref, cand_ref):
    cand_ref[...] = jnp.full(cand_ref.shape, -jnp.inf, F32)
    for h in range(PEER_HEADS):
        s1 = s_ref[2 * h]
        s2 = s_ref[2 * h + 1]
        a, _ = _top_rows(s1, PEER_RANKS)
        b, rank2 = _top_rows(s2, PEER_RANKS)
        pairs = [(p, q) for p in range(PEER_RANKS) for q in range(PEER_RANKS // (p + 1))]
        for slot, (p, q) in enumerate(pairs):
            cand_ref[slot:slot + 1, :] = a[p] + b[q]
        top, _ = _top_rows(cand_ref[...], PEER_RANKS)
        threshold = 0.5 * (top[PEER_TOPK - 1] + top[PEER_TOPK])
        z = sum(jnp.exp(row - top[0]) for row in top[:PEER_TOPK])
        need = threshold - s1
        n1 = jnp.zeros_like(s1)
        for q in range(PEER_RANKS):
            n1 = n1 + jnp.where(b[q] >= need, 1.0, 0.0)
        rank2_ref[h] = rank2.astype(BF16)
        e2_ref[h] = jnp.exp(s2 - b[0]).astype(BF16)
        n1_ref[h] = n1
        c2_ref[h] = jnp.exp(s1 - a[0]) / z


PEER_CAND_ROWS = 56


def _peer_route(scores, tn):
    groups, nk, n = scores.shape
    spec = pl.BlockSpec((PEER_HEADS, nk, tn), lambda t: (0, 0, t))
    shape = lambda dt: jax.ShapeDtypeStruct((PEER_HEADS, nk, n), dt)
    return pl.pallas_call(
        _peer_route_kernel,
        grid=(n // tn,),
        in_specs=[pl.BlockSpec((groups, nk, tn), lambda t: (0, 0, t))],
        out_specs=[spec] * 4,
        out_shape=[shape(BF16), shape(BF16), shape(F32), shape(F32)],
        scratch_shapes=[pltpu.VMEM((PEER_CAND_ROWS, tn), F32)],
        compiler_params=_cparams(1),
        name="peer_route",
    )(scores)


def _gelu_tanh(x):
    k = -2.0 * math.sqrt(2.0 / math.pi)
    return x / (1.0 + jnp.exp(x * (k + (k * 0.044715) * (x * x))))


def _peer_experts_kernel(blocks, ht_ref, u_ref, vt_ref, rank2_ref, e2_ref, n1_ref, c2_ref, o_ref, w_ref):
    e = pl.program_id(1)

    @pl.when(e == 0)
    def _():
        o_ref[...] = jnp.zeros_like(o_ref)

    ht = ht_ref[...]
    for ii in range(blocks):
        i = e * blocks + ii
        rows = slice(ii * PEER_NKEYS, (ii + 1) * PEER_NKEYS)
        act = _gelu_tanh(jnp.dot(u_ref[rows, :], ht, preferred_element_type=F32).astype(BF16))
        gate = jnp.zeros(act.shape, BF16)
        for h in range(PEER_HEADS):
            n1 = n1_ref[h, pl.ds(i, 1), :].astype(BF16)
            c2 = c2_ref[h, pl.ds(i, 1), :].astype(BF16)
            gate = gate + jnp.where(rank2_ref[h] < n1, e2_ref[h] * c2, jnp.zeros((), BF16))
        w_ref[rows, :] = gate * act
    o_ref[...] += jnp.dot(vt_ref[...], w_ref[...], preferred_element_type=F32)


def _peer_experts(ht, u, vt, rank2, e2, n1, c2, tn, ec):
    d, n = ht.shape
    blocks = ec // PEER_NKEYS
    route = pl.BlockSpec((PEER_HEADS, PEER_NKEYS, tn), lambda t, e: (0, 0, t))
    return pl.pallas_call(
        functools.partial(_peer_experts_kernel, blocks),
        grid=(n // tn, PEER_EXPERTS // ec),
        in_specs=[pl.BlockSpec((d, tn), lambda t, e: (0, t)),
                  pl.BlockSpec((ec, d), lambda t, e: (e, 0)),
                  pl.BlockSpec((d, ec), lambda t, e: (0, e)),
                  route, route, route, route],
        out_specs=pl.BlockSpec((d, tn), lambda t, e: (0, t)),
        out_shape=jax.ShapeDtypeStruct((d, n), F32),
        scratch_shapes=[pltpu.VMEM((ec, tn), BF16)],
        compiler_params=_cparams(2),
        name="peer_experts",
    )(ht, u, vt, rank2, e2, n1, c2)


def _peer(h, w_q, keys, u_tab, v_tab):
    b, s, d = h.shape
    n = b * s
    tn = min(n, 512)
    ht = h.reshape(n, d).T
    wq_t = w_q.T.astype(BF16)
    kb = keys.reshape(2 * PEER_HEADS, PEER_NKEYS, PEER_DK).astype(BF16)
    scores = _peer_scores(ht, wq_t, kb, tn)
    rank2, e2, n1, c2 = _peer_route(scores, min(n, 256))
    out_t = _peer_experts(ht, u_tab.astype(BF16), v_tab.T.astype(BF16), rank2, e2, n1, c2, tn, 2048)
    return out_t.T.reshape(b, s, d)


def _rope_tables(t, n_ctx):
    rows = t // GRID_W
    row = np.repeat(np.arange(rows), GRID_W).astype(np.float32)
    col = np.tile(np.arange(GRID_W), rows).astype(np.float32)
    inv_freq = jnp.asarray(ROPE_THETA, F32) ** (-jnp.arange(ROPE_AXIS_FREQS, dtype=F32) / ROPE_AXIS_FREQS)
    ang = jnp.concatenate([jnp.asarray(row)[:, None] * inv_freq] * 2
                          + [jnp.asarray(col)[:, None] * inv_freq] * 2, axis=1)
    pad = lambda a, fill: jnp.concatenate([jnp.full((n_ctx, MLA_ROPE), fill, F32), a], axis=0)
    return pad(jnp.cos(ang), 1.0), pad(jnp.sin(ang), 0.0)


def _split_w_in(w_in):
    ml_cols = 4 * ML_W + ML_GATE_COLS
    w_ml = jnp.pad(w_in[:, :ml_cols], ((0, 0), (0, ML_PAD - ml_cols)))
    w_rw = w_in[:, ml_cols:ml_cols + RW_COLS]
    w_at = w_in[:, ml_cols + RW_COLS:]
    k_rope = w_at[:, MLA_Q_RANK + MLA_KV_RANK:]
    w_at = jnp.concatenate([w_at, _rot_cols(k_rope)], axis=1)
    w_at = jnp.pad(w_at, ((0, 0), (0, MLA_PAD - w_at.shape[1])))
    return w_ml.astype(BF16), w_rw.astype(BF16), w_at.astype(BF16)


def kernel(x, c, ctx, c_ctx, w_mod, b_mod, w_in, ml_conv_w, ml_conv_b, ml_i_bias, ml_f_bias, ml_norm_g,
           rw_mu, rw_w0, rw_w_up, rw_a0, rw_a_up, rw_g_up, rw_k_k, rw_k_a, rw_r_k, rw_gn_g, rw_gn_b,
           mla_q_norm_g, mla_q_up, mla_kv_norm_g, mla_kv_up, w_out, ln_mix_g, ln_mix_b,
           peer_w_q, peer_keys, peer_u, peer_v, ln_ffn_g, ln_ffn_b):
    b, t, d = x.shape
    n_ctx = ctx.shape[1]
    assert n_ctx % ROW_TILE == 0 and t % ROW_TILE == 0 and d == D_MODEL
    n_ctx_tiles = n_ctx // ROW_TILE
    cos, sin = _rope_tables(t, n_ctx)
    mod_rows = -(-(b + 1) // SUBLANES) * SUBLANES
    cvec = jnp.concatenate([c, c_ctx[None], jnp.zeros((mod_rows - b - 1, d), F32)], axis=0)

    x_all = jnp.concatenate([ctx, x], axis=1)
    depth = w_mod.shape[0]
    for l in range(depth):
        need_ctx = l < depth - 1
        mod = _modulation(cvec, w_mod[l], b_mod[l]).reshape(mod_rows, 6, d)
        mods = jnp.stack([jnp.broadcast_to(mod[b], (b, 6, d)), mod[:b]], axis=1)

        pml, prw, pat = _in_proj(x_all, mods, *_split_w_in(w_in[l]), n_ctx_tiles)

        q, k, lg = _ml_prep(pml, ml_conv_w[l], ml_conv_b[l], ml_i_bias[l], ml_f_bias[l], n_ctx_tiles)
        hf, hb = _ml_scan(q, k, pml, lg, n_ctx)

        r, kh, v, g, bonus, w, ab, kt = _rw_prep(
            prw, rw_mu[l], rw_w0[l], rw_w_up[l], rw_a0[l], rw_a_up[l], rw_g_up[l], rw_k_k[l], rw_k_a[l],
            rw_r_k[l], n_ctx_tiles)
        yf, yb = _rw_scan(r, kh, v, w, ab, kt, n_ctx)

        qn, qr, kn, va, kr = _mla_prep(pat, mla_q_norm_g[l], mla_q_up[l], mla_kv_norm_g[l], mla_kv_up[l],
                                       cos, sin)
        at_ctx, at_lat = _mla(qn, qr, kn, va, kr, n_ctx, need_ctx)
        first_tile = 0 if need_ctx else n_ctx_tiles
        y_at = jnp.concatenate([at_ctx, at_lat], axis=1) if need_ctx else jnp.pad(
            at_lat, ((0, 0), (n_ctx, 0), (0, 0)))

        x_mid, h_ffn = _out_proj(x_all, mods, pml, hf, hb, yf, yb, bonus, g, y_at, ml_norm_g[l], rw_gn_g[l],
                                 rw_gn_b[l], w_out[l], ln_mix_g[l], ln_mix_b[l], n_ctx_tiles, first_tile)
        y = _peer(h_ffn, peer_w_q[l], peer_keys[l], peer_u[l], peer_v[l])
        x_all = _final_norm(x_mid, y, mods, ln_ffn_g[l], ln_ffn_b[l], n_ctx_tiles, first_tile)
    return x_all
```

```python
import functools
import math

import jax
import jax.numpy as jnp
import numpy as np
from jax import lax
from jax.experimental import pallas as pl
from jax.experimental.pallas import tpu as pltpu

F32 = jnp.float32
BF16 = jnp.bfloat16
HIGHEST = lax.Precision.HIGHEST

LANES = 128
SUBLANES = 8
BF16_SUBLANES = 16

D_MODEL = 1024
DEPTH = 2
GRID_W = 64

ML_HEADS = 4
ML_DH = 64
ML_W = ML_HEADS * ML_DH
ML_CHUNK = 64
ML_M_INIT = -1e30
ML_GATE_COLS = 4 * ML_HEADS
ML_PAD = 4 * ML_W + LANES

RW_HEADS = 6
RW_DH = 64
RW_W = RW_HEADS * RW_DH
RW_PAIRS = RW_HEADS // 2
RW_LORA = 64
RW_G_LORA = 128
RW_DECAY_SCALE = math.exp(-0.5)
RW_GN_EPS = 64e-5
RW_COLS = 3 * RW_W + 4 * RW_LORA + RW_G_LORA
RW_CHUNK = 64

MLA_HEADS = 6
MLA_NOPE = 64
MLA_ROPE = 32
MLA_V = 64
MLA_Q_RANK = 384
MLA_KV_RANK = 256
MLA_W = MLA_HEADS * MLA_V
MLA_SCALE = (MLA_NOPE + MLA_ROPE) ** -0.5
MLA_PAD = 768
ROPE_AXIS_FREQS = MLA_ROPE // 4
ROPE_THETA = 10000.0

PEER_HEADS = 8
PEER_NKEYS = 128
PEER_EXPERTS = PEER_NKEYS * PEER_NKEYS
PEER_DK = 128
PEER_TOPK = 16

DEEPNORM_ALPHA = (2 * DEPTH) ** 0.25
LN_EPS = 1e-6

ROW_TILE = 256
VMEM_LIMIT = 56 * 1024 * 1024


def _cparams(n_axes):
    return pltpu.CompilerParams(dimension_semantics=("arbitrary",) * n_axes,
                                vmem_limit_bytes=VMEM_LIMIT)


def _ln_rows(x, eps=LN_EPS):
    mu = jnp.mean(x, axis=-1, keepdims=True)
    xc = x - mu
    var = jnp.mean(xc * xc, axis=-1, keepdims=True)
    return xc * lax.rsqrt(var + eps)


def _sigmoid(x):
    return 1.0 / (1.0 + jnp.exp(-x))


def _group_sum(x, gmat):
    hi = x.astype(BF16)
    lo = (x - hi.astype(F32)).astype(BF16)
    return (jnp.dot(hi, gmat, preferred_element_type=F32) + jnp.dot(lo, gmat, preferred_element_type=F32))


def _group_ln(x, gmat, eps):
    xc = x - _group_sum(x, gmat)
    return xc * lax.rsqrt(_group_sum(xc * xc, gmat) + eps)


def _group_matrix(width, group, value):
    idx = np.arange(width) // group
    return jnp.asarray((idx[:, None] == idx[None, :]).astype(np.float32) * value).astype(BF16)


def _mod_kernel(c_ref, w_ref, b_ref, o_ref):
    c = c_ref[...]
    s = c * _sigmoid(c)
    o_ref[...] = jnp.dot(s, w_ref[...], precision=HIGHEST, preferred_element_type=F32) + b_ref[...]


def _modulation(cvec, w_mod, b_mod):
    rows, d = cvec.shape
    n = w_mod.shape[1]
    tn = 1536
    return pl.pallas_call(
        _mod_kernel,
        grid=(n // tn,),
        in_specs=[pl.BlockSpec((rows, d), lambda j: (0, 0)),
                  pl.BlockSpec((d, tn), lambda j: (0, j)),
                  pl.BlockSpec((1, tn), lambda j: (0, j))],
        out_specs=pl.BlockSpec((rows, tn), lambda j: (0, j)),
        out_shape=jax.ShapeDtypeStruct((rows, n), F32),
        compiler_params=_cparams(1),
        name="modulation",
    )(cvec, w_mod, b_mod.reshape(1, n))


def _in_proj_kernel(x_ref, mod_ref, wml_ref, wrw_ref, wat_ref, pml_ref, prw_ref, pat_ref):
    mod = mod_ref[0, 0]
    h = _ln_rows(x_ref[0]) * (1.0 + mod[1:2]) + mod[0:1]
    hb = h.astype(BF16)
    pml_ref[0] = jnp.dot(hb, wml_ref[...], preferred_element_type=F32)
    prw_ref[0] = jnp.dot(hb, wrw_ref[...], preferred_element_type=F32)
    pat_ref[0] = jnp.dot(hb, wat_ref[...], preferred_element_type=F32)


def _in_proj(x_all, mods, wml, wrw, wat, n_ctx_tiles):
    b, s, d = x_all.shape
    tm = ROW_TILE
    seg = lambda i: (i >= n_ctx_tiles).astype(jnp.int32)
    full = lambda w: pl.BlockSpec(w.shape, lambda bi, i: (0, 0))
    out = lambda w: pl.BlockSpec((1, tm, w.shape[1]), lambda bi, i: (bi, i, 0))
    return pl.pallas_call(
        _in_proj_kernel,
        grid=(b, s // tm),
        in_specs=[pl.BlockSpec((1, tm, d), lambda bi, i: (bi, i, 0)),
                  pl.BlockSpec((1, 1, 6, d), lambda bi, i: (bi, seg(i), 0, 0)),
                  full(wml), full(wrw), full(wat)],
        out_specs=[out(wml), out(wrw), out(wat)],
        out_shape=[jax.ShapeDtypeStruct((b, s, w.shape[1]), F32) for w in (wml, wrw, wat)],
        compiler_params=_cparams(2),
        name="in_proj",
    )(x_all, mods, wml, wrw, wat)


def _halo_specs(width, col_block, tm, s):
    per = tm // SUBLANES
    last = s // SUBLANES - 1
    prev = pl.BlockSpec((1, SUBLANES, width),
                        lambda bi, i: (bi, jnp.maximum(i * per - 1, 0), col_block))
    nxt = pl.BlockSpec((1, SUBLANES, width),
                       lambda bi, i: (bi, jnp.minimum((i + 1) * per, last), col_block))
    return prev, nxt


def _neighbours(x, prev_blk, next_blk, tile, n_ctx_tiles, n_tiles):
    tm = x.shape[0]
    row = lax.broadcasted_iota(jnp.int32, x.shape, 0)
    starts = jnp.logical_or(tile == 0, tile == n_ctx_tiles)
    ends = jnp.logical_or(tile == n_ctx_tiles - 1, tile == n_tiles - 1)
    prev_row = jnp.where(starts, 0.0, prev_blk[SUBLANES - 1:SUBLANES, :])
    next_row = jnp.where(ends, 0.0, next_blk[0:1, :])
    prev = jnp.where(row == 0, prev_row, pltpu.roll(x, 1, 0))
    nxt = jnp.where(row == tm - 1, next_row, pltpu.roll(x, tm - 1, 0))
    return prev, nxt


def _ml_prep_kernel(n_ctx_tiles, n_tiles, qk_ref, prev_ref, next_ref, gate_ref, cw_ref, cb_ref,
                    gb_ref, fmask_ref, q_ref, k_ref, lg_ref):
    tile = pl.program_id(1)
    x = qk_ref[0]
    prev, nxt = _neighbours(x, prev_ref[0], next_ref[0], tile, n_ctx_tiles, n_tiles)
    cw = cw_ref[...]
    z = prev * cw[0:1] + x * cw[1:2] + nxt * cw[2:3] + cb_ref[...]
    z = z * _sigmoid(z)
    q_ref[0] = z[:, :ML_W] * (ML_DH ** -0.5)
    k_ref[0] = z[:, ML_W:]
    g = gate_ref[0] + gb_ref[...]
    log_sig = jnp.minimum(g, 0.0) - jnp.log1p(jnp.exp(-jnp.abs(g)))
    lg_ref[0] = jnp.where(fmask_ref[...] > 0.5, log_sig, g)


def _ml_prep(pml, conv_w, conv_b, i_bias, f_bias, n_ctx_tiles):
    b, s, _ = pml.shape
    tm = ROW_TILE
    n_tiles = s // tm
    gate_bias = jnp.stack([i_bias, f_bias], axis=1).reshape(1, ML_GATE_COLS)
    gate_bias = jnp.pad(gate_bias, ((0, 0), (0, LANES - ML_GATE_COLS)))
    fmask = np.zeros((2, 2, ML_HEADS), np.float32)
    fmask[:, 1] = 1.0
    fmask = jnp.asarray(np.pad(fmask.reshape(1, -1), ((0, 0), (0, LANES - ML_GATE_COLS))))
    prev, nxt = _halo_specs(2 * ML_W, 0, tm, s)
    small = lambda a: pl.BlockSpec(a.shape, lambda bi, i: (0, 0))
    cb = conv_b.reshape(1, -1)
    return pl.pallas_call(
        functools.partial(_ml_prep_kernel, n_ctx_tiles, n_tiles),
        grid=(b, n_tiles),
        in_specs=[pl.BlockSpec((1, tm, 2 * ML_W), lambda bi, i: (bi, i, 0)), prev, nxt,
                  pl.BlockSpec((1, tm, LANES), lambda bi, i: (bi, i, 4 * ML_W // LANES)),
                  small(conv_w), small(cb), small(gate_bias), small(fmask)],
        out_specs=[pl.BlockSpec((1, tm, ML_W), lambda bi, i: (bi, i, 0)),
                   pl.BlockSpec((1, tm, ML_W), lambda bi, i: (bi, i, 0)),
                   pl.BlockSpec((1, tm, LANES), lambda bi, i: (bi, i, 0))],
        out_shape=[jax.ShapeDtypeStruct((b, s, ML_W), F32),
                   jax.ShapeDtypeStruct((b, s, ML_W), F32),
                   jax.ShapeDtypeStruct((b, s, LANES), F32)],
        compiler_params=_cparams(2),
        name="mlstm_prep",
    )(pml, pml, pml, pml, conv_w, cb, gate_bias, fmask)


def _ml_scan_kernel(n_chunks, n_ctx_chunks, q_ref, k_ref, v_ref, lg_ref, hf_ref, hb_ref,
                    ct_ref, n_ref, m_ref):
    L = ML_CHUNK
    units = 2 * ML_HEADS
    per_tile = LANES // ML_DH
    row = lax.broadcasted_iota(jnp.int32, (units, L, L), 1)
    col = lax.broadcasted_iota(jnp.int32, (units, L, L), 2)
    sign = jnp.where(lax.broadcasted_iota(jnp.int32, (units, L, L), 0) < ML_HEADS, 1, -1)
    eye = row == col
    seen = (col - row) * sign <= 0
    seen_t = (row - col) * sign <= 0
    h_refs = (hf_ref, hb_ref)

    def bmm(a, b):
        return jnp.einsum("uij,ujk->uik", a.astype(BF16), b.astype(BF16), preferred_element_type=F32)

    def chunk(c, _):
        back = jnp.where(c < n_ctx_chunks, n_ctx_chunks - 1 - c, n_ctx_chunks + n_chunks - 1 - c)
        rows = [pl.ds(pl.multiple_of(cc * L, L), L) for cc in (c, back)]
        q, k, kt, v, i_col, f_col = [], [], [], [], [], []
        for d in range(2):
            lg = lg_ref[0, rows[d], :]
            for tile in range(ML_W // LANES):
                lanes = slice(tile * LANES, (tile + 1) * LANES)
                q2, k2, v2 = q_ref[0, rows[d], lanes], k_ref[0, rows[d], lanes], v_ref[0, rows[d], lanes]
                kt2 = k2.T
                for hh in range(per_tile):
                    head = tile * per_tile + hh
                    sub = slice(hh * ML_DH, (hh + 1) * ML_DH)
                    gate = d * 2 * ML_HEADS + head
                    q.append(q2[:, sub]); k.append(k2[:, sub]); v.append(v2[:, sub]); kt.append(kt2[sub, :])
                    i_col.append(lg[:, gate:gate + 1])
                    f_col.append(lg[:, gate + ML_HEADS:gate + ML_HEADS + 1])
        q, k, kt, v, i_col, f_col = (jnp.stack(a, axis=0) for a in (q, k, kt, v, i_col, f_col))
        ct, n, m = ct_ref[...], n_ref[...], m_ref[...]

        bcum_row = jnp.sum(jnp.where(seen_t, f_col, 0.0), axis=1, keepdims=True)
        bcum_col = jnp.sum(jnp.where(eye, bcum_row, 0.0), axis=2, keepdims=True)
        i_row = jnp.sum(jnp.where(eye, i_col, 0.0), axis=1, keepdims=True)
        dmat = jnp.where(seen, bcum_col - bcum_row + i_row, -jnp.inf)
        rmax = jnp.max(dmat, axis=2, keepdims=True)
        sx = bmm(q, kt) * jnp.exp(dmat - rmax)
        sv = bmm(sx, v)
        rs = jnp.sum(sx, axis=2, keepdims=True)
        b_end = jnp.concatenate([bcum_col[:ML_HEADS, L - 1:L, :], bcum_col[ML_HEADS:, 0:1, :]], axis=0)
        g = b_end - bcum_col + i_col
        gmax = jnp.max(g, axis=1, keepdims=True)
        wkx = jnp.exp(g - gmax)
        kv = bmm(kt, wkx * v)
        nx = jnp.sum(wkx * k, axis=1, keepdims=True)
        a_inter = bcum_col + m
        m_t = jnp.maximum(a_inter, rmax)
        w_inter = jnp.exp(a_inter - m_t)
        scale = jnp.exp(rmax - m_t)
        num = w_inter * bmm(q, ct) + scale * sv
        den = w_inter * jnp.sum(q * n, axis=2, keepdims=True) + scale * rs
        h = num / jnp.maximum(jnp.abs(den), jnp.exp(-m_t))
        m_new = jnp.maximum(b_end + m, gmax)
        decay = jnp.exp(b_end + m - m_new)
        carry_scale = jnp.exp(gmax - m_new)
        ct_ref[...] = decay * ct + carry_scale * kv
        n_ref[...] = decay * n + carry_scale * nx
        m_ref[...] = m_new
        for d in range(2):
            for tile in range(ML_W // LANES):
                first = d * ML_HEADS + tile * per_tile
                h_refs[d][0, rows[d], tile * LANES:(tile + 1) * LANES] = jnp.concatenate(
                    [h[first + hh] for hh in range(per_tile)], axis=1)
        return 0

    ct_ref[...] = jnp.zeros_like(ct_ref)
    n_ref[...] = jnp.zeros_like(n_ref)
    m_ref[...] = jnp.full(m_ref.shape, ML_M_INIT, F32)
    lax.fori_loop(0, n_chunks, chunk, 0)


def _ml_scan(q, k, pml, lg, n_ctx):
    b, s, _ = q.shape
    nc = s // ML_CHUNK
    wide = lambda blk: pl.BlockSpec((1, s, ML_W), lambda bi: (bi, 0, blk))
    return pl.pallas_call(
        functools.partial(_ml_scan_kernel, nc, n_ctx // ML_CHUNK),
        grid=(b,),
        in_specs=[wide(0), wide(0), wide(2), pl.BlockSpec((1, s, LANES), lambda bi: (bi, 0, 0))],
        out_specs=[wide(0), wide(0)],
        out_shape=[jax.ShapeDtypeStruct((b, s, ML_W), F32)] * 2,
        scratch_shapes=[pltpu.VMEM((2 * ML_HEADS, ML_DH, ML_DH), F32),
                        pltpu.VMEM((2 * ML_HEADS, 1, ML_DH), F32),
                        pltpu.VMEM((2 * ML_HEADS, 1, 1), F32)],
        compiler_params=_cparams(1),
        name="mlstm_scan",
    )(q, k, pml, lg)


def _rw_prep_kernel(n_ctx_tiles, n_tiles, p_ref, prev_ref, next_ref, mu_ref, wup_ref, aup_ref,
                    gup_ref, w0_ref, a0_ref, kk_ref, ka_ref, rho_ref, gsum_ref,
                    r_ref, kh_ref, v_ref, g_ref, bonus_ref, w_ref, ab_ref, kt_ref):
    tile = pl.program_id(1)
    p = p_ref[0]
    prev, nxt = _neighbours(p, prev_ref[0], next_ref[0], tile, n_ctx_tiles, n_tiles)
    mu = mu_ref[...]
    p = p + mu[0:1] * (prev - p) + mu[1:2] * (nxt - p)
    r = p[:, 0:RW_W]
    k = p[:, RW_W:2 * RW_W]
    v = p[:, 2 * RW_W:3 * RW_W]
    base = 3 * RW_W
    wd = p[:, base:base + 2 * RW_LORA]
    ad = p[:, base + 2 * RW_LORA:base + 4 * RW_LORA]
    gd = p[:, base + 4 * RW_LORA:base + 4 * RW_LORA + RW_G_LORA]
    dot = functools.partial(jnp.dot, preferred_element_type=F32)
    w = jnp.exp(-RW_DECAY_SCALE * _sigmoid(w0_ref[...] + dot(jnp.tanh(wd).astype(BF16), wup_ref[...])))
    a = _sigmoid(a0_ref[...] + dot(ad.astype(BF16), aup_ref[...]))
    g = dot(_sigmoid(gd).astype(BF16), gup_ref[...])
    kk = k * kk_ref[...]
    ss = _group_sum(kk * kk, gsum_ref[...])
    kh = kk * lax.rsqrt(ss + 1e-12)
    ka = ka_ref[...]
    rk = r * rho_ref[...]
    kt_sum = jnp.zeros_like(k)
    for dr in range(2):
        a_d = a[:, dr * RW_W:(dr + 1) * RW_W]
        kt_d = k * (1.0 + (a_d - 1.0) * ka)
        kt_ref[0, :, dr * RW_W:(dr + 1) * RW_W] = kt_d
        ab_ref[0, :, dr * RW_W:(dr + 1) * RW_W] = kh * a_d
        kt_sum = kt_sum + kt_d
    bonus_ref[0] = _group_sum(rk * kt_sum, gsum_ref[...]) * v
    r_ref[0] = r
    kh_ref[0] = kh
    v_ref[0] = v
    g_ref[0] = g
    w_ref[0] = w


def _block_diag2(up):
    z = jnp.zeros_like(up[0])
    return jnp.concatenate([jnp.concatenate([up[0], z], axis=1),
                            jnp.concatenate([z, up[1]], axis=1)], axis=0)


def _rw_prep(prw, mu, w0, w_up, a0, a_up, g_up, k_k, k_a, r_k, n_ctx_tiles):
    b, s, _ = prw.shape
    tm = ROW_TILE
    n_tiles = s // tm
    prev, nxt = _halo_specs(RW_COLS, 0, tm, s)
    small = lambda a: pl.BlockSpec(a.shape, lambda bi, i: (0, 0))
    consts = [mu, _block_diag2(w_up).astype(BF16), _block_diag2(a_up).astype(BF16),
              g_up.astype(BF16), w0.reshape(1, 2 * RW_W), a0.reshape(1, 2 * RW_W),
              k_k.reshape(1, RW_W), k_a.reshape(1, RW_W), r_k.reshape(1, RW_W),
              _group_matrix(RW_W, RW_DH, 1.0)]
    one = lambda width, dt=F32: (pl.BlockSpec((1, tm, width), lambda bi, i: (bi, i, 0)),
                                 jax.ShapeDtypeStruct((b, s, width), dt))
    outs = [one(RW_W)] * 5 + [one(2 * RW_W)] * 3
    return pl.pallas_call(
        functools.partial(_rw_prep_kernel, n_ctx_tiles, n_tiles),
        grid=(b, n_tiles),
        in_specs=[pl.BlockSpec((1, tm, RW_COLS), lambda bi, i: (bi, i, 0)), prev, nxt]
                 + [small(a) for a in consts],
        out_specs=[o[0] for o in outs],
        out_shape=[o[1] for o in outs],
        compiler_params=_cparams(2),
        name="rwkv_prep",
    )(prw, prw, prw, *consts)


def _rw_scan_kernel(batch, *refs):
    ins = (refs[0:6], refs[6:12])
    ones_ref = refs[12]
    y_refs = refs[13:15]
    state_ref, yacc_ref, xs_ref, z_ref, vcat_ref = refs[15:20]
    L = RW_CHUNK
    items = batch * RW_PAIRS
    per_dir = items * RW_DH

    def item_rows(d, bi, p):
        start = ((d * batch + bi) * RW_PAIRS + p) * RW_DH
        return slice(start, start + RW_DH)

    @pl.when(pl.program_id(0) == 0)
    def _():
        state_ref[...] = jnp.zeros_like(state_ref)

    yacc_ref[...] = jnp.zeros_like(yacc_ref)
    for d in range(2):
        for bi in range(batch):
            for p in range(RW_PAIRS):
                vt = ins[d][5][bi, :, p * LANES:(p + 1) * LANES].T
                vcat_ref[item_rows(d, bi, p), :] = jnp.concatenate([vt[:RW_DH], vt[RW_DH:]], axis=1)
    lane = lax.broadcasted_iota(jnp.int32, (RW_DH, LANES), 1)
    step_of_lane = jnp.where(lane < RW_DH, lane, lane - RW_DH)
    sel_row = lax.broadcasted_iota(jnp.int32, (LANES, LANES), 0)
    sel_col = lax.broadcasted_iota(jnp.int32, (LANES, LANES), 1)
    same_head = (sel_row < RW_DH) == (sel_col < RW_DH)
    sel_step = jnp.where(sel_row < RW_DH, sel_row, sel_row - RW_DH)

    def step(j, _):
        steps = (j, L - 1 - j)
        rows_t = [[[ref[bi, pl.ds(steps[d], 1), :] for bi in range(batch)] for ref in ins[d][:5]]
                  for d in range(2)]
        vec = lambda rows, bi, p: jnp.broadcast_to(rows[bi][:, p * LANES:(p + 1) * LANES], (RW_DH, LANES))

        for d in range(2):
            kh_t = rows_t[d][3]
            for bi in range(batch):
                for p in range(RW_PAIRS):
                    rows = item_rows(d, bi, p)
                    xs_ref[rows, :] = (state_ref[rows, :] * vec(kh_t, bi, p)).astype(BF16)
        sa = jnp.dot(xs_ref[...], ones_ref[...], preferred_element_type=F32)
        pick = [jnp.where(lane < RW_DH, steps[d], steps[d] + RW_DH) for d in range(2)]

        for d in range(2):
            w_t, ab_t, kt_t, _, r_t = rows_t[d]
            for bi in range(batch):
                for p in range(RW_PAIRS):
                    rows = item_rows(d, bi, p)
                    local = slice(rows.start - d * per_dir, rows.stop - d * per_dir)
                    vb = jnp.take_along_axis(vcat_ref[rows, :], pick[d], axis=1)
                    st = (state_ref[rows, :] * vec(w_t, bi, p) - sa[rows] * vec(ab_t, bi, p)
                          + vb * vec(kt_t, bi, p))
                    state_ref[rows, :] = st
                    z_ref[rows, :] = (st * vec(r_t, bi, p)).astype(BF16)
        y = jnp.dot(z_ref[...], ones_ref[...], preferred_element_type=F32)
        for d in range(2):
            here = step_of_lane == steps[d]
            for bi in range(batch):
                for p in range(RW_PAIRS):
                    rows = item_rows(d, bi, p)
                    yacc_ref[rows, :] = jnp.where(here, y[rows], yacc_ref[rows, :])
        return 0

    lax.fori_loop(0, L, step, 0)
    for d in range(2):
        for bi in range(batch):
            for p in range(RW_PAIRS):
                ya = yacc_ref[item_rows(d, bi, p), :]
                ya = jnp.concatenate([ya[:, :RW_DH], ya[:, RW_DH:]], axis=0)
                y_refs[d][bi, :, p * LANES:(p + 1) * LANES] = ya.T


def _rw_scan(r, kh, v, w, ab, kt, n_ctx):
    b, s, _ = r.shape
    L = RW_CHUNK
    nc = s // L
    ncc = n_ctx // L
    rows = b * RW_PAIRS * RW_DH
    cmap = (lambda c: c,
            lambda c: jnp.where(c < ncc, ncc - 1 - c, ncc + nc - 1 - c))
    head_ones = _group_matrix(LANES, RW_DH, 1.0)
    in_specs, args = [], []
    for d in range(2):
        cm = cmap[d]
        per_dir = lambda c, cm=cm, d=d: (0, cm(c), d)
        shared = lambda c, cm=cm: (0, cm(c), 0)
        in_specs += [pl.BlockSpec((b, L, RW_W), per_dir)] * 3
        in_specs += [pl.BlockSpec((b, L, RW_W), shared)] * 3
        args += [w, ab, kt, kh, r, v]
    in_specs.append(pl.BlockSpec(head_ones.shape, lambda c: (0, 0)))
    args.append(head_ones)
    out_specs = [pl.BlockSpec((b, L, RW_W), lambda c, cm=cm: (0, cm(c), 0)) for cm in cmap]
    return pl.pallas_call(
        functools.partial(_rw_scan_kernel, b),
        grid=(nc,),
        in_specs=in_specs,
        out_specs=out_specs,
        out_shape=[jax.ShapeDtypeStruct((b, s, RW_W), F32)] * 2,
        scratch_shapes=[pltpu.VMEM((2 * rows, LANES), F32), pltpu.VMEM((2 * rows, LANES), F32),
                        pltpu.VMEM((2 * rows, LANES), BF16), pltpu.VMEM((2 * rows, LANES), BF16),
                        pltpu.VMEM((2 * rows, LANES), F32)],
        compiler_params=_cparams(1),
        name="rwkv_scan",
    )(*args)


def _mla_prep_kernel(p_ref, cosq_ref, sinq_ref, cosk_ref, sink_ref, qg_ref, kvg_ref, wq_ref, wkv_ref,
                     qn_ref, qr_ref, kn_ref, v_ref, kr_ref):
    p = p_ref[0]
    cq = p[:, :MLA_Q_RANK]
    ckv = p[:, MLA_Q_RANK:MLA_Q_RANK + MLA_KV_RANK]
    base = MLA_Q_RANK + MLA_KV_RANK
    k_rope = p[:, base:base + MLA_ROPE]
    k_rot = p[:, base + MLA_ROPE:base + 2 * MLA_ROPE]
    rms = lambda x, g: x * lax.rsqrt(jnp.mean(x * x, axis=-1, keepdims=True) + LN_EPS) * g
    q = jnp.dot(rms(cq, qg_ref[...]).astype(BF16), wq_ref[...], preferred_element_type=F32)
    kv = jnp.dot(rms(ckv, kvg_ref[...]).astype(BF16), wkv_ref[...], preferred_element_type=F32)
    nope = MLA_HEADS * MLA_NOPE
    rope = MLA_HEADS * MLA_ROPE
    q = q * (MLA_SCALE * math.log2(math.e))
    qn_ref[0] = q[:, :nope]
    qr_ref[0] = q[:, nope:nope + rope] * cosq_ref[...] + q[:, nope + rope:nope + 2 * rope] * sinq_ref[...]
    kn_ref[0] = kv[:, :nope]
    v_ref[0] = kv[:, nope:]
    kr_ref[0] = k_rope * cosk_ref[...] + k_rot * sink_ref[...]


def _rot_cols(w):
    shape = w.shape
    w = w.reshape(shape[:-1] + (shape[-1] // 16, 2, 8))
    return jnp.concatenate([-w[..., 1:2, :], w[..., 0:1, :]], axis=-2).reshape(shape)


def _mla_prep(pat, q_norm_g, q_up, kv_norm_g, kv_up, cos, sin):
    b, s, _ = pat.shape
    tm = ROW_TILE
    qw = q_up.reshape(MLA_Q_RANK, MLA_HEADS, MLA_NOPE + MLA_ROPE)
    q_nope = qw[:, :, :MLA_NOPE].reshape(MLA_Q_RANK, -1)
    q_rope = qw[:, :, MLA_NOPE:]
    wq = jnp.concatenate([q_nope, q_rope.reshape(MLA_Q_RANK, -1),
                          _rot_cols(q_rope).reshape(MLA_Q_RANK, -1)], axis=1).astype(BF16)
    kvw = kv_up.reshape(MLA_KV_RANK, MLA_HEADS, MLA_NOPE + MLA_V)
    wkv = jnp.concatenate([kvw[:, :, :MLA_NOPE].reshape(MLA_KV_RANK, -1),
                           kvw[:, :, MLA_NOPE:].reshape(MLA_KV_RANK, -1)], axis=1).astype(BF16)
    cosq = jnp.tile(cos, (1, MLA_HEADS))
    sinq = jnp.tile(sin, (1, MLA_HEADS))
    rows = lambda width: pl.BlockSpec((tm, width), lambda bi, i: (i, 0))
    small = lambda a: pl.BlockSpec(a.shape, lambda bi, i: (0, 0))
    qg = q_norm_g.reshape(1, -1)
    kvg = kv_norm_g.reshape(1, -1)
    out = lambda width: (pl.BlockSpec((1, tm, width), lambda bi, i: (bi, i, 0)),
                         jax.ShapeDtypeStruct((b, s, width), F32))
    outs = [out(MLA_HEADS * MLA_NOPE), out(MLA_HEADS * MLA_ROPE), out(MLA_HEADS * MLA_NOPE),
            out(MLA_W), out(MLA_ROPE)]
    return pl.pallas_call(
        _mla_prep_kernel,
        grid=(b, s // tm),
        in_specs=[pl.BlockSpec((1, tm, MLA_PAD), lambda bi, i: (bi, i, 0)),
                  rows(MLA_HEADS * MLA_ROPE), rows(MLA_HEADS * MLA_ROPE), rows(MLA_ROPE), rows(MLA_ROPE),
                  small(qg), small(kvg), small(wq), small(wkv)],
        out_specs=[o[0] for o in outs],
        out_shape=[o[1] for o in outs],
        compiler_params=_cparams(2),
        name="mla_prep",
    )(pat, cosq, sinq, cos, sin, qg, kvg, wq, wkv)


ATTN_HEADS_PER_STEP = 3


def _attn_kernel(q_ref, k_ref, v_ref, o_ref):
    for h in range(ATTN_HEADS_PER_STEP):
        s = lax.dot_general(q_ref[0, h], k_ref[0, h], (((1,), (1,)), ((), ())),
                            preferred_element_type=F32)
        p = jnp.exp2(s - jnp.max(s, axis=-1, keepdims=True))
        denom = jnp.sum(p, axis=-1, keepdims=True)
        o_ref[0, h] = jnp.dot(p.astype(BF16), v_ref[0, h], preferred_element_type=F32) / denom


def _attention(q, k, v, tq):
    b, h, t, dq = q.shape
    sk = k.shape[2]
    dv = v.shape[3]
    hp = ATTN_HEADS_PER_STEP
    return pl.pallas_call(
        _attn_kernel,
        grid=(b, h // hp, t // tq),
        in_specs=[pl.BlockSpec((1, hp, tq, dq), lambda bi, hi, i: (bi, hi, i, 0)),
                  pl.BlockSpec((1, hp, sk, dq), lambda bi, hi, i: (bi, hi, 0, 0)),
                  pl.BlockSpec((1, hp, sk, dv), lambda bi, hi, i: (bi, hi, 0, 0))],
        out_specs=pl.BlockSpec((1, hp, tq, dv), lambda bi, hi, i: (bi, hi, i, 0)),
        out_shape=jax.ShapeDtypeStruct((b, h, t, dv), F32),
        compiler_params=_cparams(3),
        name="mla_attention",
    )(q, k, v)


def _mla(qn, qr, kn, v, kr, n_ctx, need_ctx):
    b, s, _ = qn.shape
    heads = lambda a, w: a.reshape(b, s, MLA_HEADS, w).transpose(0, 2, 1, 3)
    q = jnp.concatenate([heads(qn, MLA_NOPE), heads(qr, MLA_ROPE)], axis=-1).astype(BF16)
    k_rope = jnp.broadcast_to(kr[:, None], (b, MLA_HEADS, s, MLA_ROPE))
    k = jnp.concatenate([heads(kn, MLA_NOPE), k_rope], axis=-1).astype(BF16)
    vh = heads(v, MLA_V).astype(BF16)
    pad = ((0, 0), (0, 0), (0, 0), (0, LANES - MLA_NOPE - MLA_ROPE))
    q, k = jnp.pad(q, pad), jnp.pad(k, pad)
    merge = lambda a: a.transpose(0, 2, 1, 3).reshape(b, -1, MLA_W)
    t = s - n_ctx
    y_lat = merge(_attention(q[:, :, n_ctx:], k, vh, min(t, 512)))
    if not need_ctx:
        return None, y_lat
    y_ctx = merge(_attention(q[:, :, :n_ctx], k[:, :, :n_ctx], vh[:, :, :n_ctx], n_ctx))
    return y_ctx, y_lat


def _out_proj_kernel(x_ref, mod_ref, hf_ref, hb_ref, o_ref, yf_ref, yb_ref, bonus_ref, g_ref, at_ref,
                     g4_ref, g6_ref, mlg_ref, gng_ref, gnb_ref, wml_ref, wrw_ref, wat_ref,
                     lng_ref, lnb_ref, xo_ref, ho_ref):
    mod = mod_ref[0, 0]
    ml = _sigmoid(o_ref[0]) * _group_ln(hf_ref[0] + hb_ref[0], g4_ref[...], LN_EPS) * mlg_ref[...]
    z = _group_ln(yf_ref[0] + yb_ref[0], g6_ref[...], RW_GN_EPS) * gng_ref[...] + gnb_ref[...]
    rw = (z + bonus_ref[0]) * g_ref[0]
    dot = functools.partial(jnp.dot, preferred_element_type=F32)
    y = (dot(ml.astype(BF16), wml_ref[...]) + dot(rw.astype(BF16), wrw_ref[...])
         + dot(at_ref[0].astype(BF16), wat_ref[...]))
    x = _ln_rows(DEEPNORM_ALPHA * x_ref[0] + mod[2:3] * y) * lng_ref[...] + lnb_ref[...]
    xo_ref[0] = x
    ho_ref[0] = (_ln_rows(x) * (1.0 + mod[4:5]) + mod[3:4]).astype(BF16)


def _out_proj(x_all, mods, pml, hf, hb, yf, yb, bonus, g, y_at, norm_g, gn_g, gn_b, w_out,
              ln_g, ln_b, n_ctx_tiles, first_tile):
    b, s, d = x_all.shape
    tm = ROW_TILE
    n_tiles = s // tm - first_tile
    seg = lambda i: (i + first_tile >= n_ctx_tiles).astype(jnp.int32)
    rows = lambda width, blk=0: pl.BlockSpec((1, tm, width), lambda bi, i: (bi, i + first_tile, blk))
    small = lambda a: pl.BlockSpec(a.shape, lambda bi, i: (0, 0))
    wb = w_out.astype(BF16)
    consts = [_group_matrix(ML_W, ML_DH, 1.0 / ML_DH), _group_matrix(RW_W, RW_DH, 1.0 / RW_DH),
              norm_g.reshape(1, -1), gn_g.reshape(1, -1), gn_b.reshape(1, -1),
              wb[:ML_W], wb[ML_W:ML_W + RW_W], wb[ML_W + RW_W:],
              ln_g.reshape(1, -1), ln_b.reshape(1, -1)]
    out_rows = lambda: pl.BlockSpec((1, tm, d), lambda bi, i: (bi, i, 0))
    return pl.pallas_call(
        _out_proj_kernel,
        grid=(b, n_tiles),
        in_specs=[rows(d), pl.BlockSpec((1, 1, 6, d), lambda bi, i: (bi, seg(i), 0, 0)),
                  rows(ML_W), rows(ML_W), rows(ML_W, 3 * ML_W // ML_W),
                  rows(RW_W), rows(RW_W), rows(RW_W), rows(RW_W), rows(MLA_W)]
                 + [small(a) for a in consts],
        out_specs=[out_rows(), out_rows()],
        out_shape=[jax.ShapeDtypeStruct((b, n_tiles * tm, d), F32),
                   jax.ShapeDtypeStruct((b, n_tiles * tm, d), BF16)],
        compiler_params=_cparams(2),
        name="out_proj",
    )(x_all, mods, hf, hb, pml, yf, yb, bonus, g, y_at, *consts)


def _final_norm_kernel(x_ref, y_ref, mod_ref, g_ref, b_ref, o_ref):
    mod = mod_ref[0, 0]
    o_ref[0] = _ln_rows(DEEPNORM_ALPHA * x_ref[0] + mod[5:6] * y_ref[0]) * g_ref[...] + b_ref[...]


def _final_norm(x, y, mods, ln_g, ln_b, n_ctx_tiles, first_tile):
    b, s, d = x.shape
    tm = ROW_TILE
    seg = lambda i: (i + first_tile >= n_ctx_tiles).astype(jnp.int32)
    rows = pl.BlockSpec((1, tm, d), lambda bi, i: (bi, i, 0))
    small = pl.BlockSpec((1, d), lambda bi, i: (0, 0))
    return pl.pallas_call(
        _final_norm_kernel,
        grid=(b, s // tm),
        in_specs=[rows, rows, pl.BlockSpec((1, 1, 6, d), lambda bi, i: (bi, seg(i), 0, 0)), small, small],
        out_specs=rows,
        out_shape=jax.ShapeDtypeStruct((b, s, d), F32),
        compiler_params=_cparams(2),
        name="final_norm",
    )(x, y, mods, ln_g.reshape(1, d), ln_b.reshape(1, d))


def _peer_scores_kernel(ht_ref, wq_ref, keys_ref, s_ref):
    q = jnp.dot(wq_ref[...], ht_ref[...], preferred_element_type=F32).astype(BF16)
    for g in range(2 * PEER_HEADS):
        s_ref[g] = jnp.dot(keys_ref[g], q[g * PEER_DK:(g + 1) * PEER_DK], preferred_element_type=F32)


def _peer_scores(ht, wq_t, keys, tn):
    d, n = ht.shape
    groups = 2 * PEER_HEADS
    return pl.pallas_call(
        _peer_scores_kernel,
        grid=(n // tn,),
        in_specs=[pl.BlockSpec((d, tn), lambda t: (0, t)),
                  pl.BlockSpec(wq_t.shape, lambda t: (0, 0)),
                  pl.BlockSpec(keys.shape, lambda t: (0, 0, 0))],
        out_specs=pl.BlockSpec((groups, PEER_NKEYS, tn), lambda t: (0, 0, t)),
        out_shape=jax.ShapeDtypeStruct((groups, PEER_NKEYS, n), F32),
        compiler_params=_cparams(1),
        name="peer_scores",
    )(ht, wq_t, keys)


PEER_RANKS = PEER_TOPK + 1


PEER_UNRANKED = float(PEER_NKEYS - 1)


def _top_rows(s, count):
    rows = []
    rank = jnp.full(s.shape, PEER_UNRANKED, F32)
    for r in range(count):
        m = jnp.max(s, axis=0, keepdims=True)
        rows.append(m)
        hit = s == m
        rank = jnp.where(hit, float(r), rank)
        s = jnp.where(hit, -jnp.inf, s)
    return rows, rank


def _peer_route_kernel(s_ref, rank2_ref, e2_ref, n1_ref, c2_ref, cand_ref):
    cand_ref[...] = jnp.full(cand_ref.shape, -jnp.inf, F32)
    for h in range(PEER_HEADS):
        s1 = s_ref[2 * h]
        s2 = s_ref[2 * h + 1]
        a, _ = _top_rows(s1, PEER_RANKS)
        b, rank2 = _top_rows(s2, PEER_RANKS)
        pairs = [(p, q) for p in range(PEER_RANKS) for q in range(PEER_RANKS // (p + 1))]
        for slot, (p, q) in enumerate(pairs):
            cand_ref[slot:slot + 1, :] = a[p] + b[q]
        top, _ = _top_rows(cand_ref[...], PEER_RANKS)
        threshold = 0.5 * (top[PEER_TOPK - 1] + top[PEER_TOPK])
        z = sum(jnp.exp(row - top[0]) for row in top[:PEER_TOPK])
        need = threshold - s1
        n1 = jnp.zeros_like(s1)
        for q in range(PEER_RANKS):
            n1 = n1 + jnp.where(b[q] >= need, 1.0, 0.0)
        rank2_ref[h] = rank2.astype(BF16)
        e2_ref[h] = jnp.exp(s2 - b[0]).astype(BF16)
        n1_ref[h] = n1
        c2_ref[h] = jnp.exp(s1 - a[0]) / z


PEER_CAND_ROWS = 56


def _peer_route(scores, tn):
    groups, nk, n = scores.shape
    spec = pl.BlockSpec((PEER_HEADS, nk, tn), lambda t: (0, 0, t))
    shape = lambda dt: jax.ShapeDtypeStruct((PEER_HEADS, nk, n), dt)
    return pl.pallas_call(
        _peer_route_kernel,
        grid=(n // tn,),
        in_specs=[pl.BlockSpec((groups, nk, tn), lambda t: (0, 0, t))],
        out_specs=[spec] * 4,
        out_shape=[shape(BF16), shape(BF16), shape(F32), shape(F32)],
        scratch_shapes=[pltpu.VMEM((PEER_CAND_ROWS, tn), F32)],
        compiler_params=_cparams(1),
        name="peer_route",
    )(scores)


def _gelu_tanh(x):
    k = -2.0 * math.sqrt(2.0 / math.pi)
    return x / (1.0 + jnp.exp(x * (k + (k * 0.044715) * (x * x))))


def _peer_experts_kernel(blocks, ht_ref, u_ref, vt_ref, rank2_ref, e2_ref, n1_ref, c2_ref, o_ref, w_ref):
    e = pl.program_id(1)

    @pl.when(e == 0)
    def _():
        o_ref[...] = jnp.zeros_like(o_ref)

    ht = ht_ref[...]
    for ii in range(blocks):
        i = e * blocks + ii
        rows = slice(ii * PEER_NKEYS, (ii + 1) * PEER_NKEYS)
        act = _gelu_tanh(jnp.dot(u_ref[rows, :], ht, preferred_element_type=F32).astype(BF16))
        tn = act.shape[1]
        packed = (PEER_NKEYS // BF16_SUBLANES, BF16_SUBLANES, tn)
        gate = jnp.zeros(packed, BF16)
        for h in range(PEER_HEADS):
            n1 = jnp.broadcast_to(n1_ref[h, pl.ds(i, 1), :], (BF16_SUBLANES, tn)).astype(BF16)
            c2 = jnp.broadcast_to(c2_ref[h, pl.ds(i, 1), :], (BF16_SUBLANES, tn)).astype(BF16)
            gate = gate + jnp.where(rank2_ref[h].reshape(packed) < n1, e2_ref[h].reshape(packed) * c2,
                                    jnp.zeros((), BF16))
        w_ref[rows, :] = gate.reshape(act.shape) * act
    o_ref[...] += jnp.dot(vt_ref[...], w_ref[...], preferred_element_type=F32)


def _peer_experts(ht, u, vt, rank2, e2, n1, c2, tn, ec):
    d, n = ht.shape
    blocks = ec // PEER_NKEYS
    route = pl.BlockSpec((PEER_HEADS, PEER_NKEYS, tn), lambda t, e: (0, 0, t))
    return pl.pallas_call(
        functools.partial(_peer_experts_kernel, blocks),
        grid=(n // tn, PEER_EXPERTS // ec),
        in_specs=[pl.BlockSpec((d, tn), lambda t, e: (0, t)),
                  pl.BlockSpec((ec, d), lambda t, e: (e, 0)),
                  pl.BlockSpec((d, ec), lambda t, e: (0, e)),
                  route, route, route, route],
        out_specs=pl.BlockSpec((d, tn), lambda t, e: (0, t)),
        out_shape=jax.ShapeDtypeStruct((d, n), F32),
        scratch_shapes=[pltpu.VMEM((ec, tn), BF16)],
        compiler_params=_cparams(2),
        name="peer_experts",
    )(ht, u, vt, rank2, e2, n1, c2)


def _peer(h, w_q, keys, u_tab, v_tab):
    b, s, d = h.shape
    n = b * s
    tn = min(n, 512)
    ht = h.reshape(n, d).T
    wq_t = w_q.T.astype(BF16)
    kb = keys.reshape(2 * PEER_HEADS, PEER_NKEYS, PEER_DK).astype(BF16)
    scores = _peer_scores(ht, wq_t, kb, tn)
    rank2, e2, n1, c2 = _peer_route(scores, min(n, 256))
    out_t = _peer_experts(ht, u_tab.astype(BF16), v_tab.T.astype(BF16), rank2, e2, n1, c2, tn, 2048)
    return out_t.T.reshape(b, s, d)


def _rope_tables(t, n_ctx):
    rows = t // GRID_W
    row = np.repeat(np.arange(rows), GRID_W).astype(np.float32)
    col = np.tile(np.arange(GRID_W), rows).astype(np.float32)
    inv_freq = jnp.asarray(ROPE_THETA, F32) ** (-jnp.arange(ROPE_AXIS_FREQS, dtype=F32) / ROPE_AXIS_FREQS)
    ang = jnp.concatenate([jnp.asarray(row)[:, None] * inv_freq] * 2
                          + [jnp.asarray(col)[:, None] * inv_freq] * 2, axis=1)
    pad = lambda a, fill: jnp.concatenate([jnp.full((n_ctx, MLA_ROPE), fill, F32), a], axis=0)
    return pad(jnp.cos(ang), 1.0), pad(jnp.sin(ang), 0.0)


def _split_w_in(w_in):
    ml_cols = 4 * ML_W + ML_GATE_COLS
    w_ml = jnp.pad(w_in[:, :ml_cols], ((0, 0), (0, ML_PAD - ml_cols)))
    w_rw = w_in[:, ml_cols:ml_cols + RW_COLS]
    w_at = w_in[:, ml_cols + RW_COLS:]
    k_rope = w_at[:, MLA_Q_RANK + MLA_KV_RANK:]
    w_at = jnp.concatenate([w_at, _rot_cols(k_rope)], axis=1)
    w_at = jnp.pad(w_at, ((0, 0), (0, MLA_PAD - w_at.shape[1])))
    return w_ml.astype(BF16), w_rw.astype(BF16), w_at.astype(BF16)


def kernel(x, c, ctx, c_ctx, w_mod, b_mod, w_in, ml_conv_w, ml_conv_b, ml_i_bias, ml_f_bias, ml_norm_g,
           rw_mu, rw_w0, rw_w_up, rw_a0, rw_a_up, rw_g_up, rw_k_k, rw_k_a, rw_r_k, rw_gn_g, rw_gn_b,
           mla_q_norm_g, mla_q_up, mla_kv_norm_g, mla_kv_up, w_out, ln_mix_g, ln_mix_b,
           peer_w_q, peer_keys, peer_u, peer_v, ln_ffn_g, ln_ffn_b):
    b, t, d = x.shape
    n_ctx = ctx.shape[1]
    assert n_ctx % ROW_TILE == 0 and t % ROW_TILE == 0 and d == D_MODEL
    n_ctx_tiles = n_ctx // ROW_TILE
    cos, sin = _rope_tables(t, n_ctx)
    mod_rows = -(-(b + 1) // SUBLANES) * SUBLANES
    cvec = jnp.concatenate([c, c_ctx[None], jnp.zeros((mod_rows - b - 1, d), F32)], axis=0)

    x_all = jnp.concatenate([ctx, x], axis=1)
    depth = w_mod.shape[0]
    for l in range(depth):
        need_ctx = l < depth - 1
        mod = _modulation(cvec, w_mod[l], b_mod[l]).reshape(mod_rows, 6, d)
        mods = jnp.stack([jnp.broadcast_to(mod[b], (b, 6, d)), mod[:b]], axis=1)

        pml, prw, pat = _in_proj(x_all, mods, *_split_w_in(w_in[l]), n_ctx_tiles)

        q, k, lg = _ml_prep(pml, ml_conv_w[l], ml_conv_b[l], ml_i_bias[l], ml_f_bias[l], n_ctx_tiles)
        hf, hb = _ml_scan(q, k, pml, lg, n_ctx)

        r, kh, v, g, bonus, w, ab, kt = _rw_prep(
            prw, rw_mu[l], rw_w0[l], rw_w_up[l], rw_a0[l], rw_a_up[l], rw_g_up[l], rw_k_k[l], rw_k_a[l],
            rw_r_k[l], n_ctx_tiles)
        yf, yb = _rw_scan(r, kh, v, w, ab, kt, n_ctx)

        qn, qr, kn, va, kr = _mla_prep(pat, mla_q_norm_g[l], mla_q_up[l], mla_kv_norm_g[l], mla_kv_up[l],
                                       cos, sin)
        at_ctx, at_lat = _mla(qn, qr, kn, va, kr, n_ctx, need_ctx)
        first_tile = 0 if need_ctx else n_ctx_tiles
        y_at = jnp.concatenate([at_ctx, at_lat], axis=1) if need_ctx else jnp.pad(
            at_lat, ((0, 0), (n_ctx, 0), (0, 0)))

        x_mid, h_ffn = _out_proj(x_all, mods, pml, hf, hb, yf, yb, bonus, g, y_at, ml_norm_g[l], rw_gn_g[l],
                                 rw_gn_b[l], w_out[l], ln_mix_g[l], ln_mix_b[l], n_ctx_tiles, first_tile)
        y = _peer(h_ffn, peer_w_q[l], peer_keys[l], peer_u[l], peer_v[l])
        x_all = _final_norm(x_mid, y, mods, ln_ffn_g[l], ln_ffn_b[l], n_ctx_tiles, first_tile)
    return x_all
```

```python
import functools
import math

import jax
import jax.numpy as jnp
import numpy as np
from jax import lax
from jax.experimental import pallas as pl
from jax.experimental.pallas import tpu as pltpu

F32 = jnp.float32
BF16 = jnp.bfloat16
HIGHEST = lax.Precision.HIGHEST

LANES = 128
SUBLANES = 8
BF16_SUBLANES = 16

D_MODEL = 1024
DEPTH = 2
GRID_W = 64

ML_HEADS = 4
ML_DH = 64
ML_W = ML_HEADS * ML_DH
ML_CHUNK = 64
ML_M_INIT = -1e30
ML_GATE_COLS = 4 * ML_HEADS
ML_PAD = 4 * ML_W + LANES

RW_HEADS = 6
RW_DH = 64
RW_W = RW_HEADS * RW_DH
RW_PAIRS = RW_HEADS // 2
RW_LORA = 64
RW_G_LORA = 128
RW_DECAY_SCALE = math.exp(-0.5)
RW_GN_EPS = 64e-5
RW_COLS = 3 * RW_W + 4 * RW_LORA + RW_G_LORA
RW_CHUNK = 64

MLA_HEADS = 6
MLA_NOPE = 64
MLA_ROPE = 32
MLA_V = 64
MLA_Q_RANK = 384
MLA_KV_RANK = 256
MLA_W = MLA_HEADS * MLA_V
MLA_SCALE = (MLA_NOPE + MLA_ROPE) ** -0.5
MLA_PAD = 768
ROPE_AXIS_FREQS = MLA_ROPE // 4
ROPE_THETA = 10000.0

PEER_HEADS = 8
PEER_NKEYS = 128
PEER_EXPERTS = PEER_NKEYS * PEER_NKEYS
PEER_DK = 128
PEER_TOPK = 16

DEEPNORM_ALPHA = (2 * DEPTH) ** 0.25
LN_EPS = 1e-6

ROW_TILE = 256
MOD_COL_TILE = 1536
ATTN_Q_TILE = 512
PEER_TOKEN_TILE = 512
PEER_ROUTE_TILE = 256
PEER_EXPERT_BLOCK = 2048
VMEM_LIMIT = 56 * 1024 * 1024


def _cparams(n_axes):
    return pltpu.CompilerParams(dimension_semantics=("arbitrary",) * n_axes,
                                vmem_limit_bytes=VMEM_LIMIT)


def _ln_rows(x, eps=LN_EPS):
    mu = jnp.mean(x, axis=-1, keepdims=True)
    xc = x - mu
    var = jnp.mean(xc * xc, axis=-1, keepdims=True)
    return xc * lax.rsqrt(var + eps)


def _sigmoid(x):
    return 1.0 / (1.0 + jnp.exp(-x))


def _group_sum(x, gmat):
    hi = x.astype(BF16)
    lo = (x - hi.astype(F32)).astype(BF16)
    return (jnp.dot(hi, gmat, preferred_element_type=F32) + jnp.dot(lo, gmat, preferred_element_type=F32))


def _group_ln(x, gmat, eps):
    xc = x - _group_sum(x, gmat)
    return xc * lax.rsqrt(_group_sum(xc * xc, gmat) + eps)


def _group_matrix(width, group, value):
    idx = np.arange(width) // group
    return jnp.asarray((idx[:, None] == idx[None, :]).astype(np.float32) * value).astype(BF16)


def _mod_kernel(c_ref, w_ref, b_ref, o_ref):
    c = c_ref[...]
    s = c * _sigmoid(c)
    o_ref[...] = jnp.dot(s, w_ref[...], precision=HIGHEST, preferred_element_type=F32) + b_ref[...]


def _modulation(cvec, w_mod, b_mod):
    rows, d = cvec.shape
    n = w_mod.shape[1]
    tn = MOD_COL_TILE
    return pl.pallas_call(
        _mod_kernel,
        grid=(n // tn,),
        in_specs=[pl.BlockSpec((rows, d), lambda j: (0, 0)),
                  pl.BlockSpec((d, tn), lambda j: (0, j)),
                  pl.BlockSpec((1, tn), lambda j: (0, j))],
        out_specs=pl.BlockSpec((rows, tn), lambda j: (0, j)),
        out_shape=jax.ShapeDtypeStruct((rows, n), F32),
        compiler_params=_cparams(1),
        name="modulation",
    )(cvec, w_mod, b_mod.reshape(1, n))


def _in_proj_kernel(x_ref, mod_ref, wml_ref, wrw_ref, wat_ref, pml_ref, prw_ref, pat_ref):
    mod = mod_ref[0, 0]
    h = _ln_rows(x_ref[0]) * (1.0 + mod[1:2]) + mod[0:1]
    hb = h.astype(BF16)
    pml_ref[0] = jnp.dot(hb, wml_ref[...], preferred_element_type=F32)
    prw_ref[0] = jnp.dot(hb, wrw_ref[...], preferred_element_type=F32)
    pat_ref[0] = jnp.dot(hb, wat_ref[...], preferred_element_type=F32)


def _in_proj(x_all, mods, wml, wrw, wat, n_ctx_tiles):
    b, s, d = x_all.shape
    tm = ROW_TILE
    seg = lambda i: (i >= n_ctx_tiles).astype(jnp.int32)
    full = lambda w: pl.BlockSpec(w.shape, lambda bi, i: (0, 0))
    out = lambda w: pl.BlockSpec((1, tm, w.shape[1]), lambda bi, i: (bi, i, 0))
    return pl.pallas_call(
        _in_proj_kernel,
        grid=(b, s // tm),
        in_specs=[pl.BlockSpec((1, tm, d), lambda bi, i: (bi, i, 0)),
                  pl.BlockSpec((1, 1, 6, d), lambda bi, i: (bi, seg(i), 0, 0)),
                  full(wml), full(wrw), full(wat)],
        out_specs=[out(wml), out(wrw), out(wat)],
        out_shape=[jax.ShapeDtypeStruct((b, s, w.shape[1]), F32) for w in (wml, wrw, wat)],
        compiler_params=_cparams(2),
        name="in_proj",
    )(x_all, mods, wml, wrw, wat)


def _halo_specs(width, col_block, tm, s):
    per = tm // SUBLANES
    last = s // SUBLANES - 1
    prev = pl.BlockSpec((1, SUBLANES, width),
                        lambda bi, i: (bi, jnp.maximum(i * per - 1, 0), col_block))
    nxt = pl.BlockSpec((1, SUBLANES, width),
                       lambda bi, i: (bi, jnp.minimum((i + 1) * per, last), col_block))
    return prev, nxt


def _neighbours(x, prev_blk, next_blk, tile, n_ctx_tiles, n_tiles):
    tm = x.shape[0]
    row = lax.broadcasted_iota(jnp.int32, x.shape, 0)
    starts = jnp.logical_or(tile == 0, tile == n_ctx_tiles)
    ends = jnp.logical_or(tile == n_ctx_tiles - 1, tile == n_tiles - 1)
    prev_row = jnp.where(starts, 0.0, prev_blk[SUBLANES - 1:SUBLANES, :])
    next_row = jnp.where(ends, 0.0, next_blk[0:1, :])
    prev = jnp.where(row == 0, prev_row, pltpu.roll(x, 1, 0))
    nxt = jnp.where(row == tm - 1, next_row, pltpu.roll(x, tm - 1, 0))
    return prev, nxt


def _ml_prep_kernel(n_ctx_tiles, n_tiles, qk_ref, prev_ref, next_ref, gate_ref, cw_ref, cb_ref,
                    gb_ref, fmask_ref, q_ref, k_ref, lg_ref):
    tile = pl.program_id(1)
    x = qk_ref[0]
    prev, nxt = _neighbours(x, prev_ref[0], next_ref[0], tile, n_ctx_tiles, n_tiles)
    cw = cw_ref[...]
    z = prev * cw[0:1] + x * cw[1:2] + nxt * cw[2:3] + cb_ref[...]
    z = z * _sigmoid(z)
    q_ref[0] = z[:, :ML_W] * (ML_DH ** -0.5)
    k_ref[0] = z[:, ML_W:]
    g = gate_ref[0] + gb_ref[...]
    log_sig = jnp.minimum(g, 0.0) - jnp.log1p(jnp.exp(-jnp.abs(g)))
    lg_ref[0] = jnp.where(fmask_ref[...] > 0.5, log_sig, g)


def _ml_prep(pml, conv_w, conv_b, i_bias, f_bias, n_ctx_tiles):
    b, s, _ = pml.shape
    tm = ROW_TILE
    n_tiles = s // tm
    gate_bias = jnp.stack([i_bias, f_bias], axis=1).reshape(1, ML_GATE_COLS)
    gate_bias = jnp.pad(gate_bias, ((0, 0), (0, LANES - ML_GATE_COLS)))
    fmask = np.zeros((2, 2, ML_HEADS), np.float32)
    fmask[:, 1] = 1.0
    fmask = jnp.asarray(np.pad(fmask.reshape(1, -1), ((0, 0), (0, LANES - ML_GATE_COLS))))
    prev, nxt = _halo_specs(2 * ML_W, 0, tm, s)
    small = lambda a: pl.BlockSpec(a.shape, lambda bi, i: (0, 0))
    cb = conv_b.reshape(1, -1)
    return pl.pallas_call(
        functools.partial(_ml_prep_kernel, n_ctx_tiles, n_tiles),
        grid=(b, n_tiles),
        in_specs=[pl.BlockSpec((1, tm, 2 * ML_W), lambda bi, i: (bi, i, 0)), prev, nxt,
                  pl.BlockSpec((1, tm, LANES), lambda bi, i: (bi, i, 4 * ML_W // LANES)),
                  small(conv_w), small(cb), small(gate_bias), small(fmask)],
        out_specs=[pl.BlockSpec((1, tm, ML_W), lambda bi, i: (bi, i, 0)),
                   pl.BlockSpec((1, tm, ML_W), lambda bi, i: (bi, i, 0)),
                   pl.BlockSpec((1, tm, LANES), lambda bi, i: (bi, i, 0))],
        out_shape=[jax.ShapeDtypeStruct((b, s, ML_W), F32),
                   jax.ShapeDtypeStruct((b, s, ML_W), F32),
                   jax.ShapeDtypeStruct((b, s, LANES), F32)],
        compiler_params=_cparams(2),
        name="mlstm_prep",
    )(pml, pml, pml, pml, conv_w, cb, gate_bias, fmask)


def _ml_scan_kernel(n_chunks, n_ctx_chunks, q_ref, k_ref, v_ref, lg_ref, hf_ref, hb_ref,
                    ct_ref, n_ref, m_ref):
    L = ML_CHUNK
    units = 2 * ML_HEADS
    per_tile = LANES // ML_DH
    row = lax.broadcasted_iota(jnp.int32, (units, L, L), 1)
    col = lax.broadcasted_iota(jnp.int32, (units, L, L), 2)
    sign = jnp.where(lax.broadcasted_iota(jnp.int32, (units, L, L), 0) < ML_HEADS, 1, -1)
    eye = row == col
    seen = (col - row) * sign <= 0
    seen_t = (row - col) * sign <= 0
    h_refs = (hf_ref, hb_ref)

    def bmm(a, b):
        return jnp.einsum("uij,ujk->uik", a.astype(BF16), b.astype(BF16), preferred_element_type=F32)

    def chunk(c, _):
        back = jnp.where(c < n_ctx_chunks, n_ctx_chunks - 1 - c, n_ctx_chunks + n_chunks - 1 - c)
        rows = [pl.ds(pl.multiple_of(cc * L, L), L) for cc in (c, back)]
        q, k, kt, v, i_col, f_col = [], [], [], [], [], []
        for d in range(2):
            lg = lg_ref[0, rows[d], :]
            for tile in range(ML_W // LANES):
                lanes = slice(tile * LANES, (tile + 1) * LANES)
                q2, k2, v2 = q_ref[0, rows[d], lanes], k_ref[0, rows[d], lanes], v_ref[0, rows[d], lanes]
                kt2 = k2.T
                for hh in range(per_tile):
                    head = tile * per_tile + hh
                    sub = slice(hh * ML_DH, (hh + 1) * ML_DH)
                    gate = d * 2 * ML_HEADS + head
                    q.append(q2[:, sub]); k.append(k2[:, sub]); v.append(v2[:, sub]); kt.append(kt2[sub, :])
                    i_col.append(lg[:, gate:gate + 1])
                    f_col.append(lg[:, gate + ML_HEADS:gate + ML_HEADS + 1])
        q, k, kt, v, i_col, f_col = (jnp.stack(a, axis=0) for a in (q, k, kt, v, i_col, f_col))
        ct, n, m = ct_ref[...], n_ref[...], m_ref[...]

        bcum_row = jnp.sum(jnp.where(seen_t, f_col, 0.0), axis=1, keepdims=True)
        bcum_col = jnp.sum(jnp.where(eye, bcum_row, 0.0), axis=2, keepdims=True)
        i_row = jnp.sum(jnp.where(eye, i_col, 0.0), axis=1, keepdims=True)
        dmat = jnp.where(seen, bcum_col - bcum_row + i_row, -jnp.inf)
        rmax = jnp.max(dmat, axis=2, keepdims=True)
        sx = bmm(q, kt) * jnp.exp(dmat - rmax)
        sv = bmm(sx, v)
        rs = jnp.sum(sx, axis=2, keepdims=True)
        b_end = jnp.concatenate([bcum_col[:ML_HEADS, L - 1:L, :], bcum_col[ML_HEADS:, 0:1, :]], axis=0)
        g = b_end - bcum_col + i_col
        gmax = jnp.max(g, axis=1, keepdims=True)
        wkx = jnp.exp(g - gmax)
        kv = bmm(kt, wkx * v)
        nx = jnp.sum(wkx * k, axis=1, keepdims=True)
        a_inter = bcum_col + m
        m_t = jnp.maximum(a_inter, rmax)
        w_inter = jnp.exp(a_inter - m_t)
        scale = jnp.exp(rmax - m_t)
        num = w_inter * bmm(q, ct) + scale * sv
        den = w_inter * jnp.sum(q * n, axis=2, keepdims=True) + scale * rs
        h = num / jnp.maximum(jnp.abs(den), jnp.exp(-m_t))
        m_new = jnp.maximum(b_end + m, gmax)
        decay = jnp.exp(b_end + m - m_new)
        carry_scale = jnp.exp(gmax - m_new)
        ct_ref[...] = decay * ct + carry_scale * kv
        n_ref[...] = decay * n + carry_scale * nx
        m_ref[...] = m_new
        for d in range(2):
            for tile in range(ML_W // LANES):
                first = d * ML_HEADS + tile * per_tile
                h_refs[d][0, rows[d], tile * LANES:(tile + 1) * LANES] = jnp.concatenate(
                    [h[first + hh] for hh in range(per_tile)], axis=1)
        return 0

    ct_ref[...] = jnp.zeros_like(ct_ref)
    n_ref[...] = jnp.zeros_like(n_ref)
    m_ref[...] = jnp.full(m_ref.shape, ML_M_INIT, F32)
    lax.fori_loop(0, n_chunks, chunk, 0)


def _ml_scan(q, k, pml, lg, n_ctx):
    b, s, _ = q.shape
    nc = s // ML_CHUNK
    wide = lambda blk: pl.BlockSpec((1, s, ML_W), lambda bi: (bi, 0, blk))
    return pl.pallas_call(
        functools.partial(_ml_scan_kernel, nc, n_ctx // ML_CHUNK),
        grid=(b,),
        in_specs=[wide(0), wide(0), wide(2), pl.BlockSpec((1, s, LANES), lambda bi: (bi, 0, 0))],
        out_specs=[wide(0), wide(0)],
        out_shape=[jax.ShapeDtypeStruct((b, s, ML_W), F32)] * 2,
        scratch_shapes=[pltpu.VMEM((2 * ML_HEADS, ML_DH, ML_DH), F32),
                        pltpu.VMEM((2 * ML_HEADS, 1, ML_DH), F32),
                        pltpu.VMEM((2 * ML_HEADS, 1, 1), F32)],
        compiler_params=_cparams(1),
        name="mlstm_scan",
    )(q, k, pml, lg)


def _rw_prep_kernel(n_ctx_tiles, n_tiles, p_ref, prev_ref, next_ref, mu_ref, wup_ref, aup_ref,
                    gup_ref, w0_ref, a0_ref, kk_ref, ka_ref, rho_ref, gsum_ref,
                    r_ref, kh_ref, v_ref, g_ref, bonus_ref, w_ref, ab_ref, kt_ref):
    tile = pl.program_id(1)
    p = p_ref[0]
    prev, nxt = _neighbours(p, prev_ref[0], next_ref[0], tile, n_ctx_tiles, n_tiles)
    mu = mu_ref[...]
    p = p + mu[0:1] * (prev - p) + mu[1:2] * (nxt - p)
    r = p[:, 0:RW_W]
    k = p[:, RW_W:2 * RW_W]
    v = p[:, 2 * RW_W:3 * RW_W]
    base = 3 * RW_W
    wd = p[:, base:base + 2 * RW_LORA]
    ad = p[:, base + 2 * RW_LORA:base + 4 * RW_LORA]
    gd = p[:, base + 4 * RW_LORA:base + 4 * RW_LORA + RW_G_LORA]
    dot = functools.partial(jnp.dot, preferred_element_type=F32)
    w = jnp.exp(-RW_DECAY_SCALE * _sigmoid(w0_ref[...] + dot(jnp.tanh(wd).astype(BF16), wup_ref[...])))
    a = _sigmoid(a0_ref[...] + dot(ad.astype(BF16), aup_ref[...]))
    g = dot(_sigmoid(gd).astype(BF16), gup_ref[...])
    kk = k * kk_ref[...]
    ss = _group_sum(kk * kk, gsum_ref[...])
    kh = kk * lax.rsqrt(ss + 1e-12)
    ka = ka_ref[...]
    rk = r * rho_ref[...]
    kt_sum = jnp.zeros_like(k)
    for dr in range(2):
        a_d = a[:, dr * RW_W:(dr + 1) * RW_W]
        kt_d = k * (1.0 + (a_d - 1.0) * ka)
        kt_ref[0, :, dr * RW_W:(dr + 1) * RW_W] = kt_d
        ab_ref[0, :, dr * RW_W:(dr + 1) * RW_W] = kh * a_d
        kt_sum = kt_sum + kt_d
    bonus_ref[0] = _group_sum(rk * kt_sum, gsum_ref[...]) * v
    r_ref[0] = r
    kh_ref[0] = kh
    v_ref[0] = v
    g_ref[0] = g
    w_ref[0] = w


def _block_diag2(up):
    z = jnp.zeros_like(up[0])
    return jnp.concatenate([jnp.concatenate([up[0], z], axis=1),
                            jnp.concatenate([z, up[1]], axis=1)], axis=0)


def _rw_prep(prw, mu, w0, w_up, a0, a_up, g_up, k_k, k_a, r_k, n_ctx_tiles):
    b, s, _ = prw.shape
    tm = ROW_TILE
    n_tiles = s // tm
    prev, nxt = _halo_specs(RW_COLS, 0, tm, s)
    small = lambda a: pl.BlockSpec(a.shape, lambda bi, i: (0, 0))
    consts = [mu, _block_diag2(w_up).astype(BF16), _block_diag2(a_up).astype(BF16),
              g_up.astype(BF16), w0.reshape(1, 2 * RW_W), a0.reshape(1, 2 * RW_W),
              k_k.reshape(1, RW_W), k_a.reshape(1, RW_W), r_k.reshape(1, RW_W),
              _group_matrix(RW_W, RW_DH, 1.0)]
    one = lambda width, dt=F32: (pl.BlockSpec((1, tm, width), lambda bi, i: (bi, i, 0)),
                                 jax.ShapeDtypeStruct((b, s, width), dt))
    outs = [one(RW_W)] * 5 + [one(2 * RW_W)] * 3
    return pl.pallas_call(
        functools.partial(_rw_prep_kernel, n_ctx_tiles, n_tiles),
        grid=(b, n_tiles),
        in_specs=[pl.BlockSpec((1, tm, RW_COLS), lambda bi, i: (bi, i, 0)), prev, nxt]
                 + [small(a) for a in consts],
        out_specs=[o[0] for o in outs],
        out_shape=[o[1] for o in outs],
        compiler_params=_cparams(2),
        name="rwkv_prep",
    )(prw, prw, prw, *consts)


def _rw_scan_kernel(batch, *refs):
    ins = (refs[0:6], refs[6:12])
    ones_ref = refs[12]
    y_refs = refs[13:15]
    state_ref, yacc_ref, xs_ref, z_ref, vcat_ref = refs[15:20]
    L = RW_CHUNK
    items = batch * RW_PAIRS
    per_dir = items * RW_DH

    def item_rows(d, bi, p):
        start = ((d * batch + bi) * RW_PAIRS + p) * RW_DH
        return slice(start, start + RW_DH)

    @pl.when(pl.program_id(0) == 0)
    def _():
        state_ref[...] = jnp.zeros_like(state_ref)

    yacc_ref[...] = jnp.zeros_like(yacc_ref)
    for d in range(2):
        for bi in range(batch):
            for p in range(RW_PAIRS):
                vt = ins[d][5][bi, :, p * LANES:(p + 1) * LANES].T
                vcat_ref[item_rows(d, bi, p), :] = jnp.concatenate([vt[:RW_DH], vt[RW_DH:]], axis=1)
    lane = lax.broadcasted_iota(jnp.int32, (RW_DH, LANES), 1)
    step_of_lane = jnp.where(lane < RW_DH, lane, lane - RW_DH)

    def step(j, _):
        steps = (j, L - 1 - j)
        rows_t = [[[ref[bi, pl.ds(steps[d], 1), :] for bi in range(batch)] for ref in ins[d][:5]]
                  for d in range(2)]
        vec = lambda rows, bi, p: jnp.broadcast_to(rows[bi][:, p * LANES:(p + 1) * LANES], (RW_DH, LANES))
        items = [(d, bi, p) for d in range(2) for bi in range(batch) for p in range(RW_PAIRS)]

        for d, bi, p in items:
            rows = item_rows(d, bi, p)
            xs_ref[rows, :] = (state_ref[rows, :] * vec(rows_t[d][3], bi, p)).astype(BF16)
        sa = jnp.dot(xs_ref[...], ones_ref[...], preferred_element_type=F32)
        pick = [jnp.where(lane < RW_DH, steps[d], steps[d] + RW_DH) for d in range(2)]

        for d, bi, p in items:
            w_t, ab_t, kt_t, _, r_t = rows_t[d]
            rows = item_rows(d, bi, p)
            vb = jnp.take_along_axis(vcat_ref[rows, :], pick[d], axis=1)
            st = (state_ref[rows, :] * vec(w_t, bi, p) - sa[rows] * vec(ab_t, bi, p)
                  + vb * vec(kt_t, bi, p))
            state_ref[rows, :] = st
            z_ref[rows, :] = (st * vec(r_t, bi, p)).astype(BF16)
        y = jnp.dot(z_ref[...], ones_ref[...], preferred_element_type=F32)
        for d, bi, p in items:
            rows = item_rows(d, bi, p)
            yacc_ref[rows, :] = jnp.where(step_of_lane == steps[d], y[rows], yacc_ref[rows, :])
        return 0

    lax.fori_loop(0, L, step, 0)
    for d in range(2):
        for bi in range(batch):
            for p in range(RW_PAIRS):
                ya = yacc_ref[item_rows(d, bi, p), :]
                ya = jnp.concatenate([ya[:, :RW_DH], ya[:, RW_DH:]], axis=0)
                y_refs[d][bi, :, p * LANES:(p + 1) * LANES] = ya.T


def _rw_scan(r, kh, v, w, ab, kt, n_ctx):
    b, s, _ = r.shape
    L = RW_CHUNK
    nc = s // L
    ncc = n_ctx // L
    rows = b * RW_PAIRS * RW_DH
    cmap = (lambda c: c,
            lambda c: jnp.where(c < ncc, ncc - 1 - c, ncc + nc - 1 - c))
    head_ones = _group_matrix(LANES, RW_DH, 1.0)
    in_specs, args = [], []
    for d in range(2):
        cm = cmap[d]
        per_dir = lambda c, cm=cm, d=d: (0, cm(c), d)
        shared = lambda c, cm=cm: (0, cm(c), 0)
        in_specs += [pl.BlockSpec((b, L, RW_W), per_dir)] * 3
        in_specs += [pl.BlockSpec((b, L, RW_W), shared)] * 3
        args += [w, ab, kt, kh, r, v]
    in_specs.append(pl.BlockSpec(head_ones.shape, lambda c: (0, 0)))
    args.append(head_ones)
    out_specs = [pl.BlockSpec((b, L, RW_W), lambda c, cm=cm: (0, cm(c), 0)) for cm in cmap]
    return pl.pallas_call(
        functools.partial(_rw_scan_kernel, b),
        grid=(nc,),
        in_specs=in_specs,
        out_specs=out_specs,
        out_shape=[jax.ShapeDtypeStruct((b, s, RW_W), F32)] * 2,
        scratch_shapes=[pltpu.VMEM((2 * rows, LANES), F32), pltpu.VMEM((2 * rows, LANES), F32),
                        pltpu.VMEM((2 * rows, LANES), BF16), pltpu.VMEM((2 * rows, LANES), BF16),
                        pltpu.VMEM((2 * rows, LANES), F32)],
        compiler_params=_cparams(1),
        name="rwkv_scan",
    )(*args)


def _mla_prep_kernel(p_ref, cosq_ref, sinq_ref, cosk_ref, sink_ref, qg_ref, kvg_ref, wq_ref, wkv_ref,
                     qn_ref, qr_ref, kn_ref, v_ref, kr_ref):
    p = p_ref[0]
    cq = p[:, :MLA_Q_RANK]
    ckv = p[:, MLA_Q_RANK:MLA_Q_RANK + MLA_KV_RANK]
    base = MLA_Q_RANK + MLA_KV_RANK
    k_rope = p[:, base:base + MLA_ROPE]
    k_rot = p[:, base + MLA_ROPE:base + 2 * MLA_ROPE]
    rms = lambda x, g: x * lax.rsqrt(jnp.mean(x * x, axis=-1, keepdims=True) + LN_EPS) * g
    q = jnp.dot(rms(cq, qg_ref[...]).astype(BF16), wq_ref[...], preferred_element_type=F32)
    kv = jnp.dot(rms(ckv, kvg_ref[...]).astype(BF16), wkv_ref[...], preferred_element_type=F32)
    nope = MLA_HEADS * MLA_NOPE
    rope = MLA_HEADS * MLA_ROPE
    q = q * (MLA_SCALE * math.log2(math.e))
    qn_ref[0] = q[:, :nope]
    qr_ref[0] = q[:, nope:nope + rope] * cosq_ref[...] + q[:, nope + rope:nope + 2 * rope] * sinq_ref[...]
    kn_ref[0] = kv[:, :nope]
    v_ref[0] = kv[:, nope:]
    kr_ref[0] = k_rope * cosk_ref[...] + k_rot * sink_ref[...]


def _rot_cols(w):
    shape = w.shape
    w = w.reshape(shape[:-1] + (shape[-1] // 16, 2, 8))
    return jnp.concatenate([-w[..., 1:2, :], w[..., 0:1, :]], axis=-2).reshape(shape)


def _mla_prep(pat, q_norm_g, q_up, kv_norm_g, kv_up, cos, sin):
    b, s, _ = pat.shape
    tm = ROW_TILE
    qw = q_up.reshape(MLA_Q_RANK, MLA_HEADS, MLA_NOPE + MLA_ROPE)
    q_nope = qw[:, :, :MLA_NOPE].reshape(MLA_Q_RANK, -1)
    q_rope = qw[:, :, MLA_NOPE:]
    wq = jnp.concatenate([q_nope, q_rope.reshape(MLA_Q_RANK, -1),
                          _rot_cols(q_rope).reshape(MLA_Q_RANK, -1)], axis=1).astype(BF16)
    kvw = kv_up.reshape(MLA_KV_RANK, MLA_HEADS, MLA_NOPE + MLA_V)
    wkv = jnp.concatenate([kvw[:, :, :MLA_NOPE].reshape(MLA_KV_RANK, -1),
                           kvw[:, :, MLA_NOPE:].reshape(MLA_KV_RANK, -1)], axis=1).astype(BF16)
    cosq = jnp.tile(cos, (1, MLA_HEADS))
    sinq = jnp.tile(sin, (1, MLA_HEADS))
    rows = lambda width: pl.BlockSpec((tm, width), lambda bi, i: (i, 0))
    small = lambda a: pl.BlockSpec(a.shape, lambda bi, i: (0, 0))
    qg = q_norm_g.reshape(1, -1)
    kvg = kv_norm_g.reshape(1, -1)
    out = lambda width: (pl.BlockSpec((1, tm, width), lambda bi, i: (bi, i, 0)),
                         jax.ShapeDtypeStruct((b, s, width), F32))
    outs = [out(MLA_HEADS * MLA_NOPE), out(MLA_HEADS * MLA_ROPE), out(MLA_HEADS * MLA_NOPE),
            out(MLA_W), out(MLA_ROPE)]
    return pl.pallas_call(
        _mla_prep_kernel,
        grid=(b, s // tm),
        in_specs=[pl.BlockSpec((1, tm, MLA_PAD), lambda bi, i: (bi, i, 0)),
                  rows(MLA_HEADS * MLA_ROPE), rows(MLA_HEADS * MLA_ROPE), rows(MLA_ROPE), rows(MLA_ROPE),
                  small(qg), small(kvg), small(wq), small(wkv)],
        out_specs=[o[0] for o in outs],
        out_shape=[o[1] for o in outs],
        compiler_params=_cparams(2),
        name="mla_prep",
    )(pat, cosq, sinq, cos, sin, qg, kvg, wq, wkv)


ATTN_HEADS_PER_STEP = 3


def _attn_kernel(q_ref, k_ref, v_ref, o_ref):
    for h in range(ATTN_HEADS_PER_STEP):
        s = lax.dot_general(q_ref[0, h], k_ref[0, h], (((1,), (1,)), ((), ())),
                            preferred_element_type=F32)
        p = jnp.exp2(s - jnp.max(s, axis=-1, keepdims=True))
        denom = jnp.sum(p, axis=-1, keepdims=True)
        o_ref[0, h] = jnp.dot(p.astype(BF16), v_ref[0, h], preferred_element_type=F32) / denom


def _attention(q, k, v, tq):
    b, h, t, dq = q.shape
    sk = k.shape[2]
    dv = v.shape[3]
    hp = ATTN_HEADS_PER_STEP
    return pl.pallas_call(
        _attn_kernel,
        grid=(b, h // hp, t // tq),
        in_specs=[pl.BlockSpec((1, hp, tq, dq), lambda bi, hi, i: (bi, hi, i, 0)),
                  pl.BlockSpec((1, hp, sk, dq), lambda bi, hi, i: (bi, hi, 0, 0)),
                  pl.BlockSpec((1, hp, sk, dv), lambda bi, hi, i: (bi, hi, 0, 0))],
        out_specs=pl.BlockSpec((1, hp, tq, dv), lambda bi, hi, i: (bi, hi, i, 0)),
        out_shape=jax.ShapeDtypeStruct((b, h, t, dv), F32),
        compiler_params=_cparams(3),
        name="mla_attention",
    )(q, k, v)


def _mla(qn, qr, kn, v, kr, n_ctx, need_ctx):
    b, s, _ = qn.shape
    heads = lambda a, w: a.reshape(b, s, MLA_HEADS, w).transpose(0, 2, 1, 3)
    q = jnp.concatenate([heads(qn, MLA_NOPE), heads(qr, MLA_ROPE)], axis=-1).astype(BF16)
    k_rope = jnp.broadcast_to(kr[:, None], (b, MLA_HEADS, s, MLA_ROPE))
    k = jnp.concatenate([heads(kn, MLA_NOPE), k_rope], axis=-1).astype(BF16)
    vh = heads(v, MLA_V).astype(BF16)
    pad = ((0, 0), (0, 0), (0, 0), (0, LANES - MLA_NOPE - MLA_ROPE))
    q, k = jnp.pad(q, pad), jnp.pad(k, pad)
    merge = lambda a: a.transpose(0, 2, 1, 3).reshape(b, -1, MLA_W)
    t = s - n_ctx
    y_lat = merge(_attention(q[:, :, n_ctx:], k, vh, min(t, ATTN_Q_TILE)))
    if not need_ctx:
        return None, y_lat
    y_ctx = merge(_attention(q[:, :, :n_ctx], k[:, :, :n_ctx], vh[:, :, :n_ctx], n_ctx))
    return y_ctx, y_lat


def _out_proj_kernel(x_ref, mod_ref, hf_ref, hb_ref, o_ref, yf_ref, yb_ref, bonus_ref, g_ref, at_ref,
                     g4_ref, g6_ref, mlg_ref, gng_ref, gnb_ref, wml_ref, wrw_ref, wat_ref,
                     lng_ref, lnb_ref, xo_ref, ho_ref):
    mod = mod_ref[0, 0]
    ml = _sigmoid(o_ref[0]) * _group_ln(hf_ref[0] + hb_ref[0], g4_ref[...], LN_EPS) * mlg_ref[...]
    z = _group_ln(yf_ref[0] + yb_ref[0], g6_ref[...], RW_GN_EPS) * gng_ref[...] + gnb_ref[...]
    rw = (z + bonus_ref[0]) * g_ref[0]
    dot = functools.partial(jnp.dot, preferred_element_type=F32)
    y = (dot(ml.astype(BF16), wml_ref[...]) + dot(rw.astype(BF16), wrw_ref[...])
         + dot(at_ref[0].astype(BF16), wat_ref[...]))
    x = _ln_rows(DEEPNORM_ALPHA * x_ref[0] + mod[2:3] * y) * lng_ref[...] + lnb_ref[...]
    xo_ref[0] = x
    ho_ref[...] = (_ln_rows(x) * (1.0 + mod[4:5]) + mod[3:4]).T.astype(BF16)


def _out_proj(x_all, mods, pml, hf, hb, yf, yb, bonus, g, y_at, norm_g, gn_g, gn_b, w_out,
              ln_g, ln_b, n_ctx_tiles, first_tile):
    b, s, d = x_all.shape
    tm = ROW_TILE
    n_tiles = s // tm - first_tile
    seg = lambda i: (i + first_tile >= n_ctx_tiles).astype(jnp.int32)
    rows = lambda width, blk=0: pl.BlockSpec((1, tm, width), lambda bi, i: (bi, i + first_tile, blk))
    small = lambda a: pl.BlockSpec(a.shape, lambda bi, i: (0, 0))
    wb = w_out.astype(BF16)
    consts = [_group_matrix(ML_W, ML_DH, 1.0 / ML_DH), _group_matrix(RW_W, RW_DH, 1.0 / RW_DH),
              norm_g.reshape(1, -1), gn_g.reshape(1, -1), gn_b.reshape(1, -1),
              wb[:ML_W], wb[ML_W:ML_W + RW_W], wb[ML_W + RW_W:],
              ln_g.reshape(1, -1), ln_b.reshape(1, -1)]
    out_rows = lambda: pl.BlockSpec((1, tm, d), lambda bi, i: (bi, i, 0))
    return pl.pallas_call(
        _out_proj_kernel,
        grid=(b, n_tiles),
        in_specs=[rows(d), pl.BlockSpec((1, 1, 6, d), lambda bi, i: (bi, seg(i), 0, 0)),
                  rows(ML_W), rows(ML_W), rows(ML_W, 3 * ML_W // ML_W),
                  rows(RW_W), rows(RW_W), rows(RW_W), rows(RW_W), rows(MLA_W)]
                 + [small(a) for a in consts],
        out_specs=[out_rows(), pl.BlockSpec((d, tm), lambda bi, i: (0, bi * n_tiles + i))],
        out_shape=[jax.ShapeDtypeStruct((b, n_tiles * tm, d), F32),
                   jax.ShapeDtypeStruct((d, b * n_tiles * tm), BF16)],
        compiler_params=_cparams(2),
        name="out_proj",
    )(x_all, mods, hf, hb, pml, yf, yb, bonus, g, y_at, *consts)


def _final_norm_kernel(x_ref, y_ref, mod_ref, g_ref, b_ref, o_ref):
    mod = mod_ref[0, 0]
    y = y_ref[...].T
    o_ref[0] = _ln_rows(DEEPNORM_ALPHA * x_ref[0] + mod[5:6] * y) * g_ref[...] + b_ref[...]


def _final_norm(x, y_t, mods, ln_g, ln_b, n_ctx_tiles, first_tile):
    b, s, d = x.shape
    tm = ROW_TILE
    n_tiles = s // tm
    seg = lambda i: (i + first_tile >= n_ctx_tiles).astype(jnp.int32)
    rows = pl.BlockSpec((1, tm, d), lambda bi, i: (bi, i, 0))
    cols = pl.BlockSpec((d, tm), lambda bi, i: (0, bi * n_tiles + i))
    small = pl.BlockSpec((1, d), lambda bi, i: (0, 0))
    return pl.pallas_call(
        _final_norm_kernel,
        grid=(b, n_tiles),
        in_specs=[rows, cols, pl.BlockSpec((1, 1, 6, d), lambda bi, i: (bi, seg(i), 0, 0)), small, small],
        out_specs=rows,
        out_shape=jax.ShapeDtypeStruct((b, s, d), F32),
        compiler_params=_cparams(2),
        name="final_norm",
    )(x, y_t, mods, ln_g.reshape(1, d), ln_b.reshape(1, d))


def _peer_scores_kernel(ht_ref, wq_ref, keys_ref, s_ref):
    q = jnp.dot(wq_ref[...], ht_ref[...], preferred_element_type=F32).astype(BF16)
    for g in range(2 * PEER_HEADS):
        s_ref[g] = jnp.dot(keys_ref[g], q[g * PEER_DK:(g + 1) * PEER_DK], preferred_element_type=F32)


def _peer_scores(ht, wq_t, keys, tn):
    d, n = ht.shape
    groups = 2 * PEER_HEADS
    return pl.pallas_call(
        _peer_scores_kernel,
        grid=(n // tn,),
        in_specs=[pl.BlockSpec((d, tn), lambda t: (0, t)),
                  pl.BlockSpec(wq_t.shape, lambda t: (0, 0)),
                  pl.BlockSpec(keys.shape, lambda t: (0, 0, 0))],
        out_specs=pl.BlockSpec((groups, PEER_NKEYS, tn), lambda t: (0, 0, t)),
        out_shape=jax.ShapeDtypeStruct((groups, PEER_NKEYS, n), F32),
        compiler_params=_cparams(1),
        name="peer_scores",
    )(ht, wq_t, keys)


PEER_RANKS = PEER_TOPK + 1


PEER_UNRANKED = float(PEER_NKEYS - 1)


def _top_rows(s, count):
    rows = []
    rank = jnp.full(s.shape, PEER_UNRANKED, F32)
    for r in range(count):
        m = jnp.max(s, axis=0, keepdims=True)
        rows.append(m)
        hit = s == m
        rank = jnp.where(hit, float(r), rank)
        s = jnp.where(hit, -jnp.inf, s)
    return rows, rank


def _peer_route_kernel(s_ref, rank2_ref, e2_ref, n1_ref, c2_ref, cand_ref):
    cand_ref[...] = jnp.full(cand_ref.shape, -jnp.inf, F32)
    for h in range(PEER_HEADS):
        s1 = s_ref[2 * h]
        s2 = s_ref[2 * h + 1]
        a, _ = _top_rows(s1, PEER_RANKS)
        b, rank2 = _top_rows(s2, PEER_RANKS)
        pairs = [(p, q) for p in range(PEER_RANKS) for q in range(PEER_RANKS // (p + 1))]
        for slot, (p, q) in enumerate(pairs):
            cand_ref[slot:slot + 1, :] = a[p] + b[q]
        top, _ = _top_rows(cand_ref[...], PEER_RANKS)
        threshold = 0.5 * (top[PEER_TOPK - 1] + top[PEER_TOPK])
        z = sum(jnp.exp(row - top[0]) for row in top[:PEER_TOPK])
        need = threshold - s1
        n1 = jnp.zeros_like(s1)
        for q in range(PEER_RANKS):
            n1 = jnp.where(b[q] >= need, float(q + 1), n1)
        rank2_ref[h] = rank2.astype(BF16)
        e2_ref[h] = jnp.exp(s2 - b[0]).astype(BF16)
        n1_ref[h] = n1
        c2_ref[h] = jnp.exp(s1 - a[0]) / z


PEER_CAND_ROWS = 56


def _peer_route(scores, tn):
    groups, nk, n = scores.shape
    spec = pl.BlockSpec((PEER_HEADS, nk, tn), lambda t: (0, 0, t))
    shape = lambda dt: jax.ShapeDtypeStruct((PEER_HEADS, nk, n), dt)
    return pl.pallas_call(
        _peer_route_kernel,
        grid=(n // tn,),
        in_specs=[pl.BlockSpec((groups, nk, tn), lambda t: (0, 0, t))],
        out_specs=[spec] * 4,
        out_shape=[shape(BF16), shape(BF16), shape(F32), shape(F32)],
        scratch_shapes=[pltpu.VMEM((PEER_CAND_ROWS, tn), F32)],
        compiler_params=_cparams(1),
        name="peer_route",
    )(scores)


def _gelu_tanh(x):
    k = -2.0 * math.sqrt(2.0 / math.pi)
    return x / (1.0 + jnp.exp(x * (k + (k * 0.044715) * (x * x))))


def _peer_experts_kernel(blocks, ht_ref, u_ref, vt_ref, rank2_ref, e2_ref, n1_ref, c2_ref, o_ref, w_ref):
    e = pl.program_id(1)

    @pl.when(e == 0)
    def _():
        o_ref[...] = jnp.zeros_like(o_ref)

    ht = ht_ref[...]
    for ii in range(blocks):
        i = e * blocks + ii
        rows = slice(ii * PEER_NKEYS, (ii + 1) * PEER_NKEYS)
        act = _gelu_tanh(jnp.dot(u_ref[rows, :], ht, preferred_element_type=F32).astype(BF16))
        tn = act.shape[1]
        packed = (PEER_NKEYS // BF16_SUBLANES, BF16_SUBLANES, tn)
        gate = jnp.zeros(packed, BF16)
        for h in range(PEER_HEADS):
            n1 = jnp.broadcast_to(n1_ref[h, pl.ds(i, 1), :], (BF16_SUBLANES, tn)).astype(BF16)
            c2 = jnp.broadcast_to(c2_ref[h, pl.ds(i, 1), :], (BF16_SUBLANES, tn)).astype(BF16)
            gate = gate + jnp.where(rank2_ref[h].reshape(packed) < n1, e2_ref[h].reshape(packed) * c2,
                                    jnp.zeros((), BF16))
        w_ref[rows, :] = gate.reshape(act.shape) * act
    o_ref[...] += jnp.dot(vt_ref[...], w_ref[...], preferred_element_type=F32)


def _peer_experts(ht, u, vt, rank2, e2, n1, c2, tn, ec):
    d, n = ht.shape
    blocks = ec // PEER_NKEYS
    route = pl.BlockSpec((PEER_HEADS, PEER_NKEYS, tn), lambda t, e: (0, 0, t))
    return pl.pallas_call(
        functools.partial(_peer_experts_kernel, blocks),
        grid=(n // tn, PEER_EXPERTS // ec),
        in_specs=[pl.BlockSpec((d, tn), lambda t, e: (0, t)),
                  pl.BlockSpec((ec, d), lambda t, e: (e, 0)),
                  pl.BlockSpec((d, ec), lambda t, e: (0, e)),
                  route, route, route, route],
        out_specs=pl.BlockSpec((d, tn), lambda t, e: (0, t)),
        out_shape=jax.ShapeDtypeStruct((d, n), F32),
        scratch_shapes=[pltpu.VMEM((ec, tn), BF16)],
        compiler_params=_cparams(2),
        name="peer_experts",
    )(ht, u, vt, rank2, e2, n1, c2)


def _peer(ht, w_q, keys, u_tab, v_tab):
    n = ht.shape[1]
    tn = min(n, PEER_TOKEN_TILE)
    wq_t = w_q.T.astype(BF16)
    kb = keys.reshape(2 * PEER_HEADS, PEER_NKEYS, PEER_DK).astype(BF16)
    scores = _peer_scores(ht, wq_t, kb, tn)
    rank2, e2, n1, c2 = _peer_route(scores, min(n, PEER_ROUTE_TILE))
    return _peer_experts(ht, u_tab.astype(BF16), v_tab.T.astype(BF16), rank2, e2, n1, c2, tn,
                         PEER_EXPERT_BLOCK)


def _rope_tables(t, n_ctx):
    rows = t // GRID_W
    row = np.repeat(np.arange(rows), GRID_W).astype(np.float32)
    col = np.tile(np.arange(GRID_W), rows).astype(np.float32)
    inv_freq = jnp.asarray(ROPE_THETA, F32) ** (-jnp.arange(ROPE_AXIS_FREQS, dtype=F32) / ROPE_AXIS_FREQS)
    ang = jnp.concatenate([jnp.asarray(row)[:, None] * inv_freq] * 2
                          + [jnp.asarray(col)[:, None] * inv_freq] * 2, axis=1)
    pad = lambda a, fill: jnp.concatenate([jnp.full((n_ctx, MLA_ROPE), fill, F32), a], axis=0)
    return pad(jnp.cos(ang), 1.0), pad(jnp.sin(ang), 0.0)


def _split_w_in(w_in):
    ml_cols = 4 * ML_W + ML_GATE_COLS
    w_ml = jnp.pad(w_in[:, :ml_cols], ((0, 0), (0, ML_PAD - ml_cols)))
    w_rw = w_in[:, ml_cols:ml_cols + RW_COLS]
    w_at = w_in[:, ml_cols + RW_COLS:]
    k_rope = w_at[:, MLA_Q_RANK + MLA_KV_RANK:]
    w_at = jnp.concatenate([w_at, _rot_cols(k_rope)], axis=1)
    w_at = jnp.pad(w_at, ((0, 0), (0, MLA_PAD - w_at.shape[1])))
    return w_ml.astype(BF16), w_rw.astype(BF16), w_at.astype(BF16)


def kernel(x, c, ctx, c_ctx, w_mod, b_mod, w_in, ml_conv_w, ml_conv_b, ml_i_bias, ml_f_bias, ml_norm_g,
           rw_mu, rw_w0, rw_w_up, rw_a0, rw_a_up, rw_g_up, rw_k_k, rw_k_a, rw_r_k, rw_gn_g, rw_gn_b,
           mla_q_norm_g, mla_q_up, mla_kv_norm_g, mla_kv_up, w_out, ln_mix_g, ln_mix_b,
           peer_w_q, peer_keys, peer_u, peer_v, ln_ffn_g, ln_ffn_b):
    b, t, d = x.shape
    n_ctx = ctx.shape[1]
    assert n_ctx % ROW_TILE == 0 and t % ROW_TILE == 0 and d == D_MODEL
    n_ctx_tiles = n_ctx // ROW_TILE
    cos, sin = _rope_tables(t, n_ctx)
    mod_rows = -(-(b + 1) // SUBLANES) * SUBLANES
    cvec = jnp.concatenate([c, c_ctx[None], jnp.zeros((mod_rows - b - 1, d), F32)], axis=0)

    x_all = jnp.concatenate([ctx, x], axis=1)
    depth = w_mod.shape[0]
    for l in range(depth):
        need_ctx = l < depth - 1
        mod = _modulation(cvec, w_mod[l], b_mod[l]).reshape(mod_rows, 6, d)
        mods = jnp.stack([jnp.broadcast_to(mod[b], (b, 6, d)), mod[:b]], axis=1)

        pml, prw, pat = _in_proj(x_all, mods, *_split_w_in(w_in[l]), n_ctx_tiles)

        q, k, lg = _ml_prep(pml, ml_conv_w[l], ml_conv_b[l], ml_i_bias[l], ml_f_bias[l], n_ctx_tiles)
        hf, hb = _ml_scan(q, k, pml, lg, n_ctx)

        r, kh, v, g, bonus, w, ab, kt = _rw_prep(
            prw, rw_mu[l], rw_w0[l], rw_w_up[l], rw_a0[l], rw_a_up[l], rw_g_up[l], rw_k_k[l], rw_k_a[l],
            rw_r_k[l], n_ctx_tiles)
        yf, yb = _rw_scan(r, kh, v, w, ab, kt, n_ctx)

        qn, qr, kn, va, kr = _mla_prep(pat, mla_q_norm_g[l], mla_q_up[l], mla_kv_norm_g[l], mla_kv_up[l],
                                       cos, sin)
        at_ctx, at_lat = _mla(qn, qr, kn, va, kr, n_ctx, need_ctx)
        first_tile = 0 if need_ctx else n_ctx_tiles
        y_at = jnp.concatenate([at_ctx, at_lat], axis=1) if need_ctx else jnp.pad(
            at_lat, ((0, 0), (n_ctx, 0), (0, 0)))

        x_mid, h_ffn = _out_proj(x_all, mods, pml, hf, hb, yf, yb, bonus, g, y_at, ml_norm_g[l], rw_gn_g[l],
                                 rw_gn_b[l], w_out[l], ln_mix_g[l], ln_mix_b[l], n_ctx_tiles, first_tile)
        y = _peer(h_ffn, peer_w_q[l], peer_keys[l], peer_u[l], peer_v[l])
        x_all = _final_norm(x_mid, y, mods, ln_ffn_g[l], ln_ffn_b[l], n_ctx_tiles, first_tile)
    return x_all
```

```python
import functools
import math

import jax
import jax.numpy as jnp
import numpy as np
from jax import lax
from jax.experimental import pallas as pl
from jax.experimental.pallas import tpu as pltpu

F32 = jnp.float32
BF16 = jnp.bfloat16
HIGHEST = lax.Precision.HIGHEST

LANES = 128
SUBLANES = 8
BF16_SUBLANES = 16

D_MODEL = 1024
DEPTH = 2
GRID_W = 64

ML_HEADS = 4
ML_DH = 64
ML_W = ML_HEADS * ML_DH
ML_CHUNK = 64
ML_M_INIT = -1e30
ML_GATE_COLS = 4 * ML_HEADS
ML_PAD = 4 * ML_W + LANES

RW_HEADS = 6
RW_DH = 64
RW_W = RW_HEADS * RW_DH
RW_PAIRS = RW_HEADS // 2
RW_LORA = 64
RW_G_LORA = 128
RW_DECAY_SCALE = math.exp(-0.5)
RW_GN_EPS = 64e-5
RW_COLS = 3 * RW_W + 4 * RW_LORA + RW_G_LORA
RW_CHUNK = 64

MLA_HEADS = 6
MLA_NOPE = 64
MLA_ROPE = 32
MLA_V = 64
MLA_Q_RANK = 384
MLA_KV_RANK = 256
MLA_W = MLA_HEADS * MLA_V
MLA_SCALE = (MLA_NOPE + MLA_ROPE) ** -0.5
MLA_PAD = 768
ROPE_AXIS_FREQS = MLA_ROPE // 4
ROPE_THETA = 10000.0

PEER_HEADS = 8
PEER_NKEYS = 128
PEER_EXPERTS = PEER_NKEYS * PEER_NKEYS
PEER_DK = 128
PEER_TOPK = 16

DEEPNORM_ALPHA = (2 * DEPTH) ** 0.25
LN_EPS = 1e-6

ROW_TILE = 256
MOD_COL_TILE = 1536
ATTN_Q_TILE = 256
PEER_TOKEN_TILE = 512
PEER_ROUTE_TILE = 256
PEER_EXPERT_BLOCK = 2048
VMEM_LIMIT = 56 * 1024 * 1024


def _cparams(n_axes):
    return pltpu.CompilerParams(dimension_semantics=("arbitrary",) * n_axes,
                                vmem_limit_bytes=VMEM_LIMIT)


def _ln_rows(x, eps=LN_EPS):
    mu = jnp.mean(x, axis=-1, keepdims=True)
    xc = x - mu
    var = jnp.mean(xc * xc, axis=-1, keepdims=True)
    return xc * lax.rsqrt(var + eps)


def _sigmoid(x):
    return 1.0 / (1.0 + jnp.exp(-x))


def _group_sum(x, gmat):
    hi = x.astype(BF16)
    lo = (x - hi.astype(F32)).astype(BF16)
    return (jnp.dot(hi, gmat, preferred_element_type=F32) + jnp.dot(lo, gmat, preferred_element_type=F32))


def _group_ln(x, gmat, eps):
    xc = x - _group_sum(x, gmat)
    return xc * lax.rsqrt(_group_sum(xc * xc, gmat) + eps)


def _group_matrix(width, group, value):
    idx = np.arange(width) // group
    return jnp.asarray((idx[:, None] == idx[None, :]).astype(np.float32) * value).astype(BF16)


def _mod_kernel(c_ref, w_ref, b_ref, o_ref):
    c = c_ref[...]
    s = c * _sigmoid(c)
    o_ref[...] = jnp.dot(s, w_ref[...], precision=HIGHEST, preferred_element_type=F32) + b_ref[...]


def _modulation(cvec, w_mod, b_mod):
    rows, d = cvec.shape
    n = w_mod.shape[1]
    tn = MOD_COL_TILE
    return pl.pallas_call(
        _mod_kernel,
        grid=(n // tn,),
        in_specs=[pl.BlockSpec((rows, d), lambda j: (0, 0)),
                  pl.BlockSpec((d, tn), lambda j: (0, j)),
                  pl.BlockSpec((1, tn), lambda j: (0, j))],
        out_specs=pl.BlockSpec((rows, tn), lambda j: (0, j)),
        out_shape=jax.ShapeDtypeStruct((rows, n), F32),
        compiler_params=_cparams(1),
        name="modulation",
    )(cvec, w_mod, b_mod.reshape(1, n))


def _in_proj_kernel(x_ref, mod_ref, wml_ref, wrw_ref, wat_ref, pml_ref, prw_ref, pat_ref):
    mod = mod_ref[0, 0]
    h = _ln_rows(x_ref[0]) * (1.0 + mod[1:2]) + mod[0:1]
    hb = h.astype(BF16)
    pml_ref[0] = jnp.dot(hb, wml_ref[...], preferred_element_type=F32)
    prw_ref[0] = jnp.dot(hb, wrw_ref[...], preferred_element_type=F32)
    pat_ref[0] = jnp.dot(hb, wat_ref[...], preferred_element_type=F32)


def _in_proj(x_all, mods, wml, wrw, wat, n_ctx_tiles):
    b, s, d = x_all.shape
    tm = ROW_TILE
    seg = lambda i: (i >= n_ctx_tiles).astype(jnp.int32)
    full = lambda w: pl.BlockSpec(w.shape, lambda bi, i: (0, 0))
    out = lambda w: pl.BlockSpec((1, tm, w.shape[1]), lambda bi, i: (bi, i, 0))
    return pl.pallas_call(
        _in_proj_kernel,
        grid=(b, s // tm),
        in_specs=[pl.BlockSpec((1, tm, d), lambda bi, i: (bi, i, 0)),
                  pl.BlockSpec((1, 1, 6, d), lambda bi, i: (bi, seg(i), 0, 0)),
                  full(wml), full(wrw), full(wat)],
        out_specs=[out(wml), out(wrw), out(wat)],
        out_shape=[jax.ShapeDtypeStruct((b, s, w.shape[1]), F32) for w in (wml, wrw, wat)],
        compiler_params=_cparams(2),
        name="in_proj",
    )(x_all, mods, wml, wrw, wat)


def _halo_specs(width, col_block, tm, s):
    per = tm // SUBLANES
    last = s // SUBLANES - 1
    prev = pl.BlockSpec((1, SUBLANES, width),
                        lambda bi, i: (bi, jnp.maximum(i * per - 1, 0), col_block))
    nxt = pl.BlockSpec((1, SUBLANES, width),
                       lambda bi, i: (bi, jnp.minimum((i + 1) * per, last), col_block))
    return prev, nxt


def _neighbours(x, prev_blk, next_blk, tile, n_ctx_tiles, n_tiles):
    tm = x.shape[0]
    row = lax.broadcasted_iota(jnp.int32, x.shape, 0)
    starts = jnp.logical_or(tile == 0, tile == n_ctx_tiles)
    ends = jnp.logical_or(tile == n_ctx_tiles - 1, tile == n_tiles - 1)
    prev_row = jnp.where(starts, 0.0, prev_blk[SUBLANES - 1:SUBLANES, :])
    next_row = jnp.where(ends, 0.0, next_blk[0:1, :])
    prev = jnp.where(row == 0, prev_row, pltpu.roll(x, 1, 0))
    nxt = jnp.where(row == tm - 1, next_row, pltpu.roll(x, tm - 1, 0))
    return prev, nxt


def _ml_prep_kernel(n_ctx_tiles, n_tiles, qk_ref, prev_ref, next_ref, gate_ref, cw_ref, cb_ref,
                    gb_ref, fmask_ref, q_ref, k_ref, lg_ref):
    tile = pl.program_id(1)
    x = qk_ref[0]
    prev, nxt = _neighbours(x, prev_ref[0], next_ref[0], tile, n_ctx_tiles, n_tiles)
    cw = cw_ref[...]
    z = prev * cw[0:1] + x * cw[1:2] + nxt * cw[2:3] + cb_ref[...]
    z = z * _sigmoid(z)
    q_ref[0] = z[:, :ML_W] * (ML_DH ** -0.5)
    k_ref[0] = z[:, ML_W:]
    g = gate_ref[0] + gb_ref[...]
    log_sig = jnp.minimum(g, 0.0) - jnp.log1p(jnp.exp(-jnp.abs(g)))
    lg_ref[0] = jnp.where(fmask_ref[...] > 0.5, log_sig, g)


def _ml_prep(pml, conv_w, conv_b, i_bias, f_bias, n_ctx_tiles):
    b, s, _ = pml.shape
    tm = ROW_TILE
    n_tiles = s // tm
    gate_bias = jnp.stack([i_bias, f_bias], axis=1).reshape(1, ML_GATE_COLS)
    gate_bias = jnp.pad(gate_bias, ((0, 0), (0, LANES - ML_GATE_COLS)))
    fmask = np.zeros((2, 2, ML_HEADS), np.float32)
    fmask[:, 1] = 1.0
    fmask = jnp.asarray(np.pad(fmask.reshape(1, -1), ((0, 0), (0, LANES - ML_GATE_COLS))))
    prev, nxt = _halo_specs(2 * ML_W, 0, tm, s)
    small = lambda a: pl.BlockSpec(a.shape, lambda bi, i: (0, 0))
    cb = conv_b.reshape(1, -1)
    return pl.pallas_call(
        functools.partial(_ml_prep_kernel, n_ctx_tiles, n_tiles),
        grid=(b, n_tiles),
        in_specs=[pl.BlockSpec((1, tm, 2 * ML_W), lambda bi, i: (bi, i, 0)), prev, nxt,
                  pl.BlockSpec((1, tm, LANES), lambda bi, i: (bi, i, 4 * ML_W // LANES)),
                  small(conv_w), small(cb), small(gate_bias), small(fmask)],
        out_specs=[pl.BlockSpec((1, tm, ML_W), lambda bi, i: (bi, i, 0)),
                   pl.BlockSpec((1, tm, ML_W), lambda bi, i: (bi, i, 0)),
                   pl.BlockSpec((1, tm, LANES), lambda bi, i: (bi, i, 0))],
        out_shape=[jax.ShapeDtypeStruct((b, s, ML_W), F32),
                   jax.ShapeDtypeStruct((b, s, ML_W), F32),
                   jax.ShapeDtypeStruct((b, s, LANES), F32)],
        compiler_params=_cparams(2),
        name="mlstm_prep",
    )(pml, pml, pml, pml, conv_w, cb, gate_bias, fmask)


def _ml_scan_kernel(n_chunks, n_ctx_chunks, q_ref, k_ref, v_ref, lg_ref, hf_ref, hb_ref,
                    ct_ref, n_ref, m_ref):
    L = ML_CHUNK
    units = 2 * ML_HEADS
    per_tile = LANES // ML_DH
    row = lax.broadcasted_iota(jnp.int32, (units, L, L), 1)
    col = lax.broadcasted_iota(jnp.int32, (units, L, L), 2)
    sign = jnp.where(lax.broadcasted_iota(jnp.int32, (units, L, L), 0) < ML_HEADS, 1, -1)
    eye = row == col
    seen = (col - row) * sign <= 0
    seen_t = (row - col) * sign <= 0
    h_refs = (hf_ref, hb_ref)

    def bmm(a, b):
        return jnp.einsum("uij,ujk->uik", a.astype(BF16), b.astype(BF16), preferred_element_type=F32)

    def chunk(c, _):
        back = jnp.where(c < n_ctx_chunks, n_ctx_chunks - 1 - c, n_ctx_chunks + n_chunks - 1 - c)
        rows = [pl.ds(pl.multiple_of(cc * L, L), L) for cc in (c, back)]
        q, k, kt, v, i_col, f_col = [], [], [], [], [], []
        for d in range(2):
            lg = lg_ref[0, rows[d], :]
            for tile in range(ML_W // LANES):
                lanes = slice(tile * LANES, (tile + 1) * LANES)
                q2, k2, v2 = q_ref[0, rows[d], lanes], k_ref[0, rows[d], lanes], v_ref[0, rows[d], lanes]
                kt2 = k2.T
                for hh in range(per_tile):
                    head = tile * per_tile + hh
                    sub = slice(hh * ML_DH, (hh + 1) * ML_DH)
                    gate = d * 2 * ML_HEADS + head
                    q.append(q2[:, sub]); k.append(k2[:, sub]); v.append(v2[:, sub]); kt.append(kt2[sub, :])
                    i_col.append(lg[:, gate:gate + 1])
                    f_col.append(lg[:, gate + ML_HEADS:gate + ML_HEADS + 1])
        q, k, kt, v, i_col, f_col = (jnp.stack(a, axis=0) for a in (q, k, kt, v, i_col, f_col))
        ct, n, m = ct_ref[...], n_ref[...], m_ref[...]

        bcum_row = jnp.sum(jnp.where(seen_t, f_col, 0.0), axis=1, keepdims=True)
        bcum_col = jnp.sum(jnp.where(eye, bcum_row, 0.0), axis=2, keepdims=True)
        i_row = jnp.sum(jnp.where(eye, i_col, 0.0), axis=1, keepdims=True)
        dmat = jnp.where(seen, bcum_col - bcum_row + i_row, -jnp.inf)
        rmax = jnp.max(dmat, axis=2, keepdims=True)
        sx = bmm(q, kt) * jnp.exp(dmat - rmax)
        sv = bmm(sx, v)
        rs = jnp.sum(sx, axis=2, keepdims=True)
        b_end = jnp.concatenate([bcum_col[:ML_HEADS, L - 1:L, :], bcum_col[ML_HEADS:, 0:1, :]], axis=0)
        g = b_end - bcum_col + i_col
        gmax = jnp.max(g, axis=1, keepdims=True)
        wkx = jnp.exp(g - gmax)
        kv = bmm(kt, wkx * v)
        nx = jnp.sum(wkx * k, axis=1, keepdims=True)
        a_inter = bcum_col + m
        m_t = jnp.maximum(a_inter, rmax)
        w_inter = jnp.exp(a_inter - m_t)
        scale = jnp.exp(rmax - m_t)
        num = w_inter * bmm(q, ct) + scale * sv
        den = w_inter * jnp.sum(q * n, axis=2, keepdims=True) + scale * rs
        h = num / jnp.maximum(jnp.abs(den), jnp.exp(-m_t))
        m_new = jnp.maximum(b_end + m, gmax)
        decay = jnp.exp(b_end + m - m_new)
        carry_scale = jnp.exp(gmax - m_new)
        ct_ref[...] = decay * ct + carry_scale * kv
        n_ref[...] = decay * n + carry_scale * nx
        m_ref[...] = m_new
        for d in range(2):
            for tile in range(ML_W // LANES):
                first = d * ML_HEADS + tile * per_tile
                h_refs[d][0, rows[d], tile * LANES:(tile + 1) * LANES] = jnp.concatenate(
                    [h[first + hh] for hh in range(per_tile)], axis=1)
        return 0

    ct_ref[...] = jnp.zeros_like(ct_ref)
    n_ref[...] = jnp.zeros_like(n_ref)
    m_ref[...] = jnp.full(m_ref.shape, ML_M_INIT, F32)
    lax.fori_loop(0, n_chunks, chunk, 0)


def _ml_scan(q, k, pml, lg, n_ctx):
    b, s, _ = q.shape
    nc = s // ML_CHUNK
    wide = lambda blk: pl.BlockSpec((1, s, ML_W), lambda bi: (bi, 0, blk))
    return pl.pallas_call(
        functools.partial(_ml_scan_kernel, nc, n_ctx // ML_CHUNK),
        grid=(b,),
        in_specs=[wide(0), wide(0), wide(2), pl.BlockSpec((1, s, LANES), lambda bi: (bi, 0, 0))],
        out_specs=[wide(0), wide(0)],
        out_shape=[jax.ShapeDtypeStruct((b, s, ML_W), F32)] * 2,
        scratch_shapes=[pltpu.VMEM((2 * ML_HEADS, ML_DH, ML_DH), F32),
                        pltpu.VMEM((2 * ML_HEADS, 1, ML_DH), F32),
                        pltpu.VMEM((2 * ML_HEADS, 1, 1), F32)],
        compiler_params=_cparams(1),
        name="mlstm_scan",
    )(q, k, pml, lg)


def _rw_prep_kernel(n_ctx_tiles, n_tiles, p_ref, prev_ref, next_ref, mu_ref, wup_ref, aup_ref,
                    gup_ref, w0_ref, a0_ref, kk_ref, ka_ref, rho_ref, gsum_ref,
                    r_ref, kh_ref, v_ref, g_ref, bonus_ref, w_ref, ab_ref, kt_ref):
    tile = pl.program_id(1)
    p = p_ref[0]
    prev, nxt = _neighbours(p, prev_ref[0], next_ref[0], tile, n_ctx_tiles, n_tiles)
    mu = mu_ref[...]
    p = p + mu[0:1] * (prev - p) + mu[1:2] * (nxt - p)
    r = p[:, 0:RW_W]
    k = p[:, RW_W:2 * RW_W]
    v = p[:, 2 * RW_W:3 * RW_W]
    base = 3 * RW_W
    wd = p[:, base:base + 2 * RW_LORA]
    ad = p[:, base + 2 * RW_LORA:base + 4 * RW_LORA]
    gd = p[:, base + 4 * RW_LORA:base + 4 * RW_LORA + RW_G_LORA]
    dot = functools.partial(jnp.dot, preferred_element_type=F32)
    w = jnp.exp(-RW_DECAY_SCALE * _sigmoid(w0_ref[...] + dot(jnp.tanh(wd).astype(BF16), wup_ref[...])))
    a = _sigmoid(a0_ref[...] + dot(ad.astype(BF16), aup_ref[...]))
    g = dot(_sigmoid(gd).astype(BF16), gup_ref[...])
    kk = k * kk_ref[...]
    ss = _group_sum(kk * kk, gsum_ref[...])
    kh = kk * lax.rsqrt(ss + 1e-12)
    ka = ka_ref[...]
    rk = r * rho_ref[...]
    kt_sum = jnp.zeros_like(k)
    for dr in range(2):
        a_d = a[:, dr * RW_W:(dr + 1) * RW_W]
        kt_d = k * (1.0 + (a_d - 1.0) * ka)
        kt_ref[0, :, dr * RW_W:(dr + 1) * RW_W] = kt_d
        ab_ref[0, :, dr * RW_W:(dr + 1) * RW_W] = kh * a_d
        kt_sum = kt_sum + kt_d
    bonus_ref[0] = _group_sum(rk * kt_sum, gsum_ref[...]) * v
    r_ref[0] = r
    kh_ref[0] = kh
    v_ref[0] = v
    g_ref[0] = g
    w_ref[0] = w


def _block_diag2(up):
    z = jnp.zeros_like(up[0])
    return jnp.concatenate([jnp.concatenate([up[0], z], axis=1),
                            jnp.concatenate([z, up[1]], axis=1)], axis=0)


def _rw_prep(prw, mu, w0, w_up, a0, a_up, g_up, k_k, k_a, r_k, n_ctx_tiles):
    b, s, _ = prw.shape
    tm = ROW_TILE
    n_tiles = s // tm
    prev, nxt = _halo_specs(RW_COLS, 0, tm, s)
    small = lambda a: pl.BlockSpec(a.shape, lambda bi, i: (0, 0))
    consts = [mu, _block_diag2(w_up).astype(BF16), _block_diag2(a_up).astype(BF16),
              g_up.astype(BF16), w0.reshape(1, 2 * RW_W), a0.reshape(1, 2 * RW_W),
              k_k.reshape(1, RW_W), k_a.reshape(1, RW_W), r_k.reshape(1, RW_W),
              _group_matrix(RW_W, RW_DH, 1.0)]
    one = lambda width, dt=F32: (pl.BlockSpec((1, tm, width), lambda bi, i: (bi, i, 0)),
                                 jax.ShapeDtypeStruct((b, s, width), dt))
    outs = [one(RW_W)] * 5 + [one(2 * RW_W)] * 3
    return pl.pallas_call(
        functools.partial(_rw_prep_kernel, n_ctx_tiles, n_tiles),
        grid=(b, n_tiles),
        in_specs=[pl.BlockSpec((1, tm, RW_COLS), lambda bi, i: (bi, i, 0)), prev, nxt]
                 + [small(a) for a in consts],
        out_specs=[o[0] for o in outs],
        out_shape=[o[1] for o in outs],
        compiler_params=_cparams(2),
        name="rwkv_prep",
    )(prw, prw, prw, *consts)


def _rw_scan_kernel(batch, *refs):
    ins = (refs[0:6], refs[6:12])
    ones_ref = refs[12]
    y_refs = refs[13:15]
    state_ref, yacc_ref, xs_ref, z_ref, vcat_ref = refs[15:20]
    L = RW_CHUNK
    items = batch * RW_PAIRS
    per_dir = items * RW_DH

    def item_rows(d, bi, p):
        start = ((d * batch + bi) * RW_PAIRS + p) * RW_DH
        return slice(start, start + RW_DH)

    @pl.when(pl.program_id(0) == 0)
    def _():
        state_ref[...] = jnp.zeros_like(state_ref)

    yacc_ref[...] = jnp.zeros_like(yacc_ref)
    for d in range(2):
        for bi in range(batch):
            for p in range(RW_PAIRS):
                vt = ins[d][5][bi, :, p * LANES:(p + 1) * LANES].T
                vcat_ref[item_rows(d, bi, p), :] = jnp.concatenate([vt[:RW_DH], vt[RW_DH:]], axis=1)
    lane = lax.broadcasted_iota(jnp.int32, (RW_DH, LANES), 1)
    step_of_lane = jnp.where(lane < RW_DH, lane, lane - RW_DH)

    def step(j, _):
        steps = (j, L - 1 - j)
        rows_t = [[[ref[bi, pl.ds(steps[d], 1), :] for bi in range(batch)] for ref in ins[d][:5]]
                  for d in range(2)]
        vec = lambda rows, bi, p: jnp.broadcast_to(rows[bi][:, p * LANES:(p + 1) * LANES], (RW_DH, LANES))
        items = [(d, bi, p) for d in range(2) for bi in range(batch) for p in range(RW_PAIRS)]

        for d, bi, p in items:
            rows = item_rows(d, bi, p)
            xs_ref[rows, :] = (state_ref[rows, :] * vec(rows_t[d][3], bi, p)).astype(BF16)
        sa = jnp.dot(xs_ref[...], ones_ref[...], preferred_element_type=F32)
        pick = [jnp.where(lane < RW_DH, steps[d], steps[d] + RW_DH) for d in range(2)]

        for d, bi, p in items:
            w_t, ab_t, kt_t, _, r_t = rows_t[d]
            rows = item_rows(d, bi, p)
            vb = jnp.take_along_axis(vcat_ref[rows, :], pick[d], axis=1)
            st = (state_ref[rows, :] * vec(w_t, bi, p) - sa[rows] * vec(ab_t, bi, p)
                  + vb * vec(kt_t, bi, p))
            state_ref[rows, :] = st
            z_ref[rows, :] = (st * vec(r_t, bi, p)).astype(BF16)
        y = jnp.dot(z_ref[...], ones_ref[...], preferred_element_type=F32)
        for d, bi, p in items:
            rows = item_rows(d, bi, p)
            yacc_ref[rows, :] = jnp.where(step_of_lane == steps[d], y[rows], yacc_ref[rows, :])
        return 0

    lax.fori_loop(0, L, step, 0)
    for d in range(2):
        for bi in range(batch):
            for p in range(RW_PAIRS):
                ya = yacc_ref[item_rows(d, bi, p), :]
                ya = jnp.concatenate([ya[:, :RW_DH], ya[:, RW_DH:]], axis=0)
                y_refs[d][bi, :, p * LANES:(p + 1) * LANES] = ya.T


def _rw_scan(r, kh, v, w, ab, kt, n_ctx):
    b, s, _ = r.shape
    L = RW_CHUNK
    nc = s // L
    ncc = n_ctx // L
    rows = b * RW_PAIRS * RW_DH
    cmap = (lambda c: c,
            lambda c: jnp.where(c < ncc, ncc - 1 - c, ncc + nc - 1 - c))
    head_ones = _group_matrix(LANES, RW_DH, 1.0)
    in_specs, args = [], []
    for d in range(2):
        cm = cmap[d]
        per_dir = lambda c, cm=cm, d=d: (0, cm(c), d)
        shared = lambda c, cm=cm: (0, cm(c), 0)
        in_specs += [pl.BlockSpec((b, L, RW_W), per_dir)] * 3
        in_specs += [pl.BlockSpec((b, L, RW_W), shared)] * 3
        args += [w, ab, kt, kh, r, v]
    in_specs.append(pl.BlockSpec(head_ones.shape, lambda c: (0, 0)))
    args.append(head_ones)
    out_specs = [pl.BlockSpec((b, L, RW_W), lambda c, cm=cm: (0, cm(c), 0)) for cm in cmap]
    return pl.pallas_call(
        functools.partial(_rw_scan_kernel, b),
        grid=(nc,),
        in_specs=in_specs,
        out_specs=out_specs,
        out_shape=[jax.ShapeDtypeStruct((b, s, RW_W), F32)] * 2,
        scratch_shapes=[pltpu.VMEM((2 * rows, LANES), F32), pltpu.VMEM((2 * rows, LANES), F32),
                        pltpu.VMEM((2 * rows, LANES), BF16), pltpu.VMEM((2 * rows, LANES), BF16),
                        pltpu.VMEM((2 * rows, LANES), F32)],
        compiler_params=_cparams(1),
        name="rwkv_scan",
    )(*args)


def _mla_prep_kernel(p_ref, cosq_ref, sinq_ref, cosk_ref, sink_ref, qg_ref, kvg_ref, wq_ref, wkv_ref,
                     q_ref, k_ref, v_ref):
    p = p_ref[0]
    cq = p[:, :MLA_Q_RANK]
    ckv = p[:, MLA_Q_RANK:MLA_Q_RANK + MLA_KV_RANK]
    base = MLA_Q_RANK + MLA_KV_RANK
    k_rope = p[:, base:base + MLA_ROPE]
    k_rot = p[:, base + MLA_ROPE:base + 2 * MLA_ROPE]
    rms = lambda x, g: x * lax.rsqrt(jnp.mean(x * x, axis=-1, keepdims=True) + LN_EPS) * g
    q = jnp.dot(rms(cq, qg_ref[...]).astype(BF16), wq_ref[...], preferred_element_type=F32)
    kv = jnp.dot(rms(ckv, kvg_ref[...]).astype(BF16), wkv_ref[...], preferred_element_type=F32)
    nope = MLA_HEADS * MLA_NOPE
    rope = MLA_HEADS * MLA_ROPE
    q = q * (MLA_SCALE * math.log2(math.e))
    q_rope = q[:, nope:nope + rope] * cosq_ref[...] + q[:, nope + rope:nope + 2 * rope] * sinq_ref[...]
    k_rope = k_rope * cosk_ref[...] + k_rot * sink_ref[...]
    zeros = jnp.zeros((p.shape[0], LANES - MLA_NOPE - MLA_ROPE), F32)
    for h in range(MLA_HEADS):
        nope_cols = slice(h * MLA_NOPE, (h + 1) * MLA_NOPE)
        q_ref[0, h] = jnp.concatenate([q[:, nope_cols], q_rope[:, h * MLA_ROPE:(h + 1) * MLA_ROPE], zeros],
                                      axis=1).astype(BF16)
        k_ref[0, h] = jnp.concatenate([kv[:, nope_cols], k_rope, zeros], axis=1).astype(BF16)
        v_ref[0, h] = kv[:, nope + h * MLA_V:nope + (h + 1) * MLA_V].astype(BF16)


def _rot_cols(w):
    shape = w.shape
    w = w.reshape(shape[:-1] + (shape[-1] // 16, 2, 8))
    return jnp.concatenate([-w[..., 1:2, :], w[..., 0:1, :]], axis=-2).reshape(shape)


def _mla_prep(pat, q_norm_g, q_up, kv_norm_g, kv_up, cos, sin):
    b, s, _ = pat.shape
    tm = ROW_TILE
    qw = q_up.reshape(MLA_Q_RANK, MLA_HEADS, MLA_NOPE + MLA_ROPE)
    q_nope = qw[:, :, :MLA_NOPE].reshape(MLA_Q_RANK, -1)
    q_rope = qw[:, :, MLA_NOPE:]
    wq = jnp.concatenate([q_nope, q_rope.reshape(MLA_Q_RANK, -1),
                          _rot_cols(q_rope).reshape(MLA_Q_RANK, -1)], axis=1).astype(BF16)
    kvw = kv_up.reshape(MLA_KV_RANK, MLA_HEADS, MLA_NOPE + MLA_V)
    wkv = jnp.concatenate([kvw[:, :, :MLA_NOPE].reshape(MLA_KV_RANK, -1),
                           kvw[:, :, MLA_NOPE:].reshape(MLA_KV_RANK, -1)], axis=1).astype(BF16)
    cosq = jnp.tile(cos, (1, MLA_HEADS))
    sinq = jnp.tile(sin, (1, MLA_HEADS))
    rows = lambda width: pl.BlockSpec((tm, width), lambda bi, i: (i, 0))
    small = lambda a: pl.BlockSpec(a.shape, lambda bi, i: (0, 0))
    qg = q_norm_g.reshape(1, -1)
    kvg = kv_norm_g.reshape(1, -1)
    out = lambda width: (pl.BlockSpec((1, MLA_HEADS, tm, width), lambda bi, i: (bi, 0, i, 0)),
                         jax.ShapeDtypeStruct((b, MLA_HEADS, s, width), BF16))
    outs = [out(LANES), out(LANES), out(MLA_V)]
    return pl.pallas_call(
        _mla_prep_kernel,
        grid=(b, s // tm),
        in_specs=[pl.BlockSpec((1, tm, MLA_PAD), lambda bi, i: (bi, i, 0)),
                  rows(MLA_HEADS * MLA_ROPE), rows(MLA_HEADS * MLA_ROPE), rows(MLA_ROPE), rows(MLA_ROPE),
                  small(qg), small(kvg), small(wq), small(wkv)],
        out_specs=[o[0] for o in outs],
        out_shape=[o[1] for o in outs],
        compiler_params=_cparams(2),
        name="mla_prep",
    )(pat, cosq, sinq, cos, sin, qg, kvg, wq, wkv)


def _attn_kernel(q_ref, k_ref, v_ref, o_ref):
    outs = []
    for h in range(MLA_HEADS):
        s = lax.dot_general(q_ref[0, h], k_ref[0, h], (((1,), (1,)), ((), ())),
                            preferred_element_type=F32)
        p = jnp.exp2(s - jnp.max(s, axis=-1, keepdims=True))
        denom = jnp.sum(p, axis=-1, keepdims=True)
        outs.append(jnp.dot(p.astype(BF16), v_ref[0, h], preferred_element_type=F32) / denom)
    o_ref[0] = jnp.concatenate(outs, axis=1)


def _attention(q, k, v, first_row, n_rows, n_keys):
    b, h, _, dq = q.shape
    dv = v.shape[3]
    tq = min(n_rows, ATTN_Q_TILE)
    assert first_row % tq == 0 and n_rows % tq == 0
    return pl.pallas_call(
        _attn_kernel,
        grid=(b, n_rows // tq),
        in_specs=[pl.BlockSpec((1, h, tq, dq), lambda bi, i: (bi, 0, i + first_row // tq, 0)),
                  pl.BlockSpec((1, h, n_keys, dq), lambda bi, i: (bi, 0, 0, 0)),
                  pl.BlockSpec((1, h, n_keys, dv), lambda bi, i: (bi, 0, 0, 0))],
        out_specs=pl.BlockSpec((1, tq, h * dv), lambda bi, i: (bi, i, 0)),
        out_shape=jax.ShapeDtypeStruct((b, n_rows, h * dv), F32),
        compiler_params=_cparams(2),
        name="mla_attention",
    )(q, k, v)


def _mla(q, k, v, n_ctx, need_ctx):
    s = q.shape[2]
    y_lat = _attention(q, k, v, n_ctx, s - n_ctx, s)
    y_ctx = _attention(q, k, v, 0, n_ctx, n_ctx) if need_ctx else None
    return y_ctx, y_lat


def _out_proj_kernel(x_ref, mod_ref, hf_ref, hb_ref, o_ref, yf_ref, yb_ref, bonus_ref, g_ref, at_ref,
                     g4_ref, g6_ref, mlg_ref, gng_ref, gnb_ref, wml_ref, wrw_ref, wat_ref,
                     lng_ref, lnb_ref, xo_ref, ho_ref):
    mod = mod_ref[0, 0]
    ml = _sigmoid(o_ref[0]) * _group_ln(hf_ref[0] + hb_ref[0], g4_ref[...], LN_EPS) * mlg_ref[...]
    z = _group_ln(yf_ref[0] + yb_ref[0], g6_ref[...], RW_GN_EPS) * gng_ref[...] + gnb_ref[...]
    rw = (z + bonus_ref[0]) * g_ref[0]
    dot = functools.partial(jnp.dot, preferred_element_type=F32)
    y = (dot(ml.astype(BF16), wml_ref[...]) + dot(rw.astype(BF16), wrw_ref[...])
         + dot(at_ref[0].astype(BF16), wat_ref[...]))
    x = _ln_rows(DEEPNORM_ALPHA * x_ref[0] + mod[2:3] * y) * lng_ref[...] + lnb_ref[...]
    xo_ref[0] = x
    ho_ref[...] = (_ln_rows(x) * (1.0 + mod[4:5]) + mod[3:4]).T.astype(BF16)


def _out_proj(x_all, mods, pml, hf, hb, yf, yb, bonus, g, y_at, norm_g, gn_g, gn_b, w_out,
              ln_g, ln_b, n_ctx_tiles, first_tile):
    b, s, d = x_all.shape
    tm = ROW_TILE
    n_tiles = s // tm - first_tile
    seg = lambda i: (i + first_tile >= n_ctx_tiles).astype(jnp.int32)
    rows = lambda width, blk=0: pl.BlockSpec((1, tm, width), lambda bi, i: (bi, i + first_tile, blk))
    small = lambda a: pl.BlockSpec(a.shape, lambda bi, i: (0, 0))
    wb = w_out.astype(BF16)
    consts = [_group_matrix(ML_W, ML_DH, 1.0 / ML_DH), _group_matrix(RW_W, RW_DH, 1.0 / RW_DH),
              norm_g.reshape(1, -1), gn_g.reshape(1, -1), gn_b.reshape(1, -1),
              wb[:ML_W], wb[ML_W:ML_W + RW_W], wb[ML_W + RW_W:],
              ln_g.reshape(1, -1), ln_b.reshape(1, -1)]
    out_rows = lambda: pl.BlockSpec((1, tm, d), lambda bi, i: (bi, i, 0))
    return pl.pallas_call(
        _out_proj_kernel,
        grid=(b, n_tiles),
        in_specs=[rows(d), pl.BlockSpec((1, 1, 6, d), lambda bi, i: (bi, seg(i), 0, 0)),
                  rows(ML_W), rows(ML_W), rows(ML_W, 3 * ML_W // ML_W),
                  rows(RW_W), rows(RW_W), rows(RW_W), rows(RW_W), rows(MLA_W)]
                 + [small(a) for a in consts],
        out_specs=[out_rows(), pl.BlockSpec((d, tm), lambda bi, i: (0, bi * n_tiles + i))],
        out_shape=[jax.ShapeDtypeStruct((b, n_tiles * tm, d), F32),
                   jax.ShapeDtypeStruct((d, b * n_tiles * tm), BF16)],
        compiler_params=_cparams(2),
        name="out_proj",
    )(x_all, mods, hf, hb, pml, yf, yb, bonus, g, y_at, *consts)


def _final_norm_kernel(x_ref, y_ref, mod_ref, g_ref, b_ref, o_ref):
    mod = mod_ref[0, 0]
    y = y_ref[...].T
    o_ref[0] = _ln_rows(DEEPNORM_ALPHA * x_ref[0] + mod[5:6] * y) * g_ref[...] + b_ref[...]


def _final_norm(x, y_t, mods, ln_g, ln_b, n_ctx_tiles, first_tile):
    b, s, d = x.shape
    tm = ROW_TILE
    n_tiles = s // tm
    seg = lambda i: (i + first_tile >= n_ctx_tiles).astype(jnp.int32)
    rows = pl.BlockSpec((1, tm, d), lambda bi, i: (bi, i, 0))
    cols = pl.BlockSpec((d, tm), lambda bi, i: (0, bi * n_tiles + i))
    small = pl.BlockSpec((1, d), lambda bi, i: (0, 0))
    return pl.pallas_call(
        _final_norm_kernel,
        grid=(b, n_tiles),
        in_specs=[rows, cols, pl.BlockSpec((1, 1, 6, d), lambda bi, i: (bi, seg(i), 0, 0)), small, small],
        out_specs=rows,
        out_shape=jax.ShapeDtypeStruct((b, s, d), F32),
        compiler_params=_cparams(2),
        name="final_norm",
    )(x, y_t, mods, ln_g.reshape(1, d), ln_b.reshape(1, d))


def _peer_scores_kernel(ht_ref, wq_ref, keys_ref, s_ref):
    q = jnp.dot(wq_ref[...], ht_ref[...], preferred_element_type=F32).astype(BF16)
    for g in range(2 * PEER_HEADS):
        s_ref[g] = jnp.dot(keys_ref[g], q[g * PEER_DK:(g + 1) * PEER_DK], preferred_element_type=F32)


def _peer_scores(ht, wq_t, keys, tn):
    d, n = ht.shape
    groups = 2 * PEER_HEADS
    return pl.pallas_call(
        _peer_scores_kernel,
        grid=(n // tn,),
        in_specs=[pl.BlockSpec((d, tn), lambda t: (0, t)),
                  pl.BlockSpec(wq_t.shape, lambda t: (0, 0)),
                  pl.BlockSpec(keys.shape, lambda t: (0, 0, 0))],
        out_specs=pl.BlockSpec((groups, PEER_NKEYS, tn), lambda t: (0, 0, t)),
        out_shape=jax.ShapeDtypeStruct((groups, PEER_NKEYS, n), F32),
        compiler_params=_cparams(1),
        name="peer_scores",
    )(ht, wq_t, keys)


PEER_RANKS = PEER_TOPK + 1


PEER_UNRANKED = float(PEER_NKEYS - 1)


def _top_rows(s, count):
    rows = []
    rank = jnp.full(s.shape, PEER_UNRANKED, F32)
    for r in range(count):
        m = jnp.max(s, axis=0, keepdims=True)
        rows.append(m)
        hit = s == m
        rank = jnp.where(hit, float(r), rank)
        s = jnp.where(hit, -jnp.inf, s)
    return rows, rank


def _peer_route_kernel(s_ref, rank2_ref, e2_ref, n1_ref, c2_ref, cand_ref):
    cand_ref[...] = jnp.full(cand_ref.shape, -jnp.inf, F32)
    for h in range(PEER_HEADS):
        s1 = s_ref[2 * h]
        s2 = s_ref[2 * h + 1]
        a, _ = _top_rows(s1, PEER_RANKS)
        b, rank2 = _top_rows(s2, PEER_RANKS)
        pairs = [(p, q) for p in range(PEER_RANKS) for q in range(PEER_RANKS // (p + 1))]
        for slot, (p, q) in enumerate(pairs):
            cand_ref[slot:slot + 1, :] = a[p] + b[q]
        top, _ = _top_rows(cand_ref[...], PEER_RANKS)
        threshold = 0.5 * (top[PEER_TOPK - 1] + top[PEER_TOPK])
        z = sum(jnp.exp(row - top[0]) for row in top[:PEER_TOPK])
        need = threshold - s1
        n1 = jnp.zeros_like(s1)
        for q in range(PEER_RANKS):
            n1 = jnp.where(b[q] >= need, float(q + 1), n1)
        rank2_ref[h] = rank2.astype(BF16)
        e2_ref[h] = jnp.exp(s2 - b[0]).astype(BF16)
        n1_ref[h] = n1
        c2_ref[h] = jnp.exp(s1 - a[0]) / z


PEER_CAND_ROWS = 56


def _peer_route(scores, tn):
    groups, nk, n = scores.shape
    spec = pl.BlockSpec((PEER_HEADS, nk, tn), lambda t: (0, 0, t))
    shape = lambda dt: jax.ShapeDtypeStruct((PEER_HEADS, nk, n), dt)
    return pl.pallas_call(
        _peer_route_kernel,
        grid=(n // tn,),
        in_specs=[pl.BlockSpec((groups, nk, tn), lambda t: (0, 0, t))],
        out_specs=[spec] * 4,
        out_shape=[shape(BF16), shape(BF16), shape(F32), shape(F32)],
        scratch_shapes=[pltpu.VMEM((PEER_CAND_ROWS, tn), F32)],
        compiler_params=_cparams(1),
        name="peer_route",
    )(scores)


def _gelu_tanh(x):
    k = -2.0 * math.sqrt(2.0 / math.pi)
    return x / (1.0 + jnp.exp(x * (k + (k * 0.044715) * (x * x))))


def _peer_experts_kernel(blocks, ht_ref, u_ref, vt_ref, rank2_ref, e2_ref, n1_ref, c2_ref, o_ref, w_ref):
    e = pl.program_id(1)

    @pl.when(e == 0)
    def _():
        o_ref[...] = jnp.zeros_like(o_ref)

    ht = ht_ref[...]
    for ii in range(blocks):
        i = e * blocks + ii
        rows = slice(ii * PEER_NKEYS, (ii + 1) * PEER_NKEYS)
        act = _gelu_tanh(jnp.dot(u_ref[rows, :], ht, preferred_element_type=F32).astype(BF16))
        tn = act.shape[1]
        packed = (PEER_NKEYS // BF16_SUBLANES, BF16_SUBLANES, tn)
        gate = jnp.zeros(packed, BF16)
        for h in range(PEER_HEADS):
            n1 = jnp.broadcast_to(n1_ref[h, pl.ds(i, 1), :], (BF16_SUBLANES, tn)).astype(BF16)
            c2 = jnp.broadcast_to(c2_ref[h, pl.ds(i, 1), :], (BF16_SUBLANES, tn)).astype(BF16)
            gate = gate + jnp.where(rank2_ref[h].reshape(packed) < n1, e2_ref[h].reshape(packed) * c2,
                                    jnp.zeros((), BF16))
        w_ref[rows, :] = gate.reshape(act.shape) * act
    o_ref[...] += jnp.dot(vt_ref[...], w_ref[...], preferred_element_type=F32)


def _peer_experts(ht, u, vt, rank2, e2, n1, c2, tn, ec):
    d, n = ht.shape
    blocks = ec // PEER_NKEYS
    route = pl.BlockSpec((PEER_HEADS, PEER_NKEYS, tn), lambda t, e: (0, 0, t))
    return pl.pallas_call(
        functools.partial(_peer_experts_kernel, blocks),
        grid=(n // tn, PEER_EXPERTS // ec),
        in_specs=[pl.BlockSpec((d, tn), lambda t, e: (0, t)),
                  pl.BlockSpec((ec, d), lambda t, e: (e, 0)),
                  pl.BlockSpec((d, ec), lambda t, e: (0, e)),
                  route, route, route, route],
        out_specs=pl.BlockSpec((d, tn), lambda t, e: (0, t)),
        out_shape=jax.ShapeDtypeStruct((d, n), F32),
        scratch_shapes=[pltpu.VMEM((ec, tn), BF16)],
        compiler_params=_cparams(2),
        name="peer_experts",
    )(ht, u, vt, rank2, e2, n1, c2)


def _peer(ht, w_q, keys, u_tab, v_tab):
    n = ht.shape[1]
    tn = min(n, PEER_TOKEN_TILE)
    wq_t = w_q.T.astype(BF16)
    kb = keys.reshape(2 * PEER_HEADS, PEER_NKEYS, PEER_DK).astype(BF16)
    scores = _peer_scores(ht, wq_t, kb, tn)
    rank2, e2, n1, c2 = _peer_route(scores, min(n, PEER_ROUTE_TILE))
    return _peer_experts(ht, u_tab.astype(BF16), v_tab.T.astype(BF16), rank2, e2, n1, c2, tn,
                         PEER_EXPERT_BLOCK)


def _rope_tables(t, n_ctx):
    rows = t // GRID_W
    row = np.repeat(np.arange(rows), GRID_W).astype(np.float32)
    col = np.tile(np.arange(GRID_W), rows).astype(np.float32)
    inv_freq = jnp.asarray(ROPE_THETA, F32) ** (-jnp.arange(ROPE_AXIS_FREQS, dtype=F32) / ROPE_AXIS_FREQS)
    ang = jnp.concatenate([jnp.asarray(row)[:, None] * inv_freq] * 2
                          + [jnp.asarray(col)[:, None] * inv_freq] * 2, axis=1)
    pad = lambda a, fill: jnp.concatenate([jnp.full((n_ctx, MLA_ROPE), fill, F32), a], axis=0)
    return pad(jnp.cos(ang), 1.0), pad(jnp.sin(ang), 0.0)


def _split_w_in(w_in):
    ml_cols = 4 * ML_W + ML_GATE_COLS
    w_ml = jnp.pad(w_in[:, :ml_cols], ((0, 0), (0, ML_PAD - ml_cols)))
    w_rw = w_in[:, ml_cols:ml_cols + RW_COLS]
    w_at = w_in[:, ml_cols + RW_COLS:]
    k_rope = w_at[:, MLA_Q_RANK + MLA_KV_RANK:]
    w_at = jnp.concatenate([w_at, _rot_cols(k_rope)], axis=1)
    w_at = jnp.pad(w_at, ((0, 0), (0, MLA_PAD - w_at.shape[1])))
    return w_ml.astype(BF16), w_rw.astype(BF16), w_at.astype(BF16)


def kernel(x, c, ctx, c_ctx, w_mod, b_mod, w_in, ml_conv_w, ml_conv_b, ml_i_bias, ml_f_bias, ml_norm_g,
           rw_mu, rw_w0, rw_w_up, rw_a0, rw_a_up, rw_g_up, rw_k_k, rw_k_a, rw_r_k, rw_gn_g, rw_gn_b,
           mla_q_norm_g, mla_q_up, mla_kv_norm_g, mla_kv_up, w_out, ln_mix_g, ln_mix_b,
           peer_w_q, peer_keys, peer_u, peer_v, ln_ffn_g, ln_ffn_b):
    b, t, d = x.shape
    n_ctx = ctx.shape[1]
    assert n_ctx % ROW_TILE == 0 and t % ROW_TILE == 0 and d == D_MODEL
    n_ctx_tiles = n_ctx // ROW_TILE
    cos, sin = _rope_tables(t, n_ctx)
    mod_rows = -(-(b + 1) // SUBLANES) * SUBLANES
    cvec = jnp.concatenate([c, c_ctx[None], jnp.zeros((mod_rows - b - 1, d), F32)], axis=0)

    x_all = jnp.concatenate([ctx, x], axis=1)
    depth = w_mod.shape[0]
    for l in range(depth):
        need_ctx = l < depth - 1
        mod = _modulation(cvec, w_mod[l], b_mod[l]).reshape(mod_rows, 6, d)
        mods = jnp.stack([jnp.broadcast_to(mod[b], (b, 6, d)), mod[:b]], axis=1)

        pml, prw, pat = _in_proj(x_all, mods, *_split_w_in(w_in[l]), n_ctx_tiles)

        q, k, lg = _ml_prep(pml, ml_conv_w[l], ml_conv_b[l], ml_i_bias[l], ml_f_bias[l], n_ctx_tiles)
        hf, hb = _ml_scan(q, k, pml, lg, n_ctx)

        r, kh, v, g, bonus, w, ab, kt = _rw_prep(
            prw, rw_mu[l], rw_w0[l], rw_w_up[l], rw_a0[l], rw_a_up[l], rw_g_up[l], rw_k_k[l], rw_k_a[l],
            rw_r_k[l], n_ctx_tiles)
        yf, yb = _rw_scan(r, kh, v, w, ab, kt, n_ctx)

        qa, ka, va = _mla_prep(pat, mla_q_norm_g[l], mla_q_up[l], mla_kv_norm_g[l], mla_kv_up[l], cos, sin)
        at_ctx, at_lat = _mla(qa, ka, va, n_ctx, need_ctx)
        first_tile = 0 if need_ctx else n_ctx_tiles
        y_at = jnp.concatenate([at_ctx, at_lat], axis=1) if need_ctx else jnp.pad(
            at_lat, ((0, 0), (n_ctx, 0), (0, 0)))

        x_mid, h_ffn = _out_proj(x_all, mods, pml, hf, hb, yf, yb, bonus, g, y_at, ml_norm_g[l], rw_gn_g[l],
                                 rw_gn_b[l], w_out[l], ln_mix_g[l], ln_mix_b[l], n_ctx_tiles, first_tile)
        y = _peer(h_ffn, peer_w_q[l], peer_keys[l], peer_u[l], peer_v[l])
        x_all = _final_norm(x_mid, y, mods, ln_ffn_g[l], ln_ffn_b[l], n_ctx_tiles, first_tile)
    return x_all
```

```python
import functools
import math

import jax
import jax.numpy as jnp
import numpy as np
from jax import lax
from jax.experimental import pallas as pl
from jax.experimental.pallas import tpu as pltpu

F32 = jnp.float32
BF16 = jnp.bfloat16
HIGHEST = lax.Precision.HIGHEST

LANES = 128
SUBLANES = 8
BF16_SUBLANES = 16

D_MODEL = 1024
DEPTH = 2
GRID_W = 64

ML_HEADS = 4
ML_DH = 64
ML_W = ML_HEADS * ML_DH
ML_CHUNK = 64
ML_M_INIT = -1e30
ML_GATE_COLS = 4 * ML_HEADS
ML_PAD = 4 * ML_W + LANES

RW_HEADS = 6
RW_DH = 64
RW_W = RW_HEADS * RW_DH
RW_PAIRS = RW_HEADS // 2
RW_LORA = 64
RW_G_LORA = 128
RW_DECAY_SCALE = math.exp(-0.5)
RW_GN_EPS = 64e-5
RW_COLS = 3 * RW_W + 4 * RW_LORA + RW_G_LORA
RW_CHUNK = 64

MLA_HEADS = 6
MLA_NOPE = 64
MLA_ROPE = 32
MLA_V = 64
MLA_Q_RANK = 384
MLA_KV_RANK = 256
MLA_W = MLA_HEADS * MLA_V
MLA_SCALE = (MLA_NOPE + MLA_ROPE) ** -0.5
MLA_PAD = 768
ROPE_AXIS_FREQS = MLA_ROPE // 4
ROPE_THETA = 10000.0

PEER_HEADS = 8
PEER_NKEYS = 128
PEER_EXPERTS = PEER_NKEYS * PEER_NKEYS
PEER_DK = 128
PEER_TOPK = 16

DEEPNORM_ALPHA = (2 * DEPTH) ** 0.25
LN_EPS = 1e-6

ROW_TILE = 256
MOD_COL_TILE = 1536
ATTN_Q_TILE = 256
PEER_TOKEN_TILE = 512
PEER_ROUTE_TILE = 256
PEER_EXPERT_BLOCK = 2048
VMEM_LIMIT = 56 * 1024 * 1024


def _cparams(n_axes):
    return pltpu.CompilerParams(dimension_semantics=("arbitrary",) * n_axes,
                                vmem_limit_bytes=VMEM_LIMIT)


def _ln_rows(x, eps=LN_EPS):
    mu = jnp.mean(x, axis=-1, keepdims=True)
    xc = x - mu
    var = jnp.mean(xc * xc, axis=-1, keepdims=True)
    return xc * lax.rsqrt(var + eps)


def _sigmoid(x):
    return 1.0 / (1.0 + jnp.exp(-x))


def _group_sum(x, gmat):
    hi = x.astype(BF16)
    lo = (x - hi.astype(F32)).astype(BF16)
    return (jnp.dot(hi, gmat, preferred_element_type=F32) + jnp.dot(lo, gmat, preferred_element_type=F32))


def _group_ln(x, gmat, eps):
    xc = x - _group_sum(x, gmat)
    return xc * lax.rsqrt(_group_sum(xc * xc, gmat) + eps)


def _group_matrix(width, group, value):
    idx = np.arange(width) // group
    return jnp.asarray((idx[:, None] == idx[None, :]).astype(np.float32) * value).astype(BF16)


def _mod_kernel(c_ref, w_ref, b_ref, o_ref):
    c = c_ref[...]
    s = c * _sigmoid(c)
    o_ref[...] = jnp.dot(s, w_ref[...], precision=HIGHEST, preferred_element_type=F32) + b_ref[...]


def _modulation(cvec, w_mod, b_mod):
    rows, d = cvec.shape
    n = w_mod.shape[1]
    tn = MOD_COL_TILE
    return pl.pallas_call(
        _mod_kernel,
        grid=(n // tn,),
        in_specs=[pl.BlockSpec((rows, d), lambda j: (0, 0)),
                  pl.BlockSpec((d, tn), lambda j: (0, j)),
                  pl.BlockSpec((1, tn), lambda j: (0, j))],
        out_specs=pl.BlockSpec((rows, tn), lambda j: (0, j)),
        out_shape=jax.ShapeDtypeStruct((rows, n), F32),
        compiler_params=_cparams(1),
        name="modulation",
    )(cvec, w_mod, b_mod.reshape(1, n))


def _in_proj_kernel(x_ref, mod_ref, wml_ref, wrw_ref, wat_ref, pml_ref, prw_ref, pat_ref):
    mod = mod_ref[0, 0]
    h = _ln_rows(x_ref[0]) * (1.0 + mod[1:2]) + mod[0:1]
    hb = h.astype(BF16)
    pml_ref[0] = jnp.dot(hb, wml_ref[...], preferred_element_type=F32)
    prw_ref[0] = jnp.dot(hb, wrw_ref[...], preferred_element_type=F32)
    pat_ref[0] = jnp.dot(hb, wat_ref[...], preferred_element_type=F32)


def _in_proj(x_all, mods, wml, wrw, wat, n_ctx_tiles):
    b, s, d = x_all.shape
    tm = ROW_TILE
    seg = lambda i: (i >= n_ctx_tiles).astype(jnp.int32)
    full = lambda w: pl.BlockSpec(w.shape, lambda bi, i: (0, 0))
    out = lambda w: pl.BlockSpec((1, tm, w.shape[1]), lambda bi, i: (bi, i, 0))
    return pl.pallas_call(
        _in_proj_kernel,
        grid=(b, s // tm),
        in_specs=[pl.BlockSpec((1, tm, d), lambda bi, i: (bi, i, 0)),
                  pl.BlockSpec((1, 1, 6, d), lambda bi, i: (bi, seg(i), 0, 0)),
                  full(wml), full(wrw), full(wat)],
        out_specs=[out(wml), out(wrw), out(wat)],
        out_shape=[jax.ShapeDtypeStruct((b, s, w.shape[1]), F32) for w in (wml, wrw, wat)],
        compiler_params=_cparams(2),
        name="in_proj",
    )(x_all, mods, wml, wrw, wat)


def _halo_specs(width, col_block, tm, s):
    per = tm // SUBLANES
    last = s // SUBLANES - 1
    prev = pl.BlockSpec((1, SUBLANES, width),
                        lambda bi, i: (bi, jnp.maximum(i * per - 1, 0), col_block))
    nxt = pl.BlockSpec((1, SUBLANES, width),
                       lambda bi, i: (bi, jnp.minimum((i + 1) * per, last), col_block))
    return prev, nxt


def _neighbours(x, prev_blk, next_blk, tile, n_ctx_tiles, n_tiles):
    tm = x.shape[0]
    row = lax.broadcasted_iota(jnp.int32, x.shape, 0)
    starts = jnp.logical_or(tile == 0, tile == n_ctx_tiles)
    ends = jnp.logical_or(tile == n_ctx_tiles - 1, tile == n_tiles - 1)
    prev_row = jnp.where(starts, 0.0, prev_blk[SUBLANES - 1:SUBLANES, :])
    next_row = jnp.where(ends, 0.0, next_blk[0:1, :])
    prev = jnp.where(row == 0, prev_row, pltpu.roll(x, 1, 0))
    nxt = jnp.where(row == tm - 1, next_row, pltpu.roll(x, tm - 1, 0))
    return prev, nxt


def _ml_prep_kernel(n_ctx_tiles, n_tiles, qk_ref, prev_ref, next_ref, gate_ref, cw_ref, cb_ref,
                    gb_ref, fmask_ref, q_ref, k_ref, lg_ref):
    tile = pl.program_id(1)
    x = qk_ref[0]
    prev, nxt = _neighbours(x, prev_ref[0], next_ref[0], tile, n_ctx_tiles, n_tiles)
    cw = cw_ref[...]
    z = prev * cw[0:1] + x * cw[1:2] + nxt * cw[2:3] + cb_ref[...]
    z = z * _sigmoid(z)
    q_ref[0] = z[:, :ML_W] * (ML_DH ** -0.5)
    k_ref[0] = z[:, ML_W:]
    g = gate_ref[0] + gb_ref[...]
    log_sig = jnp.minimum(g, 0.0) - jnp.log1p(jnp.exp(-jnp.abs(g)))
    lg_ref[0] = jnp.where(fmask_ref[...] > 0.5, log_sig, g)


def _ml_prep(pml, conv_w, conv_b, i_bias, f_bias, n_ctx_tiles):
    b, s, _ = pml.shape
    tm = ROW_TILE
    n_tiles = s // tm
    gate_bias = jnp.stack([i_bias, f_bias], axis=1).reshape(1, ML_GATE_COLS)
    gate_bias = jnp.pad(gate_bias, ((0, 0), (0, LANES - ML_GATE_COLS)))
    fmask = np.zeros((2, 2, ML_HEADS), np.float32)
    fmask[:, 1] = 1.0
    fmask = jnp.asarray(np.pad(fmask.reshape(1, -1), ((0, 0), (0, LANES - ML_GATE_COLS))))
    prev, nxt = _halo_specs(2 * ML_W, 0, tm, s)
    small = lambda a: pl.BlockSpec(a.shape, lambda bi, i: (0, 0))
    cb = conv_b.reshape(1, -1)
    return pl.pallas_call(
        functools.partial(_ml_prep_kernel, n_ctx_tiles, n_tiles),
        grid=(b, n_tiles),
        in_specs=[pl.BlockSpec((1, tm, 2 * ML_W), lambda bi, i: (bi, i, 0)), prev, nxt,
                  pl.BlockSpec((1, tm, LANES), lambda bi, i: (bi, i, 4 * ML_W // LANES)),
                  small(conv_w), small(cb), small(gate_bias), small(fmask)],
        out_specs=[pl.BlockSpec((1, tm, ML_W), lambda bi, i: (bi, i, 0)),
                   pl.BlockSpec((1, tm, ML_W), lambda bi, i: (bi, i, 0)),
                   pl.BlockSpec((1, tm, LANES), lambda bi, i: (bi, i, 0))],
        out_shape=[jax.ShapeDtypeStruct((b, s, ML_W), F32),
                   jax.ShapeDtypeStruct((b, s, ML_W), F32),
                   jax.ShapeDtypeStruct((b, s, LANES), F32)],
        compiler_params=_cparams(2),
        name="mlstm_prep",
    )(pml, pml, pml, pml, conv_w, cb, gate_bias, fmask)


def _ml_scan_kernel(n_chunks, n_ctx_chunks, q_ref, k_ref, v_ref, lg_ref, hf_ref, hb_ref,
                    ct_ref, n_ref, m_ref):
    L = ML_CHUNK
    units = 2 * ML_HEADS
    per_tile = LANES // ML_DH
    row = lax.broadcasted_iota(jnp.int32, (units, L, L), 1)
    col = lax.broadcasted_iota(jnp.int32, (units, L, L), 2)
    sign = jnp.where(lax.broadcasted_iota(jnp.int32, (units, L, L), 0) < ML_HEADS, 1, -1)
    eye = row == col
    seen = (col - row) * sign <= 0
    seen_t = (row - col) * sign <= 0
    h_refs = (hf_ref, hb_ref)

    def bmm(a, b):
        return jnp.einsum("uij,ujk->uik", a.astype(BF16), b.astype(BF16), preferred_element_type=F32)

    def chunk(c, _):
        back = jnp.where(c < n_ctx_chunks, n_ctx_chunks - 1 - c, n_ctx_chunks + n_chunks - 1 - c)
        rows = [pl.ds(pl.multiple_of(cc * L, L), L) for cc in (c, back)]
        q, k, kt, v, i_col, f_col = [], [], [], [], [], []
        for d in range(2):
            lg = lg_ref[0, rows[d], :]
            for tile in range(ML_W // LANES):
                lanes = slice(tile * LANES, (tile + 1) * LANES)
                q2, k2, v2 = q_ref[0, rows[d], lanes], k_ref[0, rows[d], lanes], v_ref[0, rows[d], lanes]
                kt2 = k2.T
                for hh in range(per_tile):
                    head = tile * per_tile + hh
                    sub = slice(hh * ML_DH, (hh + 1) * ML_DH)
                    gate = d * 2 * ML_HEADS + head
                    q.append(q2[:, sub]); k.append(k2[:, sub]); v.append(v2[:, sub]); kt.append(kt2[sub, :])
                    i_col.append(lg[:, gate:gate + 1])
                    f_col.append(lg[:, gate + ML_HEADS:gate + ML_HEADS + 1])
        q, k, kt, v, i_col, f_col = (jnp.stack(a, axis=0) for a in (q, k, kt, v, i_col, f_col))
        ct, n, m = ct_ref[...], n_ref[...], m_ref[...]

        bcum_row = jnp.sum(jnp.where(seen_t, f_col, 0.0), axis=1, keepdims=True)
        bcum_col = jnp.sum(jnp.where(eye, bcum_row, 0.0), axis=2, keepdims=True)
        i_row = jnp.sum(jnp.where(eye, i_col, 0.0), axis=1, keepdims=True)
        dmat = jnp.where(seen, bcum_col - bcum_row + i_row, -jnp.inf)
        rmax = jnp.max(dmat, axis=2, keepdims=True)
        sx = bmm(q, kt) * jnp.exp(dmat - rmax)
        sv_rs = bmm(sx, jnp.concatenate([v, jnp.ones_like(v)], axis=2))
        sv = sv_rs[:, :, :ML_DH]
        rs = sv_rs[:, :, ML_DH:ML_DH + 1]
        b_end = jnp.concatenate([bcum_col[:ML_HEADS, L - 1:L, :], bcum_col[ML_HEADS:, 0:1, :]], axis=0)
        g = b_end - bcum_col + i_col
        gmax = jnp.max(g, axis=1, keepdims=True)
        wkx = jnp.exp(g - gmax)
        kv = bmm(kt, wkx * v)
        nx = jnp.sum(wkx * k, axis=1, keepdims=True)
        a_inter = bcum_col + m
        m_t = jnp.maximum(a_inter, rmax)
        w_inter = jnp.exp(a_inter - m_t)
        scale = jnp.exp(rmax - m_t)
        num = w_inter * bmm(q, ct) + scale * sv
        den = w_inter * jnp.sum(q * n, axis=2, keepdims=True) + scale * rs
        h = num / jnp.maximum(jnp.abs(den), jnp.exp(-m_t))
        m_new = jnp.maximum(b_end + m, gmax)
        decay = jnp.exp(b_end + m - m_new)
        carry_scale = jnp.exp(gmax - m_new)
        ct_ref[...] = decay * ct + carry_scale * kv
        n_ref[...] = decay * n + carry_scale * nx
        m_ref[...] = m_new
        for d in range(2):
            for tile in range(ML_W // LANES):
                first = d * ML_HEADS + tile * per_tile
                h_refs[d][0, rows[d], tile * LANES:(tile + 1) * LANES] = jnp.concatenate(
                    [h[first + hh] for hh in range(per_tile)], axis=1)
        return 0

    ct_ref[...] = jnp.zeros_like(ct_ref)
    n_ref[...] = jnp.zeros_like(n_ref)
    m_ref[...] = jnp.full(m_ref.shape, ML_M_INIT, F32)
    lax.fori_loop(0, n_chunks, chunk, 0)


def _ml_scan(q, k, pml, lg, n_ctx):
    b, s, _ = q.shape
    nc = s // ML_CHUNK
    wide = lambda blk: pl.BlockSpec((1, s, ML_W), lambda bi: (bi, 0, blk))
    return pl.pallas_call(
        functools.partial(_ml_scan_kernel, nc, n_ctx // ML_CHUNK),
        grid=(b,),
        in_specs=[wide(0), wide(0), wide(2), pl.BlockSpec((1, s, LANES), lambda bi: (bi, 0, 0))],
        out_specs=[wide(0), wide(0)],
        out_shape=[jax.ShapeDtypeStruct((b, s, ML_W), F32)] * 2,
        scratch_shapes=[pltpu.VMEM((2 * ML_HEADS, ML_DH, ML_DH), F32),
                        pltpu.VMEM((2 * ML_HEADS, 1, ML_DH), F32),
                        pltpu.VMEM((2 * ML_HEADS, 1, 1), F32)],
        compiler_params=_cparams(1),
        name="mlstm_scan",
    )(q, k, pml, lg)


def _rw_prep_kernel(n_ctx_tiles, n_tiles, p_ref, prev_ref, next_ref, mu_ref, wup_ref, aup_ref,
                    gup_ref, w0_ref, a0_ref, kk_ref, ka_ref, rho_ref, gsum_ref,
                    r_ref, kh_ref, v_ref, g_ref, bonus_ref, w_ref, ab_ref, kt_ref):
    tile = pl.program_id(1)
    p = p_ref[0]
    prev, nxt = _neighbours(p, prev_ref[0], next_ref[0], tile, n_ctx_tiles, n_tiles)
    mu = mu_ref[...]
    p = p + mu[0:1] * (prev - p) + mu[1:2] * (nxt - p)
    r = p[:, 0:RW_W]
    k = p[:, RW_W:2 * RW_W]
    v = p[:, 2 * RW_W:3 * RW_W]
    base = 3 * RW_W
    wd = p[:, base:base + 2 * RW_LORA]
    ad = p[:, base + 2 * RW_LORA:base + 4 * RW_LORA]
    gd = p[:, base + 4 * RW_LORA:base + 4 * RW_LORA + RW_G_LORA]
    dot = functools.partial(jnp.dot, preferred_element_type=F32)
    w = jnp.exp(-RW_DECAY_SCALE * _sigmoid(w0_ref[...] + dot(jnp.tanh(wd).astype(BF16), wup_ref[...])))
    a = _sigmoid(a0_ref[...] + dot(ad.astype(BF16), aup_ref[...]))
    g = dot(_sigmoid(gd).astype(BF16), gup_ref[...])
    kk = k * kk_ref[...]
    ss = _group_sum(kk * kk, gsum_ref[...])
    kh = kk * lax.rsqrt(ss + 1e-12)
    ka = ka_ref[...]
    rk = r * rho_ref[...]
    kt_sum = jnp.zeros_like(k)
    for dr in range(2):
        a_d = a[:, dr * RW_W:(dr + 1) * RW_W]
        kt_d = k * (1.0 + (a_d - 1.0) * ka)
        kt_ref[0, :, dr * RW_W:(dr + 1) * RW_W] = kt_d
        ab_ref[0, :, dr * RW_W:(dr + 1) * RW_W] = kh * a_d
        kt_sum = kt_sum + kt_d
    bonus_ref[0] = _group_sum(rk * kt_sum, gsum_ref[...]) * v
    r_ref[0] = r
    kh_ref[0] = kh
    v_ref[0] = v
    g_ref[0] = g
    w_ref[0] = w


def _block_diag2(up):
    z = jnp.zeros_like(up[0])
    return jnp.concatenate([jnp.concatenate([up[0], z], axis=1),
                            jnp.concatenate([z, up[1]], axis=1)], axis=0)


def _rw_prep(prw, mu, w0, w_up, a0, a_up, g_up, k_k, k_a, r_k, n_ctx_tiles):
    b, s, _ = prw.shape
    tm = ROW_TILE
    n_tiles = s // tm
    prev, nxt = _halo_specs(RW_COLS, 0, tm, s)
    small = lambda a: pl.BlockSpec(a.shape, lambda bi, i: (0, 0))
    consts = [mu, _block_diag2(w_up).astype(BF16), _block_diag2(a_up).astype(BF16),
              g_up.astype(BF16), w0.reshape(1, 2 * RW_W), a0.reshape(1, 2 * RW_W),
              k_k.reshape(1, RW_W), k_a.reshape(1, RW_W), r_k.reshape(1, RW_W),
              _group_matrix(RW_W, RW_DH, 1.0)]
    one = lambda width, dt=F32: (pl.BlockSpec((1, tm, width), lambda bi, i: (bi, i, 0)),
                                 jax.ShapeDtypeStruct((b, s, width), dt))
    outs = [one(RW_W)] * 5 + [one(2 * RW_W)] * 3
    return pl.pallas_call(
        functools.partial(_rw_prep_kernel, n_ctx_tiles, n_tiles),
        grid=(b, n_tiles),
        in_specs=[pl.BlockSpec((1, tm, RW_COLS), lambda bi, i: (bi, i, 0)), prev, nxt]
                 + [small(a) for a in consts],
        out_specs=[o[0] for o in outs],
        out_shape=[o[1] for o in outs],
        compiler_params=_cparams(2),
        name="rwkv_prep",
    )(prw, prw, prw, *consts)


def _rw_scan_kernel(batch, *refs):
    ins = (refs[0:6], refs[6:12])
    ones_ref = refs[12]
    y_refs = refs[13:15]
    state_ref, yacc_ref, xs_ref, z_ref, vcat_ref = refs[15:20]
    L = RW_CHUNK
    items = batch * RW_PAIRS
    per_dir = items * RW_DH

    def item_rows(d, bi, p):
        start = ((d * batch + bi) * RW_PAIRS + p) * RW_DH
        return slice(start, start + RW_DH)

    @pl.when(pl.program_id(0) == 0)
    def _():
        state_ref[...] = jnp.zeros_like(state_ref)

    yacc_ref[...] = jnp.zeros_like(yacc_ref)
    for d in range(2):
        for bi in range(batch):
            for p in range(RW_PAIRS):
                vt = ins[d][5][bi, :, p * LANES:(p + 1) * LANES].T
                vcat_ref[item_rows(d, bi, p), :] = jnp.concatenate([vt[:RW_DH], vt[RW_DH:]], axis=1)
    lane = lax.broadcasted_iota(jnp.int32, (RW_DH, LANES), 1)
    step_of_lane = jnp.where(lane < RW_DH, lane, lane - RW_DH)

    def step(j, _):
        steps = (j, L - 1 - j)
        rows_t = [[[ref[bi, pl.ds(steps[d], 1), :] for bi in range(batch)] for ref in ins[d][:5]]
                  for d in range(2)]
        vec = lambda rows, bi, p: jnp.broadcast_to(rows[bi][:, p * LANES:(p + 1) * LANES], (RW_DH, LANES))
        items = [(d, bi, p) for d in range(2) for bi in range(batch) for p in range(RW_PAIRS)]

        for d, bi, p in items:
            rows = item_rows(d, bi, p)
            xs_ref[rows, :] = (state_ref[rows, :] * vec(rows_t[d][3], bi, p)).astype(BF16)
        sa = jnp.dot(xs_ref[...], ones_ref[...], preferred_element_type=F32)
        pick = [jnp.where(lane < RW_DH, steps[d], steps[d] + RW_DH) for d in range(2)]

        for d, bi, p in items:
            w_t, ab_t, kt_t, _, r_t = rows_t[d]
            rows = item_rows(d, bi, p)
            vb = jnp.take_along_axis(vcat_ref[rows, :], pick[d], axis=1)
            st = (state_ref[rows, :] * vec(w_t, bi, p) - sa[rows] * vec(ab_t, bi, p)
                  + vb * vec(kt_t, bi, p))
            state_ref[rows, :] = st
            z_ref[rows, :] = (st * vec(r_t, bi, p)).astype(BF16)
        y = jnp.dot(z_ref[...], ones_ref[...], preferred_element_type=F32)
        for d, bi, p in items:
            rows = item_rows(d, bi, p)
            yacc_ref[rows, :] = jnp.where(step_of_lane == steps[d], y[rows], yacc_ref[rows, :])
        return 0

    lax.fori_loop(0, L, step, 0)
    for d in range(2):
        for bi in range(batch):
            for p in range(RW_PAIRS):
                ya = yacc_ref[item_rows(d, bi, p), :]
                ya = jnp.concatenate([ya[:, :RW_DH], ya[:, RW_DH:]], axis=0)
                y_refs[d][bi, :, p * LANES:(p + 1) * LANES] = ya.T


def _rw_scan(r, kh, v, w, ab, kt, n_ctx):
    b, s, _ = r.shape
    L = RW_CHUNK
    nc = s // L
    ncc = n_ctx // L
    rows = b * RW_PAIRS * RW_DH
    cmap = (lambda c: c,
            lambda c: jnp.where(c < ncc, ncc - 1 - c, ncc + nc - 1 - c))
    head_ones = _group_matrix(LANES, RW_DH, 1.0)
    in_specs, args = [], []
    for d in range(2):
        cm = cmap[d]
        per_dir = lambda c, cm=cm, d=d: (0, cm(c), d)
        shared = lambda c, cm=cm: (0, cm(c), 0)
        in_specs += [pl.BlockSpec((b, L, RW_W), per_dir)] * 3
        in_specs += [pl.BlockSpec((b, L, RW_W), shared)] * 3
        args += [w, ab, kt, kh, r, v]
    in_specs.append(pl.BlockSpec(head_ones.shape, lambda c: (0, 0)))
    args.append(head_ones)
    out_specs = [pl.BlockSpec((b, L, RW_W), lambda c, cm=cm: (0, cm(c), 0)) for cm in cmap]
    return pl.pallas_call(
        functools.partial(_rw_scan_kernel, b),
        grid=(nc,),
        in_specs=in_specs,
        out_specs=out_specs,
        out_shape=[jax.ShapeDtypeStruct((b, s, RW_W), F32)] * 2,
        scratch_shapes=[pltpu.VMEM((2 * rows, LANES), F32), pltpu.VMEM((2 * rows, LANES), F32),
                        pltpu.VMEM((2 * rows, LANES), BF16), pltpu.VMEM((2 * rows, LANES), BF16),
                        pltpu.VMEM((2 * rows, LANES), F32)],
        compiler_params=_cparams(1),
        name="rwkv_scan",
    )(*args)


def _mla_prep_kernel(p_ref, cosq_ref, sinq_ref, cosk_ref, sink_ref, qg_ref, kvg_ref, wq_ref, wkv_ref,
                     q_ref, k_ref, v_ref):
    p = p_ref[0]
    cq = p[:, :MLA_Q_RANK]
    ckv = p[:, MLA_Q_RANK:MLA_Q_RANK + MLA_KV_RANK]
    base = MLA_Q_RANK + MLA_KV_RANK
    k_rope = p[:, base:base + MLA_ROPE]
    k_rot = p[:, base + MLA_ROPE:base + 2 * MLA_ROPE]
    rms = lambda x, g: x * lax.rsqrt(jnp.mean(x * x, axis=-1, keepdims=True) + LN_EPS) * g
    q = jnp.dot(rms(cq, qg_ref[...]).astype(BF16), wq_ref[...], preferred_element_type=F32)
    kv = jnp.dot(rms(ckv, kvg_ref[...]).astype(BF16), wkv_ref[...], preferred_element_type=F32)
    nope = MLA_HEADS * MLA_NOPE
    rope = MLA_HEADS * MLA_ROPE
    q = q * (MLA_SCALE * math.log2(math.e))
    q_rope = q[:, nope:nope + rope] * cosq_ref[...] + q[:, nope + rope:nope + 2 * rope] * sinq_ref[...]
    k_rope = k_rope * cosk_ref[...] + k_rot * sink_ref[...]
    zeros = jnp.zeros((p.shape[0], LANES - MLA_NOPE - MLA_ROPE), F32)
    for h in range(MLA_HEADS):
        nope_cols = slice(h * MLA_NOPE, (h + 1) * MLA_NOPE)
        q_ref[0, h] = jnp.concatenate([q[:, nope_cols], q_rope[:, h * MLA_ROPE:(h + 1) * MLA_ROPE], zeros],
                                      axis=1).astype(BF16)
        k_ref[0, h] = jnp.concatenate([kv[:, nope_cols], k_rope, zeros], axis=1).astype(BF16)
        v_ref[0, h] = kv[:, nope + h * MLA_V:nope + (h + 1) * MLA_V].astype(BF16)


def _rot_cols(w):
    shape = w.shape
    w = w.reshape(shape[:-1] + (shape[-1] // 16, 2, 8))
    return jnp.concatenate([-w[..., 1:2, :], w[..., 0:1, :]], axis=-2).reshape(shape)


def _mla_prep(pat, q_norm_g, q_up, kv_norm_g, kv_up, cos, sin):
    b, s, _ = pat.shape
    tm = ROW_TILE
    qw = q_up.reshape(MLA_Q_RANK, MLA_HEADS, MLA_NOPE + MLA_ROPE)
    q_nope = qw[:, :, :MLA_NOPE].reshape(MLA_Q_RANK, -1)
    q_rope = qw[:, :, MLA_NOPE:]
    wq = jnp.concatenate([q_nope, q_rope.reshape(MLA_Q_RANK, -1),
                          _rot_cols(q_rope).reshape(MLA_Q_RANK, -1)], axis=1).astype(BF16)
    kvw = kv_up.reshape(MLA_KV_RANK, MLA_HEADS, MLA_NOPE + MLA_V)
    wkv = jnp.concatenate([kvw[:, :, :MLA_NOPE].reshape(MLA_KV_RANK, -1),
                           kvw[:, :, MLA_NOPE:].reshape(MLA_KV_RANK, -1)], axis=1).astype(BF16)
    cosq = jnp.tile(cos, (1, MLA_HEADS))
    sinq = jnp.tile(sin, (1, MLA_HEADS))
    rows = lambda width: pl.BlockSpec((tm, width), lambda bi, i: (i, 0))
    small = lambda a: pl.BlockSpec(a.shape, lambda bi, i: (0, 0))
    qg = q_norm_g.reshape(1, -1)
    kvg = kv_norm_g.reshape(1, -1)
    out = lambda width: (pl.BlockSpec((1, MLA_HEADS, tm, width), lambda bi, i: (bi, 0, i, 0)),
                         jax.ShapeDtypeStruct((b, MLA_HEADS, s, width), BF16))
    outs = [out(LANES), out(LANES), out(MLA_V)]
    return pl.pallas_call(
        _mla_prep_kernel,
        grid=(b, s // tm),
        in_specs=[pl.BlockSpec((1, tm, MLA_PAD), lambda bi, i: (bi, i, 0)),
                  rows(MLA_HEADS * MLA_ROPE), rows(MLA_HEADS * MLA_ROPE), rows(MLA_ROPE), rows(MLA_ROPE),
                  small(qg), small(kvg), small(wq), small(wkv)],
        out_specs=[o[0] for o in outs],
        out_shape=[o[1] for o in outs],
        compiler_params=_cparams(2),
        name="mla_prep",
    )(pat, cosq, sinq, cos, sin, qg, kvg, wq, wkv)


def _attn_kernel(q_ref, k_ref, v_ref, o_ref):
    outs = []
    for h in range(MLA_HEADS):
        s = lax.dot_general(q_ref[0, h], k_ref[0, h], (((1,), (1,)), ((), ())),
                            preferred_element_type=F32)
        p = jnp.exp2(s - jnp.max(s, axis=-1, keepdims=True))
        denom = jnp.sum(p, axis=-1, keepdims=True)
        outs.append(jnp.dot(p.astype(BF16), v_ref[0, h], preferred_element_type=F32) / denom)
    o_ref[0] = jnp.concatenate(outs, axis=1)


def _attention(q, k, v, first_row, n_rows, n_keys):
    b, h, _, dq = q.shape
    dv = v.shape[3]
    tq = min(n_rows, ATTN_Q_TILE)
    assert first_row % tq == 0 and n_rows % tq == 0
    return pl.pallas_call(
        _attn_kernel,
        grid=(b, n_rows // tq),
        in_specs=[pl.BlockSpec((1, h, tq, dq), lambda bi, i: (bi, 0, i + first_row // tq, 0)),
                  pl.BlockSpec((1, h, n_keys, dq), lambda bi, i: (bi, 0, 0, 0)),
                  pl.BlockSpec((1, h, n_keys, dv), lambda bi, i: (bi, 0, 0, 0))],
        out_specs=pl.BlockSpec((1, tq, h * dv), lambda bi, i: (bi, i, 0)),
        out_shape=jax.ShapeDtypeStruct((b, n_rows, h * dv), F32),
        compiler_params=_cparams(2),
        name="mla_attention",
    )(q, k, v)


def _mla(q, k, v, n_ctx, need_ctx):
    s = q.shape[2]
    y_lat = _attention(q, k, v, n_ctx, s - n_ctx, s)
    y_ctx = _attention(q, k, v, 0, n_ctx, n_ctx) if need_ctx else None
    return y_ctx, y_lat


def _out_proj_kernel(x_ref, mod_ref, hf_ref, hb_ref, o_ref, yf_ref, yb_ref, bonus_ref, g_ref, at_ref,
                     g4_ref, g6_ref, mlg_ref, gng_ref, gnb_ref, wml_ref, wrw_ref, wat_ref,
                     lng_ref, lnb_ref, xo_ref, ho_ref):
    mod = mod_ref[0, 0]
    ml = _sigmoid(o_ref[0]) * _group_ln(hf_ref[0] + hb_ref[0], g4_ref[...], LN_EPS) * mlg_ref[...]
    z = _group_ln(yf_ref[0] + yb_ref[0], g6_ref[...], RW_GN_EPS) * gng_ref[...] + gnb_ref[...]
    rw = (z + bonus_ref[0]) * g_ref[0]
    dot = functools.partial(jnp.dot, preferred_element_type=F32)
    y = (dot(ml.astype(BF16), wml_ref[...]) + dot(rw.astype(BF16), wrw_ref[...])
         + dot(at_ref[0].astype(BF16), wat_ref[...]))
    x = _ln_rows(DEEPNORM_ALPHA * x_ref[0] + mod[2:3] * y) * lng_ref[...] + lnb_ref[...]
    xo_ref[0] = x
    ho_ref[...] = (_ln_rows(x) * (1.0 + mod[4:5]) + mod[3:4]).T.astype(BF16)


def _out_proj(x_all, mods, pml, hf, hb, yf, yb, bonus, g, y_at, norm_g, gn_g, gn_b, w_out,
              ln_g, ln_b, n_ctx_tiles, first_tile):
    b, s, d = x_all.shape
    tm = ROW_TILE
    n_tiles = s // tm - first_tile
    seg = lambda i: (i + first_tile >= n_ctx_tiles).astype(jnp.int32)
    rows = lambda width, blk=0: pl.BlockSpec((1, tm, width), lambda bi, i: (bi, i + first_tile, blk))
    small = lambda a: pl.BlockSpec(a.shape, lambda bi, i: (0, 0))
    wb = w_out.astype(BF16)
    consts = [_group_matrix(ML_W, ML_DH, 1.0 / ML_DH), _group_matrix(RW_W, RW_DH, 1.0 / RW_DH),
              norm_g.reshape(1, -1), gn_g.reshape(1, -1), gn_b.reshape(1, -1),
              wb[:ML_W], wb[ML_W:ML_W + RW_W], wb[ML_W + RW_W:],
              ln_g.reshape(1, -1), ln_b.reshape(1, -1)]
    out_rows = lambda: pl.BlockSpec((1, tm, d), lambda bi, i: (bi, i, 0))
    return pl.pallas_call(
        _out_proj_kernel,
        grid=(b, n_tiles),
        in_specs=[rows(d), pl.BlockSpec((1, 1, 6, d), lambda bi, i: (bi, seg(i), 0, 0)),
                  rows(ML_W), rows(ML_W), rows(ML_W, 3 * ML_W // ML_W),
                  rows(RW_W), rows(RW_W), rows(RW_W), rows(RW_W), rows(MLA_W)]
                 + [small(a) for a in consts],
        out_specs=[out_rows(), pl.BlockSpec((d, tm), lambda bi, i: (0, bi * n_tiles + i))],
        out_shape=[jax.ShapeDtypeStruct((b, n_tiles * tm, d), F32),
                   jax.ShapeDtypeStruct((d, b * n_tiles * tm), BF16)],
        compiler_params=_cparams(2),
        name="out_proj",
    )(x_all, mods, hf, hb, pml, yf, yb, bonus, g, y_at, *consts)


def _final_norm_kernel(x_ref, y_ref, mod_ref, g_ref, b_ref, o_ref):
    mod = mod_ref[0, 0]
    y = y_ref[...].T
    o_ref[0] = _ln_rows(DEEPNORM_ALPHA * x_ref[0] + mod[5:6] * y) * g_ref[...] + b_ref[...]


def _final_norm(x, y_t, mods, ln_g, ln_b, n_ctx_tiles, first_tile):
    b, s, d = x.shape
    tm = ROW_TILE
    n_tiles = s // tm
    seg = lambda i: (i + first_tile >= n_ctx_tiles).astype(jnp.int32)
    rows = pl.BlockSpec((1, tm, d), lambda bi, i: (bi, i, 0))
    cols = pl.BlockSpec((d, tm), lambda bi, i: (0, bi * n_tiles + i))
    small = pl.BlockSpec((1, d), lambda bi, i: (0, 0))
    return pl.pallas_call(
        _final_norm_kernel,
        grid=(b, n_tiles),
        in_specs=[rows, cols, pl.BlockSpec((1, 1, 6, d), lambda bi, i: (bi, seg(i), 0, 0)), small, small],
        out_specs=rows,
        out_shape=jax.ShapeDtypeStruct((b, s, d), F32),
        compiler_params=_cparams(2),
        name="final_norm",
    )(x, y_t, mods, ln_g.reshape(1, d), ln_b.reshape(1, d))


def _peer_scores_kernel(ht_ref, wq_ref, keys_ref, s_ref):
    q = jnp.dot(wq_ref[...], ht_ref[...], preferred_element_type=F32).astype(BF16)
    for g in range(2 * PEER_HEADS):
        s_ref[g] = jnp.dot(keys_ref[g], q[g * PEER_DK:(g + 1) * PEER_DK], preferred_element_type=F32)


def _peer_scores(ht, wq_t, keys, tn):
    d, n = ht.shape
    groups = 2 * PEER_HEADS
    return pl.pallas_call(
        _peer_scores_kernel,
        grid=(n // tn,),
        in_specs=[pl.BlockSpec((d, tn), lambda t: (0, t)),
                  pl.BlockSpec(wq_t.shape, lambda t: (0, 0)),
                  pl.BlockSpec(keys.shape, lambda t: (0, 0, 0))],
        out_specs=pl.BlockSpec((groups, PEER_NKEYS, tn), lambda t: (0, 0, t)),
        out_shape=jax.ShapeDtypeStruct((groups, PEER_NKEYS, n), F32),
        compiler_params=_cparams(1),
        name="peer_scores",
    )(ht, wq_t, keys)


PEER_RANKS = PEER_TOPK + 1


PEER_UNRANKED = float(PEER_NKEYS - 1)


def _top_rows(s, count):
    rows = []
    rank = jnp.full(s.shape, PEER_UNRANKED, F32)
    for r in range(count):
        m = jnp.max(s, axis=0, keepdims=True)
        rows.append(m)
        hit = s == m
        rank = jnp.where(hit, float(r), rank)
        s = jnp.where(hit, -jnp.inf, s)
    return rows, rank


def _peer_route_kernel(s_ref, rank2_ref, e2_ref, n1_ref, c2_ref, cand_ref):
    cand_ref[...] = jnp.full(cand_ref.shape, -jnp.inf, F32)
    for h in range(PEER_HEADS):
        s1 = s_ref[2 * h]
        s2 = s_ref[2 * h + 1]
        a, _ = _top_rows(s1, PEER_RANKS)
        b, rank2 = _top_rows(s2, PEER_RANKS)
        pairs = [(p, q) for p in range(PEER_RANKS) for q in range(PEER_RANKS // (p + 1))]
        for slot, (p, q) in enumerate(pairs):
            cand_ref[slot:slot + 1, :] = a[p] + b[q]
        top, _ = _top_rows(cand_ref[...], PEER_RANKS)
        threshold = 0.5 * (top[PEER_TOPK - 1] + top[PEER_TOPK])
        z = sum(jnp.exp(row - top[0]) for row in top[:PEER_TOPK])
        need = threshold - s1
        n1 = jnp.zeros_like(s1)
        for q in range(PEER_RANKS):
            n1 = jnp.where(b[q] >= need, float(q + 1), n1)
        rank2_ref[h] = rank2.astype(BF16)
        e2_ref[h] = jnp.exp(s2 - b[0]).astype(BF16)
        n1_ref[h] = n1
        c2_ref[h] = jnp.exp(s1 - a[0]) / z


PEER_CAND_ROWS = 56


def _peer_route(scores, tn):
    groups, nk, n = scores.shape
    spec = pl.BlockSpec((PEER_HEADS, nk, tn), lambda t: (0, 0, t))
    shape = lambda dt: jax.ShapeDtypeStruct((PEER_HEADS, nk, n), dt)
    return pl.pallas_call(
        _peer_route_kernel,
        grid=(n // tn,),
        in_specs=[pl.BlockSpec((groups, nk, tn), lambda t: (0, 0, t))],
        out_specs=[spec] * 4,
        out_shape=[shape(BF16), shape(BF16), shape(F32), shape(F32)],
        scratch_shapes=[pltpu.VMEM((PEER_CAND_ROWS, tn), F32)],
        compiler_params=_cparams(1),
        name="peer_route",
    )(scores)


def _gelu_tanh(x):
    k = -2.0 * math.sqrt(2.0 / math.pi)
    return x / (1.0 + jnp.exp(x * (k + (k * 0.044715) * (x * x))))


def _peer_experts_kernel(blocks, ht_ref, u_ref, vt_ref, rank2_ref, e2_ref, n1_ref, c2_ref, o_ref, w_ref):
    e = pl.program_id(1)

    @pl.when(e == 0)
    def _():
        o_ref[...] = jnp.zeros_like(o_ref)

    ht = ht_ref[...]
    for ii in range(blocks):
        i = e * blocks + ii
        rows = slice(ii * PEER_NKEYS, (ii + 1) * PEER_NKEYS)
        act = _gelu_tanh(jnp.dot(u_ref[rows, :], ht, preferred_element_type=F32).astype(BF16))
        tn = act.shape[1]
        packed = (PEER_NKEYS // BF16_SUBLANES, BF16_SUBLANES, tn)
        gate = jnp.zeros(packed, BF16)
        for h in range(PEER_HEADS):
            n1 = jnp.broadcast_to(n1_ref[h, pl.ds(i, 1), :], (BF16_SUBLANES, tn)).astype(BF16)
            c2 = jnp.broadcast_to(c2_ref[h, pl.ds(i, 1), :], (BF16_SUBLANES, tn)).astype(BF16)
            gate = gate + jnp.where(rank2_ref[h].reshape(packed) < n1, e2_ref[h].reshape(packed) * c2,
                                    jnp.zeros((), BF16))
        w_ref[rows, :] = gate.reshape(act.shape) * act
    o_ref[...] += jnp.dot(vt_ref[...], w_ref[...], preferred_element_type=F32)


def _peer_experts(ht, u, vt, rank2, e2, n1, c2, tn, ec):
    d, n = ht.shape
    blocks = ec // PEER_NKEYS
    route = pl.BlockSpec((PEER_HEADS, PEER_NKEYS, tn), lambda t, e: (0, 0, t))
    return pl.pallas_call(
        functools.partial(_peer_experts_kernel, blocks),
        grid=(n // tn, PEER_EXPERTS // ec),
        in_specs=[pl.BlockSpec((d, tn), lambda t, e: (0, t)),
                  pl.BlockSpec((ec, d), lambda t, e: (e, 0)),
                  pl.BlockSpec((d, ec), lambda t, e: (0, e)),
                  route, route, route, route],
        out_specs=pl.BlockSpec((d, tn), lambda t, e: (0, t)),
        out_shape=jax.ShapeDtypeStruct((d, n), F32),
        scratch_shapes=[pltpu.VMEM((ec, tn), BF16)],
        compiler_params=_cparams(2),
        name="peer_experts",
    )(ht, u, vt, rank2, e2, n1, c2)


def _peer(ht, w_q, keys, u_tab, v_tab):
    n = ht.shape[1]
    tn = min(n, PEER_TOKEN_TILE)
    wq_t = w_q.T.astype(BF16)
    kb = keys.reshape(2 * PEER_HEADS, PEER_NKEYS, PEER_DK).astype(BF16)
    scores = _peer_scores(ht, wq_t, kb, tn)
    rank2, e2, n1, c2 = _peer_route(scores, min(n, PEER_ROUTE_TILE))
    return _peer_experts(ht, u_tab.astype(BF16), v_tab.T.astype(BF16), rank2, e2, n1, c2, tn,
                         PEER_EXPERT_BLOCK)


def _rope_tables(t, n_ctx):
    rows = t // GRID_W
    row = np.repeat(np.arange(rows), GRID_W).astype(np.float32)
    col = np.tile(np.arange(GRID_W), rows).astype(np.float32)
    inv_freq = jnp.asarray(ROPE_THETA, F32) ** (-jnp.arange(ROPE_AXIS_FREQS, dtype=F32) / ROPE_AXIS_FREQS)
    ang = jnp.concatenate([jnp.asarray(row)[:, None] * inv_freq] * 2
                          + [jnp.asarray(col)[:, None] * inv_freq] * 2, axis=1)
    pad = lambda a, fill: jnp.concatenate([jnp.full((n_ctx, MLA_ROPE), fill, F32), a], axis=0)
    return pad(jnp.cos(ang), 1.0), pad(jnp.sin(ang), 0.0)


def _split_w_in(w_in):
    ml_cols = 4 * ML_W + ML_GATE_COLS
    w_ml = jnp.pad(w_in[:, :ml_cols], ((0, 0), (0, ML_PAD - ml_cols)))
    w_rw = w_in[:, ml_cols:ml_cols + RW_COLS]
    w_at = w_in[:, ml_cols + RW_COLS:]
    k_rope = w_at[:, MLA_Q_RANK + MLA_KV_RANK:]
    w_at = jnp.concatenate([w_at, _rot_cols(k_rope)], axis=1)
    w_at = jnp.pad(w_at, ((0, 0), (0, MLA_PAD - w_at.shape[1])))
    return w_ml.astype(BF16), w_rw.astype(BF16), w_at.astype(BF16)


def kernel(x, c, ctx, c_ctx, w_mod, b_mod, w_in, ml_conv_w, ml_conv_b, ml_i_bias, ml_f_bias, ml_norm_g,
           rw_mu, rw_w0, rw_w_up, rw_a0, rw_a_up, rw_g_up, rw_k_k, rw_k_a, rw_r_k, rw_gn_g, rw_gn_b,
           mla_q_norm_g, mla_q_up, mla_kv_norm_g, mla_kv_up, w_out, ln_mix_g, ln_mix_b,
           peer_w_q, peer_keys, peer_u, peer_v, ln_ffn_g, ln_ffn_b):
    b, t, d = x.shape
    n_ctx = ctx.shape[1]
    assert n_ctx % ROW_TILE == 0 and t % ROW_TILE == 0 and d == D_MODEL
    n_ctx_tiles = n_ctx // ROW_TILE
    cos, sin = _rope_tables(t, n_ctx)
    mod_rows = -(-(b + 1) // SUBLANES) * SUBLANES
    cvec = jnp.concatenate([c, c_ctx[None], jnp.zeros((mod_rows - b - 1, d), F32)], axis=0)

    x_all = jnp.concatenate([ctx, x], axis=1)
    depth = w_mod.shape[0]
    for l in range(depth):
        need_ctx = l < depth - 1
        mod = _modulation(cvec, w_mod[l], b_mod[l]).reshape(mod_rows, 6, d)
        mods = jnp.stack([jnp.broadcast_to(mod[b], (b, 6, d)), mod[:b]], axis=1)

        pml, prw, pat = _in_proj(x_all, mods, *_split_w_in(w_in[l]), n_ctx_tiles)

        q, k, lg = _ml_prep(pml, ml_conv_w[l], ml_conv_b[l], ml_i_bias[l], ml_f_bias[l], n_ctx_tiles)
        hf, hb = _ml_scan(q, k, pml, lg, n_ctx)

        r, kh, v, g, bonus, w, ab, kt = _rw_prep(
            prw, rw_mu[l], rw_w0[l], rw_w_up[l], rw_a0[l], rw_a_up[l], rw_g_up[l], rw_k_k[l], rw_k_a[l],
            rw_r_k[l], n_ctx_tiles)
        yf, yb = _rw_scan(r, kh, v, w, ab, kt, n_ctx)

        qa, ka, va = _mla_prep(pat, mla_q_norm_g[l], mla_q_up[l], mla_kv_norm_g[l], mla_kv_up[l], cos, sin)
        at_ctx, at_lat = _mla(qa, ka, va, n_ctx, need_ctx)
        first_tile = 0 if need_ctx else n_ctx_tiles
        y_at = jnp.concatenate([at_ctx, at_lat], axis=1) if need_ctx else jnp.pad(
            at_lat, ((0, 0), (n_ctx, 0), (0, 0)))

        x_mid, h_ffn = _out_proj(x_all, mods, pml, hf, hb, yf, yb, bonus, g, y_at, ml_norm_g[l], rw_gn_g[l],
                                 rw_gn_b[l], w_out[l], ln_mix_g[l], ln_mix_b[l], n_ctx_tiles, first_tile)
        y = _peer(h_ffn, peer_w_q[l], peer_keys[l], peer_u[l], peer_v[l])
        x_all = _final_norm(x_mid, y, mods, ln_ffn_g[l], ln_ffn_b[l], n_ctx_tiles, first_tile)
    return x_all
```
